```python
import jax
import jax.numpy as jnp
from jax import lax
import numpy as np

D_MODEL = 1024
BATCH = 2
SEQ = 8192
DEPTH = 2

MIX_WIDTH = D_MODEL // 2
N_BRANCH = 3
NORM_EPS = 1e-6

SGU_GROUPS = 4
SGU_CHUNK = 128
SGU_WIDTH = MIX_WIDTH
SGU_GROUP_DIM = SGU_WIDTH // SGU_GROUPS

SWA_HEADS = 8
SWA_KV_HEADS = 2
SWA_HEAD_DIM = MIX_WIDTH // SWA_HEADS
SWA_GROUP = SWA_HEADS // SWA_KV_HEADS
WINDOW = 128
ROPE_THETA = 500000.0
ROPE_DIM = SWA_HEAD_DIM // 4

DN_HEADS = 4
DN_HEAD_DIM = MIX_WIDTH // DN_HEADS
DN_CONV = 4
DN_CHUNK = 64

D_FF = ((8 * D_MODEL // 3 + 255) // 256) * 256

IN_WIDTHS = (SGU_WIDTH, SGU_WIDTH,
             SWA_HEADS * SWA_HEAD_DIM, SWA_KV_HEADS * SWA_HEAD_DIM, SWA_KV_HEADS * SWA_HEAD_DIM,
             3 * MIX_WIDTH, MIX_WIDTH, DN_HEADS, DN_HEADS,
             N_BRANCH * D_MODEL)
IN_COLS = sum(IN_WIDTHS)

kernel_name = 'hybrid_gated_parallel_mixers'


def rmsnorm(x, g):
    xf = x.astype(jnp.float32)
    y = xf * lax.rsqrt(jnp.mean(xf * xf, axis=-1, keepdims=True) + NORM_EPS)
    return (y * g.astype(jnp.float32)).astype(x.dtype)


def layernorm(x, g, b):
    xf = x.astype(jnp.float32)
    xc = xf - jnp.mean(xf, axis=-1, keepdims=True)
    y = xc * lax.rsqrt(jnp.mean(xc * xc, axis=-1, keepdims=True) + NORM_EPS)
    return (y * g.astype(jnp.float32) + b.astype(jnp.float32)).astype(x.dtype)


def l2norm(x):
    return x * lax.rsqrt(jnp.sum(x * x, axis=-1, keepdims=True) + NORM_EPS)


def split_columns(t):
    parts, start = [], 0
    for w in IN_WIDTHS:
        parts.append(t[..., start:start + w])
        start += w
    return parts


def rotary_tables(positions):
    inv_freq = ROPE_THETA ** (-jnp.arange(0, ROPE_DIM, 2, dtype=jnp.float32) / ROPE_DIM)
    ang = positions.astype(jnp.float32)[..., None] * inv_freq
    return jnp.cos(ang)[:, :, None, :], jnp.sin(ang)[:, :, None, :]


def apply_partial_rope(x, cos, sin):
    half = ROPE_DIM // 2
    x1, x2, rest = x[..., :half], x[..., half:ROPE_DIM], x[..., ROPE_DIM:]
    c, s = cos.astype(x.dtype), sin.astype(x.dtype)
    return jnp.concatenate([x1 * c - x2 * s, x2 * c + x1 * s, rest], axis=-1)


def spatial_gating(u, v, ln_g, ln_b, w_s, b_s):
    B_, S_ = u.shape[:2]
    nc = S_ // SGU_CHUNK
    vn = layernorm(v, ln_g, ln_b).reshape(B_, nc, SGU_CHUNK, SGU_GROUPS, SGU_GROUP_DIM)
    causal = jnp.tril(jnp.ones((SGU_CHUNK, SGU_CHUNK), dtype=bool))
    w_causal = jnp.where(causal, w_s, 0.0).astype(vn.dtype)
    mixed = jnp.einsum('gts,bnsgc->bntgc', w_causal, vn) + b_s.T.astype(vn.dtype)[None, None, :, :, None]
    return u * mixed.reshape(B_, S_, SGU_WIDTH)


def sliding_window_attention(q, k, v, sinks, cos, sin):
    B_, S_ = q.shape[:2]
    nc = S_ // WINDOW
    q = apply_partial_rope(q, cos, sin) * (SWA_HEAD_DIM ** -0.5)
    k = apply_partial_rope(k, cos, sin)
    qb = q.reshape(B_, nc, WINDOW, SWA_KV_HEADS, SWA_GROUP, SWA_HEAD_DIM)

    def band(t):
        cur = t.reshape(B_, nc, WINDOW, SWA_KV_HEADS, SWA_HEAD_DIM)
        prev = jnp.concatenate([jnp.zeros_like(cur[:, :1]), cur[:, :-1]], axis=1)
        return jnp.concatenate([prev, cur], axis=2)

    kb, vb = band(k), band(v)
    logits = jnp.einsum('bnqkgd,bnskd->bnkgqs', qb, kb).astype(jnp.float32)
    qi = jnp.arange(WINDOW)[:, None]
    sj = jnp.arange(2 * WINDOW)[None, :]
    diff = qi + WINDOW - sj
    in_band = (diff >= 0) & (diff < WINDOW)
    valid = (jnp.arange(nc) > 0)[:, None, None] | (sj >= WINDOW)[None]
    mask = in_band[None] & valid
    logits = jnp.where(mask[None, :, None, None], logits, -jnp.inf)
    sink = jnp.broadcast_to(sinks.astype(jnp.float32).reshape(1, 1, SWA_KV_HEADS, SWA_GROUP, 1, 1),
                            logits.shape[:-1] + (1,))
    probs = jax.nn.softmax(jnp.concatenate([logits, sink], axis=-1), axis=-1)[..., :-1]
    out = jnp.einsum('bnkgqs,bnskd->bnqkgd', probs.astype(vb.dtype), vb)
    return out.reshape(B_, S_, SWA_HEADS * SWA_HEAD_DIM)


def causal_short_conv(x, w):
    S_ = x.shape[1]
    xp = jnp.pad(x, ((0, 0), (DN_CONV - 1, 0), (0, 0)))
    out = xp[:, 0:S_] * w[0]
    for i in range(1, DN_CONV):
        out = out + xp[:, i:i + S_] * w[i]
    return jax.nn.silu(out)


def gated_deltanet(qkv, z, beta_logit, a_logit, conv_w, a_log, dt_bias, norm_g):
    B_, S_ = qkv.shape[:2]
    in_dtype = qkv.dtype
    nt = S_ // DN_CHUNK
    H, hd, C = DN_HEADS, DN_HEAD_DIM, DN_CHUNK
    qkv = causal_short_conv(qkv, conv_w).astype(jnp.float32)
    q = l2norm(qkv[..., :MIX_WIDTH].reshape(B_, S_, H, hd)) * (hd ** -0.5)
    k = l2norm(qkv[..., MIX_WIDTH:2 * MIX_WIDTH].reshape(B_, S_, H, hd))
    v = qkv[..., 2 * MIX_WIDTH:].reshape(B_, S_, H, hd)
    beta = jax.nn.sigmoid(beta_logit.astype(jnp.float32))
    g = -jnp.exp(a_log.astype(jnp.float32)) * jax.nn.softplus(a_logit.astype(jnp.float32) + dt_bias.astype(jnp.float32))

    def to_chunks(t):
        t = t.reshape((B_, nt, C) + t.shape[2:])
        return jnp.swapaxes(jnp.swapaxes(t, 0, 1), 2, 3)

    q, k, v, beta, g = to_chunks(q), to_chunks(k), to_chunks(v), to_chunks(beta), to_chunks(g)
    gc = jnp.cumsum(g, axis=-1)
    tril = jnp.tril(jnp.ones((C, C), dtype=bool))
    strict = jnp.tril(jnp.ones((C, C), dtype=bool), -1)
    decay = jnp.exp(jnp.where(tril, gc[..., :, None] - gc[..., None, :], -jnp.inf))
    k_beta = k * beta[..., None]
    lower = jnp.where(strict, jnp.einsum('nbhid,nbhjd->nbhij', k_beta, k) * decay, 0.0)
    a_mat = lower + jnp.eye(C, dtype=jnp.float32)
    rhs = jnp.concatenate([v * beta[..., None], k_beta * jnp.exp(gc)[..., None]], axis=-1)
    sol = lax.linalg.triangular_solve(a_mat, rhs, left_side=True, lower=True, unit_diagonal=True)
    u_c, w_c = sol[..., :hd], sol[..., hd:]
    attn = jnp.einsum('nbhid,nbhjd->nbhij', q, k) * decay
    q_dec = q * jnp.exp(gc)[..., None]
    k_dec = k * jnp.exp(gc[..., -1:] - gc)[..., None]
    c_dec = jnp.exp(gc[..., -1])

    def step(state, xs):
        qd, wc, uc, at, kd, cd = xs
        v_new = uc - jnp.einsum('bhcd,bhde->bhce', wc, state)
        o_c = jnp.einsum('bhcd,bhde->bhce', qd, state) + jnp.einsum('bhij,bhje->bhie', at, v_new)
        state = state * cd[..., None, None] + jnp.einsum('bhcd,bhce->bhde', kd, v_new)
        return state, o_c

    state0 = jnp.zeros((B_, H, hd, hd), dtype=jnp.float32)
    _, o = lax.scan(step, state0, (q_dec, w_c, u_c, attn, k_dec, c_dec))
    o = jnp.swapaxes(jnp.swapaxes(o, 0, 1), 2, 3).reshape(B_, S_, H, hd)
    o = rmsnorm(o, norm_g) * jax.nn.silu(z.astype(jnp.float32).reshape(B_, S_, H, hd))
    return o.reshape(B_, S_, MIX_WIDTH).astype(in_dtype)


def setup_inputs(seed: int = 0) -> dict:
    key = jax.random.key(seed)
    ks = jax.random.split(key, 20)
    f32 = jnp.float32

    def nrm(k, shape, scale):
        return jax.random.normal(k, shape, dtype=f32) * scale

    dt = jnp.exp(jax.random.uniform(ks[11], (DEPTH, DN_HEADS), dtype=f32,
                                    minval=np.log(1e-3), maxval=np.log(1e-1)))
    return {
        'x': nrm(ks[0], (BATCH, SEQ, D_MODEL), 1.0),
        'positions': jnp.broadcast_to(jnp.arange(SEQ, dtype=jnp.int32), (BATCH, SEQ)),
        'attn_norm': 1.0 + nrm(ks[1], (DEPTH, D_MODEL), 0.02),
        'w_in': nrm(ks[2], (DEPTH, D_MODEL, IN_COLS), D_MODEL ** -0.5),
        'sgu_ln_g': 1.0 + nrm(ks[3], (DEPTH, SGU_WIDTH), 0.02),
        'sgu_ln_b': nrm(ks[4], (DEPTH, SGU_WIDTH), 0.02),
        'sgu_w': nrm(ks[5], (DEPTH, SGU_GROUPS, SGU_CHUNK, SGU_CHUNK), SGU_CHUNK ** -0.5),
        'sgu_b': 1.0 + nrm(ks[6], (DEPTH, SGU_GROUPS, SGU_CHUNK), 0.02),
        'attn_sinks': nrm(ks[7], (DEPTH, SWA_HEADS), 0.5),
        'dn_conv_w': nrm(ks[8], (DEPTH, DN_CONV, 3 * MIX_WIDTH), DN_CONV ** -0.5),
        'dn_a_log': jnp.log(jax.random.uniform(ks[9], (DEPTH, DN_HEADS), dtype=f32, minval=1.0, maxval=16.0)),
        'dn_dt_bias': dt + jnp.log(-jnp.expm1(-dt)),
        'dn_norm': 1.0 + nrm(ks[10], (DEPTH, DN_HEAD_DIM), 0.02),
        'w_branch': nrm(ks[12], (DEPTH, N_BRANCH, MIX_WIDTH, D_MODEL), MIX_WIDTH ** -0.5),
        'w_out': nrm(ks[13], (DEPTH, D_MODEL, D_MODEL), D_MODEL ** -0.5),
        'ffn_norm': 1.0 + nrm(ks[14], (DEPTH, D_MODEL), 0.02),
        'w_gate_up': nrm(ks[15], (DEPTH, D_MODEL, 2 * D_FF), D_MODEL ** -0.5),
        'w_down': nrm(ks[16], (DEPTH, D_FF, D_MODEL), D_FF ** -0.5),
        'final_norm': 1.0 + nrm(ks[17], (D_MODEL,), 0.02),
    }


def reference(x, positions, attn_norm, w_in, sgu_ln_g, sgu_ln_b, sgu_w, sgu_b, attn_sinks,
              dn_conv_w, dn_a_log, dn_dt_bias, dn_norm, w_branch, w_out, ffn_norm,
              w_gate_up, w_down, final_norm):
    B_, S_ = x.shape[:2]
    cos, sin = rotary_tables(positions)
    for layer in range(DEPTH):
        h = rmsnorm(x, attn_norm[layer])
        proj = jnp.einsum('bsd,dc->bsc', h, w_in[layer])
        u_a, v_a, q_b, k_b, v_b, qkv_c, z_c, beta_c, a_c, gate_pre = split_columns(proj)
        out_a = spatial_gating(jax.nn.gelu(u_a), jax.nn.gelu(v_a), sgu_ln_g[layer], sgu_ln_b[layer],
                               sgu_w[layer], sgu_b[layer])
        out_b = sliding_window_attention(q_b.reshape(B_, S_, SWA_HEADS, SWA_HEAD_DIM),
                                         k_b.reshape(B_, S_, SWA_KV_HEADS, SWA_HEAD_DIM),
                                         v_b.reshape(B_, S_, SWA_KV_HEADS, SWA_HEAD_DIM),
                                         attn_sinks[layer], cos, sin)
        out_c = gated_deltanet(qkv_c, z_c, beta_c, a_c, dn_conv_w[layer], dn_a_log[layer],
                               dn_dt_bias[layer], dn_norm[layer])
        branches = jnp.stack([out_a, out_b, out_c], axis=0)
        branch_d = jnp.einsum('nbsc,ncd->nbsd', branches, w_branch[layer])
        gates = jax.nn.sigmoid(gate_pre.reshape(B_, S_, N_BRANCH, D_MODEL))
        merged = jnp.einsum('bsnd,nbsd->bsd', gates, branch_d)
        x = x + jnp.einsum('bsd,de->bse', merged, w_out[layer])
        h2 = rmsnorm(x, ffn_norm[layer])
        gu = jnp.einsum('bsd,df->bsf', h2, w_gate_up[layer])
        x = x + jnp.einsum('bsf,fd->bsd', jax.nn.silu(gu[..., :D_FF]) * gu[..., D_FF:], w_down[layer])
    return rmsnorm(x, final_norm)
```

```python
import functools

import jax
import jax.numpy as jnp
from jax import lax
from jax.experimental import pallas as pl
from jax.experimental.pallas import tpu as pltpu

D_MODEL = 1024
MIX = 512
NORM_EPS = 1e-6

SGU_GROUPS = 4
BLOCK = 128

SWA_HEADS = 8
SWA_KV = 2
SWA_HD = 64
ROPE_THETA = 500000.0
ROPE_DIM = 16

DN_HEADS = 4
DN_HD = 128
DN_CONV = 4
DN_CHUNK = 64

D_FF = 2816

LANES = 128
SUBLANES = 8

C_UV = 0
C_QKVB = C_UV + 2 * MIX
C_QKVC = C_QKVB + MIX + 2 * SWA_KV * SWA_HD
C_Z = C_QKVC + 3 * MIX
C_BA = C_Z + MIX
C_GATE = C_BA + LANES
IN_COLS_PACKED = C_GATE + 3 * D_MODEL

MIXER_TILE = 256
FFN_TILE = 256
VMEM_LIMIT_BYTES = 56 * 1024 * 1024

BF16 = jnp.bfloat16
F32 = jnp.float32


def _dot(a, b):
    return jnp.dot(a.astype(BF16), b.astype(BF16), preferred_element_type=F32)


def _dot_nt(a, b):
    return lax.dot_general(a.astype(BF16), b.astype(BF16), (((1,), (1,)), ((), ())),
                           preferred_element_type=F32)


def _dot_tn(a, b):
    return lax.dot_general(a.astype(BF16), b.astype(BF16), (((0,), (0,)), ((), ())),
                           preferred_element_type=F32)


def _rmsnorm(x, g):
    return x * lax.rsqrt(jnp.mean(x * x, axis=-1, keepdims=True) + NORM_EPS) * g


def _gelu_tanh(x):
    return 0.5 * x * (1.0 + jnp.tanh(0.7978845608028654 * (x + 0.044715 * (x * x * x))))


def _silu(x):
    return x * jax.nn.sigmoid(x)


def _softplus(x):
    return jnp.maximum(x, 0.0) + jnp.log1p(jnp.exp(-jnp.abs(x)))


def _iota(shape, dim):
    return lax.broadcasted_iota(jnp.int32, shape, dim)


def _spatial_gating(u_pre, v_pre, ln_g, ln_b, sguw_ref, sgub_ref):
    t = u_pre.shape[0]
    u = _gelu_tanh(u_pre)
    v = _gelu_tanh(v_pre)
    vc = v - jnp.mean(v, axis=-1, keepdims=True)
    vn = vc * lax.rsqrt(jnp.mean(vc * vc, axis=-1, keepdims=True) + NORM_EPS) * ln_g + ln_b
    causal = _iota((BLOCK, BLOCK), 0) >= _iota((BLOCK, BLOCK), 1)
    groups = []
    for g in range(SGU_GROUPS):
        w = jnp.where(causal, sguw_ref[g], 0.0)
        bias = sgub_ref[:, g:g + 1]
        cols = slice(g * LANES, (g + 1) * LANES)
        blocks = []
        for blk in range(t // BLOCK):
            rows = slice(blk * BLOCK, (blk + 1) * BLOCK)
            blocks.append(u[rows, cols] * (_dot(w, vn[rows, cols]) + bias))
        groups.append(jnp.concatenate(blocks, axis=0))
    return jnp.concatenate(groups, axis=1)


def _rope(x, cos, sin_a, sin_b):
    n = x.shape[1] // LANES
    if n > 1:
        cos = jnp.concatenate([cos] * n, axis=1)
        sin_a = jnp.concatenate([sin_a] * n, axis=1)
        sin_b = jnp.concatenate([sin_b] * n, axis=1)
    half = ROPE_DIM // 2
    width = x.shape[1]
    return (x * cos + pltpu.roll(x, width - half, 1) * sin_a + pltpu.roll(x, half, 1) * sin_b)


def _sliding_window_attention(q, k, v, cos, sin_a, sin_b, sinks_ref, kv_scr, seq_start):
    t = q.shape[0]
    q = _rope(q, cos, sin_a, sin_b) * (SWA_HD ** -0.5)
    k = _rope(k, cos, sin_a, sin_b)

    @pl.when(seq_start)
    def _():
        kv_scr[...] = jnp.zeros_like(kv_scr)

    kcat = jnp.concatenate([kv_scr[0], k], axis=0)
    vcat = jnp.concatenate([kv_scr[1], v], axis=0)
    kv_scr[0] = k[t - BLOCK:, :]
    kv_scr[1] = v[t - BLOCK:, :]

    lane = _iota((1, LANES), 1)
    lo = lane < SWA_HD
    krot = pltpu.roll(kcat, SWA_HD, 1)
    vrot = pltpu.roll(vcat, SWA_HD, 1)
    kdup = [jnp.where(lo, kcat, krot).astype(BF16), jnp.where(lo, krot, kcat).astype(BF16)]
    vlo = [jnp.where(lo, vcat, 0.0).astype(BF16), jnp.where(lo, vrot, 0.0).astype(BF16)]
    vhi = [jnp.where(lo, 0.0, vrot).astype(BF16), jnp.where(lo, 0.0, vcat).astype(BF16)]

    cur_mask = _iota((BLOCK, BLOCK), 0) >= _iota((BLOCK, BLOCK), 1)
    neg_inf = jnp.float32(-jnp.inf)

    out_blocks = []
    for blk in range(t // BLOCK):
        rows = slice(blk * BLOCK, (blk + 1) * BLOCK)
        keys = slice(blk * BLOCK, blk * BLOCK + 2 * BLOCK)
        prev_valid = jnp.logical_not(seq_start) if blk == 0 else True
        slabs = []
        for slab_idx in range(SWA_HEADS // 2):
            kh = slab_idx // 2
            q_slab = q[rows, slab_idx * LANES:(slab_idx + 1) * LANES]
            acc = None
            for half in range(2):
                head = 2 * slab_idx + half
                sink = sinks_ref[head:head + 1, 0:1]
                qm = jnp.where(lo if half == 0 else jnp.logical_not(lo), q_slab, 0.0)
                logits = _dot_nt(qm, kdup[kh][keys])
                l_prev = logits[:, :BLOCK]
                l_cur = logits[:, BLOCK:]
                if blk == 0:
                    l_prev = jnp.where(prev_valid, l_prev, neg_inf)
                act = jnp.where(cur_mask, l_cur, l_prev)
                m = jnp.maximum(jnp.max(act, axis=1, keepdims=True), sink)
                p = jnp.exp(act - m)
                den = jnp.sum(p, axis=1, keepdims=True) + jnp.exp(sink - m)
                p = p / den
                pcat = jnp.concatenate([jnp.where(cur_mask, 0.0, p), jnp.where(cur_mask, p, 0.0)],
                                       axis=1)
                vv = (vlo if half == 0 else vhi)[kh][keys]
                contrib = _dot(pcat, vv)
                acc = contrib if acc is None else acc + contrib
            slabs.append(acc)
        out_blocks.append(jnp.concatenate(slabs, axis=1))
    return jnp.concatenate(out_blocks, axis=0)


def _chunk_cumsum(x):
    row = _iota(x.shape, 0) % DN_CHUNK
    shift = 1
    while shift < DN_CHUNK:
        x = x + jnp.where(row >= shift, pltpu.roll(x, shift, 0), 0.0)
        shift *= 2
    return x


def _gated_deltanet(qkv, z, ba, convw_ref, alog, dtb, dnorm, conv_scr, state_scr, seq_start):
    t = qkv.shape[0]
    c = DN_CHUNK
    pad = SUBLANES

    @pl.when(seq_start)
    def _():
        conv_scr[0:pad, :] = jnp.zeros((pad, 3 * MIX), F32)
        state_scr[...] = jnp.zeros_like(state_scr)

    conv_scr[pad:pad + t, :] = qkv
    y = None
    for i in range(DN_CONV):
        off = pad - (DN_CONV - 1) + i
        term = conv_scr[off:off + t, :] * convw_ref[i:i + 1, :]
        y = term if y is None else y + term
    conv_scr[0:pad, :] = conv_scr[t:t + pad, :]
    y = _silu(y)

    beta = jax.nn.sigmoid(ba)
    g = -jnp.exp(alog) * _softplus(ba + dtb)
    gc = _chunk_cumsum(g)
    egc = jnp.exp(gc)
    gc_t = gc.T

    tril = _iota((c, c), 0) >= _iota((c, c), 1)
    strict = _iota((c, c), 0) > _iota((c, c), 1)
    neg_inf = jnp.float32(-jnp.inf)

    outs = []
    for h in range(DN_HEADS):
        cols = slice(h * DN_HD, (h + 1) * DN_HD)
        qh = y[:, h * DN_HD:(h + 1) * DN_HD]
        kh = y[:, MIX + h * DN_HD:MIX + (h + 1) * DN_HD]
        vh = y[:, 2 * MIX + h * DN_HD:2 * MIX + (h + 1) * DN_HD]
        qh = qh * lax.rsqrt(jnp.sum(qh * qh, axis=-1, keepdims=True) + NORM_EPS) * (DN_HD ** -0.5)
        kh = kh * lax.rsqrt(jnp.sum(kh * kh, axis=-1, keepdims=True) + NORM_EPS)
        b_col = beta[:, h:h + 1]
        gc_col = gc[:, DN_HEADS + h:DN_HEADS + h + 1]
        e_col = egc[:, DN_HEADS + h:DN_HEADS + h + 1]
        kb = kh * b_col
        vb = vh * b_col
        kbe = kb * e_col
        qd = qh * e_col

        state = state_scr[h]
        o_chunks = []
        for ci in range(t // c):
            rows = slice(ci * c, (ci + 1) * c)
            gcc = gc_col[rows]
            g_last = gcc[c - 1:c, :]
            g_row = gc_t[DN_HEADS + h:DN_HEADS + h + 1, ci * c:(ci + 1) * c]
            decay = jnp.exp(jnp.where(tril, gcc - g_row, neg_inf))
            kc = kh[rows]
            aq = _dot_nt(jnp.concatenate([kb[rows], qh[rows]], axis=0), kc)
            lower = jnp.where(strict, aq[:c] * decay, 0.0)
            attn = aq[c:] * decay
            sol = jnp.concatenate([vb[rows], kbe[rows]], axis=1)
            sol = sol - _dot(lower, sol)
            power = _dot(lower, lower)
            n_sq = 1
            while True:
                sol = sol + _dot(power, sol)
                n_sq *= 2
                if 2 * n_sq >= c:
                    break
                power = _dot(power, power)
            u_c = sol[:, :DN_HD]
            w_c = sol[:, DN_HD:]
            v_new = u_c - _dot(w_c, state)
            o_chunks.append(_dot(qd[rows], state) + _dot(attn, v_new))
            k_dec = kc * jnp.exp(g_last - gcc)
            state = state * jnp.exp(g_last) + _dot_tn(k_dec, v_new)
        state_scr[h] = state
        o = jnp.concatenate(o_chunks, axis=0)
        o = _rmsnorm(o, dnorm) * _silu(z[:, cols])
        outs.append(o)
    return jnp.concatenate(outs, axis=1)


def _mixer_kernel(x_ref, cos_ref, sina_ref, sinb_ref, anorm_ref, win_ref, lng_ref, lnb_ref,
                  sguw_ref, sgub_ref, sinks_ref, convw_ref, alog_ref, dtb_ref, dnorm_ref,
                  wbr_ref, wout_ref, o_ref, kv_scr, conv_scr, state_scr):
    seq_start = pl.program_id(1) == 0
    x = x_ref[...]
    h = _rmsnorm(x, anorm_ref[...]).astype(BF16)

    def proj(lo, width):
        return jnp.dot(h, win_ref[:, lo:lo + width], preferred_element_type=F32)

    uv = proj(C_UV, 2 * MIX)
    out_a = _spatial_gating(uv[:, :MIX], uv[:, MIX:], lng_ref[...], lnb_ref[...],
                            sguw_ref, sgub_ref)

    qkv_b = proj(C_QKVB, MIX + 2 * SWA_KV * SWA_HD)
    out_b = _sliding_window_attention(
        qkv_b[:, :MIX], qkv_b[:, MIX:MIX + LANES], qkv_b[:, MIX + LANES:],
        cos_ref[...], sina_ref[...], sinb_ref[...], sinks_ref, kv_scr, seq_start)

    out_c = _gated_deltanet(proj(C_QKVC, 3 * MIX), proj(C_Z, MIX), proj(C_BA, LANES),
                            convw_ref, alog_ref[...], dtb_ref[...], dnorm_ref[...],
                            conv_scr, state_scr, seq_start)

    merged = None
    for n, branch in enumerate((out_a, out_b, out_c)):
        gate = jax.nn.sigmoid(proj(C_GATE + n * D_MODEL, D_MODEL))
        term = gate * _dot(branch, wbr_ref[n])
        merged = term if merged is None else merged + term
    o_ref[...] = x + _dot(merged, wout_ref[...])


def _ffn_kernel(x_ref, fnorm_ref, wgu_ref, wd_ref, final_ref, o_ref, *, apply_final_norm):
    x = x_ref[...]
    h = _rmsnorm(x, fnorm_ref[...]).astype(BF16)
    gu = jnp.dot(h, wgu_ref[...], preferred_element_type=F32)
    act = _silu(gu[:, :D_FF]) * gu[:, D_FF:]
    y = x + _dot(act, wd_ref[...])
    if apply_final_norm:
        y = _rmsnorm(y, final_ref[...])
    o_ref[...] = y


def _resident(shape_tail, layer):
    n = len(shape_tail)
    return pl.BlockSpec((None,) + tuple(shape_tail), lambda b, i: (layer,) + (0,) * n,
                        pipeline_mode=pl.Buffered(1))


def _mixer_call(x, rope, params, layer):
    batch, seq, _ = x.shape
    t = MIXER_TILE
    tok = lambda w: pl.BlockSpec((None, t, w), lambda b, i: (b, i, 0))
    in_specs = [
        tok(D_MODEL), tok(LANES), tok(LANES), tok(LANES),
        _resident((1, D_MODEL), layer),
        _resident((D_MODEL, IN_COLS_PACKED), layer),
        _resident((1, MIX), layer), _resident((1, MIX), layer),
        _resident((SGU_GROUPS, BLOCK, BLOCK), layer), _resident((BLOCK, SGU_GROUPS), layer),
        _resident((SWA_HEADS, LANES), layer),
        _resident((DN_CONV, 3 * MIX), layer),
        _resident((1, LANES), layer), _resident((1, LANES), layer), _resident((1, DN_HD), layer),
        _resident((3, MIX, D_MODEL), layer), _resident((D_MODEL, D_MODEL), layer),
    ]
    return pl.pallas_call(
        _mixer_kernel,
        out_shape=jax.ShapeDtypeStruct(x.shape, F32),
        grid=(batch, seq // t),
        in_specs=in_specs,
        out_specs=tok(D_MODEL),
        scratch_shapes=[
            pltpu.VMEM((2, BLOCK, LANES), F32),
            pltpu.VMEM((t + 2 * SUBLANES, 3 * MIX), F32),
            pltpu.VMEM((DN_HEADS, DN_HD, DN_HD), F32),
        ],
        compiler_params=pltpu.CompilerParams(
            dimension_semantics=("arbitrary", "arbitrary"),
            vmem_limit_bytes=VMEM_LIMIT_BYTES),
        name=f"mixer_l{layer}",
    )(x, *rope, params["attn_norm"], params["w_in"], params["sgu_ln_g"], params["sgu_ln_b"],
      params["sgu_w"], params["sgu_b"], params["attn_sinks"], params["dn_conv_w"],
      params["dn_a_log"], params["dn_dt_bias"], params["dn_norm"], params["w_branch"],
      params["w_out"])


def _ffn_call(x, params, layer, apply_final_norm):
    batch, seq, _ = x.shape
    t = FFN_TILE
    tok = pl.BlockSpec((None, t, D_MODEL), lambda b, i: (b, i, 0))
    return pl.pallas_call(
        functools.partial(_ffn_kernel, apply_final_norm=apply_final_norm),
        out_shape=jax.ShapeDtypeStruct(x.shape, F32),
        grid=(batch, seq // t),
        in_specs=[
            tok,
            _resident((1, D_MODEL), layer),
            _resident((D_MODEL, 2 * D_FF), layer),
            _resident((D_FF, D_MODEL), layer),
            pl.BlockSpec((1, D_MODEL), lambda b, i: (0, 0)),
        ],
        out_specs=tok,
        compiler_params=pltpu.CompilerParams(
            dimension_semantics=("arbitrary", "arbitrary"),
            vmem_limit_bytes=VMEM_LIMIT_BYTES),
        name=f"ffn_l{layer}",
    )(x, params["ffn_norm"], params["w_gate_up"], params["w_down"], params["final_norm"])


def _rope_tables(positions):
    inv_freq = ROPE_THETA ** (-jnp.arange(0, ROPE_DIM, 2, dtype=F32) / ROPE_DIM)
    ang = positions.astype(F32)[..., None] * inv_freq
    cos, sin = jnp.cos(ang), jnp.sin(ang)
    rest = SWA_HD - ROPE_DIM
    ones = jnp.ones(ang.shape[:-1] + (rest,), F32)
    zeros = jnp.zeros(ang.shape[:-1] + (rest,), F32)
    zhalf = jnp.zeros_like(sin)
    cos_h = jnp.concatenate([cos, cos, ones], axis=-1)
    sina_h = jnp.concatenate([-sin, zhalf, zeros], axis=-1)
    sinb_h = jnp.concatenate([zhalf, sin, zeros], axis=-1)
    twice = lambda a: jnp.concatenate([a, a], axis=-1)
    return twice(cos_h), twice(sina_h), twice(sinb_h)


def _pack_params(attn_norm, w_in, sgu_ln_g, sgu_ln_b, sgu_w, sgu_b, attn_sinks, dn_conv_w,
                 dn_a_log, dn_dt_bias, dn_norm, w_branch, w_out, ffn_norm, w_gate_up, w_down,
                 final_norm):
    depth = w_in.shape[0]
    n_tail = 2 * DN_HEADS
    c_tail = w_in.shape[2] - 3 * D_MODEL - n_tail
    ba = jnp.pad(w_in[:, :, c_tail:c_tail + n_tail], ((0, 0), (0, 0), (0, LANES - n_tail)))
    w_in_p = jnp.concatenate([w_in[:, :, :c_tail], ba, w_in[:, :, c_tail + n_tail:]],
                             axis=-1).astype(BF16)
    lane_pad = lambda a: jnp.pad(a, ((0, 0), (DN_HEADS, LANES - 2 * DN_HEADS)))[:, None, :]
    return {
        "attn_norm": attn_norm[:, None, :],
        "w_in": w_in_p,
        "sgu_ln_g": sgu_ln_g[:, None, :],
        "sgu_ln_b": sgu_ln_b[:, None, :],
        "sgu_w": sgu_w,
        "sgu_b": jnp.swapaxes(sgu_b, 1, 2),
        "attn_sinks": jnp.broadcast_to(attn_sinks[:, :, None], (depth, SWA_HEADS, LANES)),
        "dn_conv_w": dn_conv_w,
        "dn_a_log": lane_pad(dn_a_log),
        "dn_dt_bias": lane_pad(dn_dt_bias),
        "dn_norm": dn_norm[:, None, :],
        "w_branch": w_branch.astype(BF16),
        "w_out": w_out.astype(BF16),
        "ffn_norm": ffn_norm[:, None, :],
        "w_gate_up": w_gate_up.astype(BF16),
        "w_down": w_down.astype(BF16),
        "final_norm": final_norm[None, :],
    }


def kernel(x, positions, attn_norm, w_in, sgu_ln_g, sgu_ln_b, sgu_w, sgu_b, attn_sinks,
           dn_conv_w, dn_a_log, dn_dt_bias, dn_norm, w_branch, w_out, ffn_norm, w_gate_up,
           w_down, final_norm):
    assert x.shape[1] % MIXER_TILE == 0 and x.shape[1] % FFN_TILE == 0
    assert x.shape[2] == D_MODEL and w_in.shape[2] + LANES - 2 * DN_HEADS == IN_COLS_PACKED
    depth = w_in.shape[0]
    params = _pack_params(attn_norm, w_in, sgu_ln_g, sgu_ln_b, sgu_w, sgu_b, attn_sinks,
                          dn_conv_w, dn_a_log, dn_dt_bias, dn_norm, w_branch, w_out, ffn_norm,
                          w_gate_up, w_down, final_norm)
    rope = _rope_tables(positions)
    for layer in range(depth):
        x = _mixer_call(x, rope, params, layer)
        x = _ffn_call(x, params, layer, apply_final_norm=(layer == depth - 1))
    return x
```

```python
import functools

import jax
import jax.numpy as jnp
from jax import lax
from jax.experimental import pallas as pl
from jax.experimental.pallas import tpu as pltpu

D_MODEL = 1024
MIX = 512
NORM_EPS = 1e-6

SGU_GROUPS = 4
BLOCK = 128

SWA_HEADS = 8
SWA_KV = 2
SWA_HD = 64
ROPE_THETA = 500000.0
ROPE_DIM = 16

DN_HEADS = 4
DN_HD = 128
DN_CONV = 4
DN_CHUNK = 64

D_FF = 2816

LANES = 128
SUBLANES = 8
MXU_DIM = 256

C_UV = 0
C_QKVB = C_UV + 2 * MIX
C_QKVC = C_QKVB + MIX + 2 * SWA_KV * SWA_HD
C_Z = C_QKVC + 3 * MIX
C_BETA = C_Z + MIX
C_DECAY = C_BETA + MIX
C_GATE = C_DECAY + MIX
IN_COLS_PACKED = C_GATE + 3 * D_MODEL

MIXER_TILE = 256
FFN_TILE = 256
VMEM_LIMIT_BYTES = 56 * 1024 * 1024

BF16 = jnp.bfloat16
F32 = jnp.float32


def _dot(a, b):
    return jnp.dot(a.astype(BF16), b.astype(BF16), preferred_element_type=F32)


def _dot_nt(a, b):
    return lax.dot_general(a.astype(BF16), b.astype(BF16), (((1,), (1,)), ((), ())),
                           preferred_element_type=F32)


def _rmsnorm(x, g):
    return x * lax.rsqrt(jnp.mean(x * x, axis=-1, keepdims=True) + NORM_EPS) * g


def _gelu_tanh(x):
    return 0.5 * x * (1.0 + jnp.tanh(0.7978845608028654 * (x + 0.044715 * (x * x * x))))


def _silu(x):
    return x * jax.nn.sigmoid(x)


def _softplus(x):
    return jnp.maximum(x, 0.0) + jnp.log1p(jnp.exp(-jnp.abs(x)))


def _iota(shape, dim):
    return lax.broadcasted_iota(jnp.int32, shape, dim)


def _spatial_gating(u_pre, v_pre, ln_g, ln_b, sguw_ref, sgub_ref):
    t = u_pre.shape[0]
    u = _gelu_tanh(u_pre)
    v = _gelu_tanh(v_pre)
    vc = v - jnp.mean(v, axis=-1, keepdims=True)
    vn = (vc * lax.rsqrt(jnp.mean(vc * vc, axis=-1, keepdims=True) + NORM_EPS) * ln_g
          + ln_b).astype(BF16)
    causal = _iota((BLOCK, BLOCK), 0) >= _iota((BLOCK, BLOCK), 1)
    groups = []
    for g in range(SGU_GROUPS):
        w = jnp.where(causal, sguw_ref[g], 0.0).astype(BF16)
        bias = sgub_ref[:, g:g + 1]
        cols = slice(g * LANES, (g + 1) * LANES)
        mixed = [_dot(w, vn[blk * BLOCK:(blk + 1) * BLOCK, cols]) + bias
                 for blk in range(t // BLOCK)]
        groups.append(u[:, cols] * jnp.concatenate(mixed, axis=0))
    return jnp.concatenate(groups, axis=1)


def _rope(x, cos, sin_a, sin_b):
    n = x.shape[1] // LANES
    if n > 1:
        cos = jnp.concatenate([cos] * n, axis=1)
        sin_a = jnp.concatenate([sin_a] * n, axis=1)
        sin_b = jnp.concatenate([sin_b] * n, axis=1)
    half = ROPE_DIM // 2
    width = x.shape[1]
    return (x * cos + pltpu.roll(x, width - half, 1) * sin_a + pltpu.roll(x, half, 1) * sin_b)


def _sliding_window_attention(q, k, v, cos, sin_a, sin_b, sinks_ref, kv_scr, seq_start):
    t = q.shape[0]
    n_blk = t // BLOCK
    group = SWA_HEADS // SWA_KV
    q = _rope(q, cos, sin_a, sin_b) * (SWA_HD ** -0.5)
    k = _rope(k, cos, sin_a, sin_b)

    kcat = jnp.concatenate([kv_scr[0], k], axis=0)
    vcat = jnp.concatenate([kv_scr[1], v], axis=0)

    lane = _iota((1, LANES), 1)
    lo = lane < SWA_HD
    krot = pltpu.roll(kcat, SWA_HD, 1)
    vrot = pltpu.roll(vcat, SWA_HD, 1)
    kdup = [jnp.where(lo, kcat, krot).astype(BF16), jnp.where(lo, krot, kcat).astype(BF16)]
    vlo = [jnp.where(lo, vcat, 0.0).astype(BF16), jnp.where(lo, vrot, 0.0).astype(BF16)]
    vhi = [jnp.where(lo, 0.0, vrot).astype(BF16), jnp.where(lo, 0.0, vcat).astype(BF16)]

    rows4 = group * BLOCK
    cur_mask = (_iota((rows4, BLOCK), 0) % BLOCK) >= _iota((rows4, BLOCK), 1)
    neg_inf = jnp.float32(-jnp.inf)
    units = [(blk, kh) for blk in range(n_blk) for kh in range(SWA_KV)]

    lhs = []
    for blk, kh in units:
        rows = slice(blk * BLOCK, (blk + 1) * BLOCK)
        s0 = q[rows, (2 * kh) * LANES:(2 * kh + 1) * LANES]
        s1 = q[rows, (2 * kh + 1) * LANES:(2 * kh + 2) * LANES]
        lhs.append(jnp.concatenate([jnp.where(lo, s0, 0.0), jnp.where(lo, s1, 0.0),
                                    jnp.where(lo, 0.0, s0), jnp.where(lo, 0.0, s1)],
                                   axis=0).astype(BF16))
    logits = [_dot_nt(lhs[i], kdup[kh][blk * BLOCK:(blk + 2) * BLOCK])
              for i, (blk, kh) in enumerate(units)]
    probs = []
    for i, (blk, kh) in enumerate(units):
        l_prev = logits[i][:, :BLOCK]
        l_cur = logits[i][:, BLOCK:]
        if blk == 0:
            l_prev = jnp.where(seq_start, neg_inf, l_prev)
        act = jnp.where(cur_mask, l_cur, l_prev)
        sink = jnp.concatenate(
            [jnp.broadcast_to(sinks_ref[group * kh + 2 * j + half:group * kh + 2 * j + half + 1,
                                        0:1], (BLOCK, 1))
             for half in range(2) for j in range(2)], axis=0)
        m = jnp.maximum(jnp.max(act, axis=1, keepdims=True), sink)
        p = jnp.exp(act - m)
        den = jnp.sum(p, axis=1, keepdims=True) + jnp.exp(sink - m)
        p = p * (1.0 / den)
        probs.append(jnp.concatenate([jnp.where(cur_mask, 0.0, p), jnp.where(cur_mask, p, 0.0)],
                                     axis=1).astype(BF16))
    out_rows = []
    for blk in range(n_blk):
        slabs = []
        for kh in range(SWA_KV):
            i = blk * SWA_KV + kh
            keys = slice(blk * BLOCK, (blk + 2) * BLOCK)
            res = (_dot(probs[i][:2 * BLOCK], vlo[kh][keys])
                   + _dot(probs[i][2 * BLOCK:], vhi[kh][keys]))
            slabs += [res[:BLOCK], res[BLOCK:]]
        out_rows.append(jnp.concatenate(slabs, axis=1))
    kv_scr[0] = k[t - BLOCK:, :]
    kv_scr[1] = v[t - BLOCK:, :]
    return jnp.concatenate(out_rows, axis=0)


def _chunk_cumsum(x):
    row = _iota(x.shape, 0) % DN_CHUNK
    shift = 1
    while shift < DN_CHUNK:
        x = x + jnp.where(row >= shift, pltpu.roll(x, shift, 0), 0.0)
        shift *= 2
    return x


def _head_sums(x, ones_bd):
    xx = x * x
    return jnp.concatenate([_dot(xx[:, i * MXU_DIM:(i + 1) * MXU_DIM], ones_bd)
                            for i in range(x.shape[1] // MXU_DIM)], axis=1)


def _gated_deltanet(qkv, z, beta_logit, decay_logit, convw_ref, alog, dtb, dnorm,
                    conv_scr, state_scr, seq_start):
    t = qkv.shape[0]
    c = DN_CHUNK
    n_chunk = t // c
    pad = SUBLANES
    heads = range(DN_HEADS)
    hs = [slice(h * DN_HD, (h + 1) * DN_HD) for h in heads]

    conv_scr[pad:pad + t, :] = qkv
    y = None
    for i in range(DN_CONV):
        off = pad - (DN_CONV - 1) + i
        term = conv_scr[off:off + t, :] * convw_ref[i:i + 1, :]
        y = term if y is None else y + term
    conv_scr[0:pad, :] = qkv[t - pad:, :]
    y = _silu(y)

    ones_bd = ((_iota((MXU_DIM, MXU_DIM), 0) // DN_HD)
               == (_iota((MXU_DIM, MXU_DIM), 1) // DN_HD)).astype(BF16)
    q = y[:, :MIX]
    k = y[:, MIX:2 * MIX]
    v = y[:, 2 * MIX:]
    q = q * lax.rsqrt(_head_sums(q, ones_bd) + NORM_EPS) * (DN_HD ** -0.5)
    k = k * lax.rsqrt(_head_sums(k, ones_bd) + NORM_EPS)

    beta = jax.nn.sigmoid(beta_logit)
    g = -jnp.exp(alog) * _softplus(decay_logit + dtb)
    gc = _chunk_cumsum(g)
    egc = jnp.exp(gc)
    g_last = jnp.concatenate(
        [jnp.broadcast_to(gc[(ci + 1) * c - 1:(ci + 1) * c, :], (c, MIX)) for ci in range(n_chunk)],
        axis=0)
    kb = k * beta
    vb = v * beta
    kbe = kb * egc
    qd = q * egc
    kd = k * jnp.exp(g_last - gc)

    lane = _iota((1, LANES), 1)
    g4 = gc[:, hs[DN_HEADS - 1]]
    for h in reversed(range(DN_HEADS - 1)):
        g4 = jnp.where(lane == h, gc[:, hs[h]], g4)
    g4_t = g4.T

    ri = _iota((t, t), 0)
    ci_ = _iota((t, t), 1)
    same = (ri // c) == (ci_ // c)
    incl = jnp.logical_and(same, ri >= ci_)
    strict = jnp.logical_and(same, ri > ci_)
    neg_inf = jnp.float32(-jnp.inf)

    aq = [_dot_nt(jnp.concatenate([kb[:, hs[h]], q[:, hs[h]]], axis=0), k[:, hs[h]])
          for h in heads]
    lower, attn, sol = [], [], []
    for h in heads:
        g_col = gc[:, hs[h]]
        diff = jnp.concatenate([g_col] * (t // DN_HD), axis=1) - g4_t[h:h + 1, :]
        decay = jnp.exp(jnp.where(incl, diff, neg_inf))
        lower.append(jnp.where(strict, aq[h][:t] * decay, 0.0).astype(BF16))
        attn.append((aq[h][t:] * decay).astype(BF16))
        sol.append(jnp.concatenate([vb[:, hs[h]], kbe[:, hs[h]]], axis=1))

    sol = [sol[h] - _dot(lower[h], sol[h]) for h in heads]
    power = [_dot(lower[h], lower[h]).astype(BF16) for h in heads]
    n_sq = 2
    while True:
        sol = [sol[h] + _dot(power[h], sol[h]) for h in heads]
        if 2 * n_sq >= c:
            break
        power = [_dot(power[h], power[h]).astype(BF16) for h in heads]
        n_sq *= 2

    kd_t = [kd[:, hs[h]].T.astype(BF16) for h in heads]
    row_chunk = _iota((2 * c, 1), 0) // c
    gn = []
    for h in heads:
        per_chunk = []
        for ci in range(n_chunk):
            pair = slice((ci // 2) * 2 * c, (ci // 2 + 1) * 2 * c)
            rhs = jnp.where(row_chunk == (ci % 2), sol[h][pair], 0.0)
            per_chunk.append(_dot(kd_t[h][:, pair], rhs))
        gn.append(per_chunk)

    state = [state_scr[h] for h in heads]
    starts = [[] for _ in heads]
    for ci in range(n_chunk):
        for h in heads:
            starts[h].append(state[h])
        upd = [_dot(gn[h][ci][:, DN_HD:], state[h]) for h in heads]
        state = [state[h] * jnp.exp(g_last[ci * c:ci * c + 1, hs[h]])
                 + (gn[h][ci][:, :DN_HD] - upd[h]) for h in heads]

    xs = [[_dot(jnp.concatenate([qd[ci * c:(ci + 1) * c, hs[h]],
                                 sol[h][ci * c:(ci + 1) * c, DN_HD:]], axis=0), starts[h][ci])
           for ci in range(n_chunk)] for h in heads]
    outs = []
    for h in heads:
        xq = jnp.concatenate([xs[h][ci][:c] for ci in range(n_chunk)], axis=0)
        xw = jnp.concatenate([xs[h][ci][c:] for ci in range(n_chunk)], axis=0)
        outs.append(xq + _dot(attn[h], sol[h][:, :DN_HD] - xw))
    o = jnp.concatenate(outs, axis=1)
    o = o * lax.rsqrt(_head_sums(o, ones_bd) * (1.0 / DN_HD) + NORM_EPS) * dnorm * _silu(z)
    for h in heads:
        state_scr[h] = state[h]
    return o


def _mixer_kernel(x_ref, cos_ref, sina_ref, sinb_ref, anorm_ref, win_ref, lng_ref, lnb_ref,
                  sguw_ref, sgub_ref, sinks_ref, convw_ref, alog_ref, dtb_ref, dnorm_ref,
                  wbr_ref, wout_ref, o_ref, kv_scr, conv_scr, state_scr):
    seq_start = pl.program_id(1) == 0

    @pl.when(seq_start)
    def _():
        kv_scr[...] = jnp.zeros_like(kv_scr)
        conv_scr[0:SUBLANES, :] = jnp.zeros((SUBLANES, 3 * MIX), F32)
        state_scr[...] = jnp.zeros_like(state_scr)

    x = x_ref[...]
    h = _rmsnorm(x, anorm_ref[...]).astype(BF16)

    def proj(lo, width):
        return jnp.dot(h, win_ref[:, lo:lo + width], preferred_element_type=F32)

    uv = proj(C_UV, 2 * MIX)
    out_a = _spatial_gating(uv[:, :MIX], uv[:, MIX:], lng_ref[...], lnb_ref[...],
                            sguw_ref, sgub_ref)

    qkv_b = proj(C_QKVB, MIX + 2 * SWA_KV * SWA_HD)
    out_b = _sliding_window_attention(
        qkv_b[:, :MIX], qkv_b[:, MIX:MIX + LANES], qkv_b[:, MIX + LANES:],
        cos_ref[...], sina_ref[...], sinb_ref[...], sinks_ref, kv_scr, seq_start)

    out_c = _gated_deltanet(proj(C_QKVC, 3 * MIX), proj(C_Z, MIX), proj(C_BETA, MIX),
                            proj(C_DECAY, MIX), convw_ref, alog_ref[...], dtb_ref[...],
                            dnorm_ref[...], conv_scr, state_scr, seq_start)

    merged = None
    for n, branch in enumerate((out_a, out_b, out_c)):
        gate = jax.nn.sigmoid(proj(C_GATE + n * D_MODEL, D_MODEL))
        term = gate * _dot(branch, wbr_ref[n])
        merged = term if merged is None else merged + term
    o_ref[...] = x + _dot(merged, wout_ref[...])


def _ffn_kernel(x_ref, fnorm_ref, wgu_ref, wd_ref, final_ref, o_ref, *, apply_final_norm):
    x = x_ref[...]
    h = _rmsnorm(x, fnorm_ref[...]).astype(BF16)
    gu = jnp.dot(h, wgu_ref[...], preferred_element_type=F32)
    act = _silu(gu[:, :D_FF]) * gu[:, D_FF:]
    y = x + _dot(act, wd_ref[...])
    if apply_final_norm:
        y = _rmsnorm(y, final_ref[...])
    o_ref[...] = y


def _resident(shape_tail, layer):
    n = len(shape_tail)
    return pl.BlockSpec((None,) + tuple(shape_tail), lambda b, i: (layer,) + (0,) * n,
                        pipeline_mode=pl.Buffered(1))


def _mixer_call(x, rope, params, layer):
    batch, seq, _ = x.shape
    t = MIXER_TILE
    tok = lambda w: pl.BlockSpec((None, t, w), lambda b, i: (b, i, 0))
    in_specs = [
        tok(D_MODEL), tok(LANES), tok(LANES), tok(LANES),
        _resident((1, D_MODEL), layer),
        _resident((D_MODEL, IN_COLS_PACKED), layer),
        _resident((1, MIX), layer), _resident((1, MIX), layer),
        _resident((SGU_GROUPS, BLOCK, BLOCK), layer), _resident((BLOCK, SGU_GROUPS), layer),
        _resident((SWA_HEADS, LANES), layer),
        _resident((DN_CONV, 3 * MIX), layer),
        _resident((1, MIX), layer), _resident((1, MIX), layer), _resident((1, MIX), layer),
        _resident((3, MIX, D_MODEL), layer), _resident((D_MODEL, D_MODEL), layer),
    ]
    return pl.pallas_call(
        _mixer_kernel,
        out_shape=jax.ShapeDtypeStruct(x.shape, F32),
        grid=(batch, seq // t),
        in_specs=in_specs,
        out_specs=tok(D_MODEL),
        scratch_shapes=[
            pltpu.VMEM((2, BLOCK, LANES), F32),
            pltpu.VMEM((t + SUBLANES, 3 * MIX), F32),
            pltpu.VMEM((DN_HEADS, DN_HD, DN_HD), F32),
        ],
        compiler_params=pltpu.CompilerParams(
            dimension_semantics=("arbitrary", "arbitrary"),
            vmem_limit_bytes=VMEM_LIMIT_BYTES),
        name=f"mixer_l{layer}",
    )(x, *rope, params["attn_norm"], params["w_in"], params["sgu_ln_g"], params["sgu_ln_b"],
      params["sgu_w"], params["sgu_b"], params["attn_sinks"], params["dn_conv_w"],
      params["dn_a_log"], params["dn_dt_bias"], params["dn_norm"], params["w_branch"],
      params["w_out"])


def _ffn_call(x, params, layer, apply_final_norm):
    batch, seq, _ = x.shape
    t = FFN_TILE
    tok = pl.BlockSpec((None, t, D_MODEL), lambda b, i: (b, i, 0))
    return pl.pallas_call(
        functools.partial(_ffn_kernel, apply_final_norm=apply_final_norm),
        out_shape=jax.ShapeDtypeStruct(x.shape, F32),
        grid=(batch, seq // t),
        in_specs=[
            tok,
            _resident((1, D_MODEL), layer),
            _resident((D_MODEL, 2 * D_FF), layer),
            _resident((D_FF, D_MODEL), layer),
            pl.BlockSpec((1, D_MODEL), lambda b, i: (0, 0)),
        ],
        out_specs=tok,
        compiler_params=pltpu.CompilerParams(
            dimension_semantics=("arbitrary", "arbitrary"),
            vmem_limit_bytes=VMEM_LIMIT_BYTES),
        name=f"ffn_l{layer}",
    )(x, params["ffn_norm"], params["w_gate_up"], params["w_down"], params["final_norm"])


def _rope_tables(positions):
    inv_freq = ROPE_THETA ** (-jnp.arange(0, ROPE_DIM, 2, dtype=F32) / ROPE_DIM)
    ang = positions.astype(F32)[..., None] * inv_freq
    cos, sin = jnp.cos(ang), jnp.sin(ang)
    rest = SWA_HD - ROPE_DIM
    ones = jnp.ones(ang.shape[:-1] + (rest,), F32)
    zeros = jnp.zeros(ang.shape[:-1] + (rest,), F32)
    zhalf = jnp.zeros_like(sin)
    cos_h = jnp.concatenate([cos, cos, ones], axis=-1)
    sina_h = jnp.concatenate([-sin, zhalf, zeros], axis=-1)
    sinb_h = jnp.concatenate([zhalf, sin, zeros], axis=-1)
    twice = lambda a: jnp.concatenate([a, a], axis=-1)
    return twice(cos_h), twice(sina_h), twice(sinb_h)


def _pack_params(attn_norm, w_in, sgu_ln_g, sgu_ln_b, sgu_w, sgu_b, attn_sinks, dn_conv_w,
                 dn_a_log, dn_dt_bias, dn_norm, w_branch, w_out, ffn_norm, w_gate_up, w_down,
                 final_norm):
    depth = w_in.shape[0]
    c_beta = w_in.shape[2] - 3 * D_MODEL - 2 * DN_HEADS
    c_decay = c_beta + DN_HEADS
    per_head = lambda a: jnp.repeat(a, DN_HD, axis=-1)
    w_in_p = jnp.concatenate(
        [w_in[:, :, :c_beta], per_head(w_in[:, :, c_beta:c_decay]),
         per_head(w_in[:, :, c_decay:c_decay + DN_HEADS]), w_in[:, :, c_decay + DN_HEADS:]],
        axis=-1).astype(BF16)
    return {
        "attn_norm": attn_norm[:, None, :],
        "w_in": w_in_p,
        "sgu_ln_g": sgu_ln_g[:, None, :],
        "sgu_ln_b": sgu_ln_b[:, None, :],
        "sgu_w": sgu_w,
        "sgu_b": jnp.swapaxes(sgu_b, 1, 2),
        "attn_sinks": jnp.broadcast_to(attn_sinks[:, :, None], (depth, SWA_HEADS, LANES)),
        "dn_conv_w": dn_conv_w,
        "dn_a_log": per_head(dn_a_log)[:, None, :],
        "dn_dt_bias": per_head(dn_dt_bias)[:, None, :],
        "dn_norm": jnp.tile(dn_norm, (1, DN_HEADS))[:, None, :],
        "w_branch": w_branch.astype(BF16),
        "w_out": w_out.astype(BF16),
        "ffn_norm": ffn_norm[:, None, :],
        "w_gate_up": w_gate_up.astype(BF16),
        "w_down": w_down.astype(BF16),
        "final_norm": final_norm[None, :],
    }


def kernel(x, positions, attn_norm, w_in, sgu_ln_g, sgu_ln_b, sgu_w, sgu_b, attn_sinks,
           dn_conv_w, dn_a_log, dn_dt_bias, dn_norm, w_branch, w_out, ffn_norm, w_gate_up,
           w_down, final_norm):
    assert MIXER_TILE == MXU_DIM == 4 * DN_CHUNK
    assert x.shape[1] % MIXER_TILE == 0 and x.shape[1] % FFN_TILE == 0
    assert x.shape[2] == D_MODEL
    assert w_in.shape[2] + 2 * DN_HEADS * (DN_HD - 1) == IN_COLS_PACKED
    depth = w_in.shape[0]
    params = _pack_params(attn_norm, w_in, sgu_ln_g, sgu_ln_b, sgu_w, sgu_b, attn_sinks,
                          dn_conv_w, dn_a_log, dn_dt_bias, dn_norm, w_branch, w_out, ffn_norm,
                          w_gate_up, w_down, final_norm)
    rope = _rope_tables(positions)
    for layer in range(depth):
        x = _mixer_call(x, rope, params, layer)
        x = _ffn_call(x, params, layer, apply_final_norm=(layer == depth - 1))
    return x
```

```python
import functools

import jax
import jax.numpy as jnp
from jax import lax
from jax.experimental import pallas as pl
from jax.experimental.pallas import tpu as pltpu

D_MODEL = 1024
MIX = 512
NORM_EPS = 1e-6

SGU_GROUPS = 4
BLOCK = 128

SWA_HEADS = 8
SWA_KV = 2
SWA_HD = 64
ROPE_THETA = 500000.0
ROPE_DIM = 16

DN_HEADS = 4
DN_HD = 128
DN_CONV = 4
DN_CHUNK = 64

D_FF = 2816

LANES = 128
SUBLANES = 8
MXU_DIM = 256

C_UV = 0
C_QKVB = C_UV + 2 * MIX
C_QKVC = C_QKVB + MIX + 2 * SWA_KV * SWA_HD
C_Z = C_QKVC + 3 * MIX
C_FRONT_END = C_Z + MIX

MIXER_TILE = 256
FFN_TILE = 256
VMEM_LIMIT_BYTES = 56 * 1024 * 1024

BF16 = jnp.bfloat16
F32 = jnp.float32
_GELU_C = 0.7978845608028654


def _dot(a, b):
    return jnp.dot(a.astype(BF16), b.astype(BF16), preferred_element_type=F32)


def _dot_nt(a, b):
    return lax.dot_general(a.astype(BF16), b.astype(BF16), (((1,), (1,)), ((), ())),
                           preferred_element_type=F32)


def _rmsnorm(x, g):
    return x * lax.rsqrt(jnp.mean(x * x, axis=-1, keepdims=True) + NORM_EPS) * g


def _gelu_tanh(x):
    inner = x * (_GELU_C + (_GELU_C * 0.044715) * (x * x))
    return (0.5 * x) * (1.0 + jnp.tanh(inner))


def _silu(x):
    return x * jax.nn.sigmoid(x)


def _softplus(x):
    return jnp.maximum(x, 0.0) + jnp.log(1.0 + jnp.exp(-jnp.abs(x)))


def _iota(shape, dim):
    return lax.broadcasted_iota(jnp.int32, shape, dim)


def _spatial_gating(u_pre, v_pre, ln_g, ln_b, sguw_ref, sgub_ref):
    t = u_pre.shape[0]
    u = _gelu_tanh(u_pre)
    v = _gelu_tanh(v_pre)
    vc = v - jnp.mean(v, axis=-1, keepdims=True)
    vn = (vc * lax.rsqrt(jnp.mean(vc * vc, axis=-1, keepdims=True) + NORM_EPS) * ln_g
          + ln_b).astype(BF16)
    causal = _iota((BLOCK, BLOCK), 0) >= _iota((BLOCK, BLOCK), 1)
    groups = []
    for g in range(SGU_GROUPS):
        w = jnp.where(causal, sguw_ref[g], 0.0).astype(BF16)
        bias = sgub_ref[:, g:g + 1]
        cols = slice(g * LANES, (g + 1) * LANES)
        mixed = [_dot(w, vn[blk * BLOCK:(blk + 1) * BLOCK, cols]) + bias
                 for blk in range(t // BLOCK)]
        groups.append(u[:, cols] * jnp.concatenate(mixed, axis=0))
    return jnp.concatenate(groups, axis=1)


def _rope(x, cos, sin):
    half = ROPE_DIM // 2
    first_half = (_iota((1, LANES), 1) % SWA_HD) < half
    sin_a = jnp.where(first_half, -sin, 0.0)
    sin_b = jnp.where(first_half, 0.0, sin)
    n = x.shape[1] // LANES
    if n > 1:
        cos = jnp.concatenate([cos] * n, axis=1)
        sin_a = jnp.concatenate([sin_a] * n, axis=1)
        sin_b = jnp.concatenate([sin_b] * n, axis=1)
    width = x.shape[1]
    return (x * cos + pltpu.roll(x, width - half, 1) * sin_a + pltpu.roll(x, half, 1) * sin_b)


def _sliding_window_attention(q, k, v, cos, sin, sinks_ref, kv_scr, seq_start):
    t = q.shape[0]
    n_blk = t // BLOCK
    group = SWA_HEADS // SWA_KV
    q = _rope(q, cos, sin) * (SWA_HD ** -0.5)
    k = _rope(k, cos, sin)

    kcat = jnp.concatenate([kv_scr[0], k], axis=0)
    vcat = jnp.concatenate([kv_scr[1], v], axis=0)

    lane = _iota((1, LANES), 1)
    lo = lane < SWA_HD
    krot = pltpu.roll(kcat, SWA_HD, 1)
    vrot = pltpu.roll(vcat, SWA_HD, 1)
    kdup = [jnp.where(lo, kcat, krot).astype(BF16), jnp.where(lo, krot, kcat).astype(BF16)]
    vlo = [jnp.where(lo, vcat, 0.0).astype(BF16), jnp.where(lo, vrot, 0.0).astype(BF16)]
    vhi = [jnp.where(lo, 0.0, vrot).astype(BF16), jnp.where(lo, 0.0, vcat).astype(BF16)]

    rows4 = group * BLOCK
    cur_mask = (_iota((rows4, BLOCK), 0) % BLOCK) >= _iota((rows4, BLOCK), 1)
    neg_inf = jnp.float32(-jnp.inf)
    units = [(blk, kh) for blk in range(n_blk) for kh in range(SWA_KV)]

    lhs = []
    for blk, kh in units:
        rows = slice(blk * BLOCK, (blk + 1) * BLOCK)
        s0 = q[rows, (2 * kh) * LANES:(2 * kh + 1) * LANES]
        s1 = q[rows, (2 * kh + 1) * LANES:(2 * kh + 2) * LANES]
        lhs.append(jnp.concatenate([jnp.where(lo, s0, 0.0), jnp.where(lo, s1, 0.0),
                                    jnp.where(lo, 0.0, s0), jnp.where(lo, 0.0, s1)],
                                   axis=0).astype(BF16))
    logits = [_dot_nt(lhs[i], kdup[kh][blk * BLOCK:(blk + 2) * BLOCK])
              for i, (blk, kh) in enumerate(units)]
    yield
    probs = []
    for i, (blk, kh) in enumerate(units):
        l_prev = logits[i][:, :BLOCK]
        l_cur = logits[i][:, BLOCK:]
        if blk == 0:
            l_prev = jnp.where(seq_start, neg_inf, l_prev)
        act = jnp.where(cur_mask, l_cur, l_prev)
        sink = jnp.concatenate(
            [jnp.broadcast_to(sinks_ref[group * kh + 2 * j + half:group * kh + 2 * j + half + 1,
                                        0:1], (BLOCK, 1))
             for half in range(2) for j in range(2)], axis=0)
        m = jnp.maximum(jnp.max(act, axis=1, keepdims=True), sink)
        p = jnp.exp(act - m)
        den = jnp.sum(p, axis=1, keepdims=True) + jnp.exp(sink - m)
        p = p * (1.0 / den)
        probs.append(jnp.concatenate([jnp.where(cur_mask, 0.0, p), jnp.where(cur_mask, p, 0.0)],
                                     axis=1).astype(BF16))
    out_rows = []
    for blk in range(n_blk):
        slabs = []
        for kh in range(SWA_KV):
            i = blk * SWA_KV + kh
            keys = slice(blk * BLOCK, (blk + 2) * BLOCK)
            res = (_dot(probs[i][:2 * BLOCK], vlo[kh][keys])
                   + _dot(probs[i][2 * BLOCK:], vhi[kh][keys]))
            slabs += [res[:BLOCK], res[BLOCK:]]
        out_rows.append(jnp.concatenate(slabs, axis=1))
    kv_scr[0] = k[t - BLOCK:, :]
    kv_scr[1] = v[t - BLOCK:, :]
    yield jnp.concatenate(out_rows, axis=0)


def _chunk_cumsum(x):
    row = _iota(x.shape, 0) % DN_CHUNK
    shift = 1
    while shift < DN_CHUNK:
        x = x + jnp.where(row >= shift, pltpu.roll(x, shift, 0), 0.0)
        shift *= 2
    return x


def _head_sums(x, ones_bd):
    xx = x * x
    return jnp.concatenate([_dot(xx[:, i * MXU_DIM:(i + 1) * MXU_DIM], ones_bd)
                            for i in range(x.shape[1] // MXU_DIM)], axis=1)


def _gated_deltanet(qkv, z, beta_logit, decay_logit, convw_ref, alog, dtb, dnorm, ones_bd,
                    conv_scr, state_scr):
    t = qkv.shape[0]
    c = DN_CHUNK
    n_chunk = t // c
    pad = SUBLANES
    heads = range(DN_HEADS)
    hs = [slice(h * DN_HD, (h + 1) * DN_HD) for h in heads]

    xa = jnp.concatenate([conv_scr[0:pad, :], qkv], axis=0)
    y = qkv * convw_ref[DN_CONV - 1:DN_CONV, :]
    for i in range(DN_CONV - 1):
        y = y + pltpu.roll(xa, DN_CONV - 1 - i, 0)[pad:, :] * convw_ref[i:i + 1, :]
    conv_scr[0:pad, :] = qkv[t - pad:, :]
    y = _silu(y)

    q = y[:, :MIX]
    k = y[:, MIX:2 * MIX]
    v = y[:, 2 * MIX:]
    q = q * lax.rsqrt(_head_sums(q, ones_bd) + NORM_EPS) * (DN_HD ** -0.5)
    k = k * lax.rsqrt(_head_sums(k, ones_bd) + NORM_EPS)

    beta = jax.nn.sigmoid(beta_logit)
    g = -jnp.exp(alog) * _softplus(decay_logit + dtb)
    gc = _chunk_cumsum(g)
    egc = jnp.exp(gc)
    g_last = jnp.concatenate(
        [jnp.broadcast_to(gc[(ci + 1) * c - 1:(ci + 1) * c, :], (c, MIX)) for ci in range(n_chunk)],
        axis=0)
    kb = k * beta
    vb = v * beta
    kbe = kb * egc
    qd = q * egc
    kd = k * jnp.exp(g_last - gc)

    lane = _iota((1, LANES), 1)
    g4 = gc[:, hs[DN_HEADS - 1]]
    for h in reversed(range(DN_HEADS - 1)):
        g4 = jnp.where(lane == h, gc[:, hs[h]], g4)
    g4_t = g4.T
    kd_t = [kd[:, hs[h]].T.astype(BF16) for h in heads]
    yield

    ri = _iota((t, t), 0)
    ci_ = _iota((t, t), 1)
    same = (ri // c) == (ci_ // c)
    incl = jnp.logical_and(same, ri >= ci_)
    strict = jnp.logical_and(same, ri > ci_)
    neg_inf = jnp.float32(-jnp.inf)

    aq = [_dot_nt(jnp.concatenate([kb[:, hs[h]], q[:, hs[h]]], axis=0), k[:, hs[h]])
          for h in heads]
    lower, attn, sol = [], [], []
    for h in heads:
        g_col = gc[:, hs[h]]
        diff = jnp.concatenate([g_col] * (t // DN_HD), axis=1) - g4_t[h:h + 1, :]
        decay = jnp.exp(jnp.where(incl, diff, neg_inf))
        lower.append(jnp.where(strict, aq[h][:t] * decay, 0.0).astype(BF16))
        attn.append((aq[h][t:] * decay).astype(BF16))
        sol.append(jnp.concatenate([vb[:, hs[h]], kbe[:, hs[h]]], axis=1))
    yield

    sol = [sol[h] - _dot(lower[h], sol[h]) for h in heads]
    power = [_dot(lower[h], lower[h]).astype(BF16) for h in heads]
    n_sq = 2
    while True:
        sol = [sol[h] + _dot(power[h], sol[h]) for h in heads]
        if 2 * n_sq >= c:
            break
        power = [_dot(power[h], power[h]).astype(BF16) for h in heads]
        n_sq *= 2

    row_chunk = _iota((2 * c, 1), 0) // c
    gn = []
    for h in heads:
        per_chunk = []
        for ci in range(n_chunk):
            pair = slice((ci // 2) * 2 * c, (ci // 2 + 1) * 2 * c)
            rhs = jnp.where(row_chunk == (ci % 2), sol[h][pair], 0.0)
            per_chunk.append(_dot(kd_t[h][:, pair], rhs))
        gn.append(per_chunk)

    state = [state_scr[h] for h in heads]
    starts = [[] for _ in heads]
    for ci in range(n_chunk):
        for h in heads:
            starts[h].append(state[h])
        upd = [_dot(gn[h][ci][:, DN_HD:], state[h]) for h in heads]
        state = [state[h] * jnp.exp(g_last[ci * c:ci * c + 1, hs[h]])
                 + (gn[h][ci][:, :DN_HD] - upd[h]) for h in heads]

    xs = [[_dot(jnp.concatenate([qd[ci * c:(ci + 1) * c, hs[h]],
                                 sol[h][ci * c:(ci + 1) * c, DN_HD:]], axis=0), starts[h][ci])
           for ci in range(n_chunk)] for h in heads]
    outs = []
    for h in heads:
        xq = jnp.concatenate([xs[h][ci][:c] for ci in range(n_chunk)], axis=0)
        xw = jnp.concatenate([xs[h][ci][c:] for ci in range(n_chunk)], axis=0)
        outs.append(xq + _dot(attn[h], sol[h][:, :DN_HD] - xw))
    o = jnp.concatenate(outs, axis=1)
    o = o * lax.rsqrt(_head_sums(o, ones_bd) * (1.0 / DN_HD) + NORM_EPS) * dnorm * _silu(z)
    for h in heads:
        state_scr[h] = state[h]
    yield o


def _mixer_kernel(x_ref, cos_ref, sin_ref, ones_ref, anorm_ref, wfront_ref, wrep_ref, wgate_ref,
                  lng_ref, lnb_ref, sguw_ref, sgub_ref, sinks_ref, convw_ref, alog_ref, dtb_ref,
                  dnorm_ref, wbr_ref, wout_ref, o_ref, kv_scr, conv_scr, state_scr, merged_scr,
                  gate_scr):
    seq_start = pl.program_id(1) == 0

    @pl.when(seq_start)
    def _():
        kv_scr[...] = jnp.zeros_like(kv_scr)
        conv_scr[0:SUBLANES, :] = jnp.zeros((SUBLANES, 3 * MIX), F32)
        state_scr[...] = jnp.zeros_like(state_scr)

    x = x_ref[...]
    h = _rmsnorm(x, anorm_ref[...]).astype(BF16)

    def proj(w_ref, lo, width):
        return jnp.dot(h, w_ref[:, lo:lo + width], preferred_element_type=F32)

    dn = _gated_deltanet(proj(wfront_ref, C_QKVC, 3 * MIX), proj(wfront_ref, C_Z, MIX),
                         proj(wrep_ref, 0, MIX), proj(wrep_ref, MIX, MIX), convw_ref,
                         alog_ref[...], dtb_ref[...], dnorm_ref[...], ones_ref[...],
                         conv_scr, state_scr)
    next(dn)

    qkv_b = proj(wfront_ref, C_QKVB, MIX + 2 * SWA_KV * SWA_HD)
    swa = _sliding_window_attention(
        qkv_b[:, :MIX], qkv_b[:, MIX:MIX + LANES], qkv_b[:, MIX + LANES:],
        cos_ref[...], sin_ref[...], sinks_ref, kv_scr, seq_start)
    next(swa)

    def gated_branch(n, branch):
        gate = jax.nn.sigmoid(proj(wgate_ref, n * D_MODEL, D_MODEL))
        return gate * _dot(branch, wbr_ref[n])

    uv = proj(wfront_ref, C_UV, 2 * MIX)
    out_a = _spatial_gating(uv[:, :MIX], uv[:, MIX:], lng_ref[...], lnb_ref[...],
                            sguw_ref, sgub_ref)
    merged_scr[...] = gated_branch(0, out_a)

    out_b = next(swa)
    next(dn)
    merged_scr[...] += gated_branch(1, out_b)

    gate_scr[...] = jax.nn.sigmoid(proj(wgate_ref, 2 * D_MODEL, D_MODEL))
    out_c = next(dn)
    merged = merged_scr[...] + gate_scr[...] * _dot(out_c, wbr_ref[2])
    o_ref[...] = x + _dot(merged, wout_ref[...])


def _ffn_kernel(x_ref, fnorm_ref, wgu_ref, wd_ref, final_ref, o_ref, *, apply_final_norm):
    x = x_ref[...]
    h = _rmsnorm(x, fnorm_ref[...]).astype(BF16)
    gu = jnp.dot(h, wgu_ref[...], preferred_element_type=F32)
    act = _silu(gu[:, :D_FF]) * gu[:, D_FF:]
    y = x + _dot(act, wd_ref[...])
    if apply_final_norm:
        y = _rmsnorm(y, final_ref[...])
    o_ref[...] = y


def _resident(shape_tail, layer):
    n = len(shape_tail)
    return pl.BlockSpec((None,) + tuple(shape_tail), lambda b, i: (layer,) + (0,) * n,
                        pipeline_mode=pl.Buffered(1))


def _mixer_call(x, rope, params, layer):
    batch, seq, _ = x.shape
    t = MIXER_TILE
    tok = lambda w: pl.BlockSpec((None, t, w), lambda b, i: (b, i, 0))
    in_specs = [
        tok(D_MODEL), tok(LANES), tok(LANES),
        pl.BlockSpec((MXU_DIM, MXU_DIM), lambda b, i: (0, 0), pipeline_mode=pl.Buffered(1)),
        _resident((1, D_MODEL), layer),
        _resident((D_MODEL, C_FRONT_END), layer),
        _resident((D_MODEL, 2 * MIX), layer),
        _resident((D_MODEL, 3 * D_MODEL), layer),
        _resident((1, MIX), layer), _resident((1, MIX), layer),
        _resident((SGU_GROUPS, BLOCK, BLOCK), layer), _resident((BLOCK, SGU_GROUPS), layer),
        _resident((SWA_HEADS, LANES), layer),
        _resident((DN_CONV, 3 * MIX), layer),
        _resident((1, MIX), layer), _resident((1, MIX), layer), _resident((1, MIX), layer),
        _resident((3, MIX, D_MODEL), layer), _resident((D_MODEL, D_MODEL), layer),
    ]
    return pl.pallas_call(
        _mixer_kernel,
        out_shape=jax.ShapeDtypeStruct(x.shape, F32),
        grid=(batch, seq // t),
        in_specs=in_specs,
        out_specs=tok(D_MODEL),
        scratch_shapes=[
            pltpu.VMEM((2, BLOCK, LANES), F32),
            pltpu.VMEM((SUBLANES, 3 * MIX), F32),
            pltpu.VMEM((DN_HEADS, DN_HD, DN_HD), F32),
            pltpu.VMEM((t, D_MODEL), F32),
            pltpu.VMEM((t, D_MODEL), F32),
        ],
        compiler_params=pltpu.CompilerParams(
            dimension_semantics=("arbitrary", "arbitrary"),
            vmem_limit_bytes=VMEM_LIMIT_BYTES),
        name=f"mixer_l{layer}",
    )(x, *rope, params["head_ones"], params["attn_norm"], params["w_front"], params["w_rep"],
      params["w_gates"], params["sgu_ln_g"], params["sgu_ln_b"],
      params["sgu_w"], params["sgu_b"], params["attn_sinks"], params["dn_conv_w"],
      params["dn_a_log"], params["dn_dt_bias"], params["dn_norm"], params["w_branch"],
      params["w_out"])


def _ffn_call(x, params, layer, apply_final_norm):
    batch, seq, _ = x.shape
    t = FFN_TILE
    tok = pl.BlockSpec((None, t, D_MODEL), lambda b, i: (b, i, 0))
    return pl.pallas_call(
        functools.partial(_ffn_kernel, apply_final_norm=apply_final_norm),
        out_shape=jax.ShapeDtypeStruct(x.shape, F32),
        grid=(batch, seq // t),
        in_specs=[
            tok,
            _resident((1, D_MODEL), layer),
            _resident((D_MODEL, 2 * D_FF), layer),
            _resident((D_FF, D_MODEL), layer),
            pl.BlockSpec((1, D_MODEL), lambda b, i: (0, 0)),
        ],
        out_specs=tok,
        compiler_params=pltpu.CompilerParams(
            dimension_semantics=("arbitrary", "arbitrary"),
            vmem_limit_bytes=VMEM_LIMIT_BYTES),
        name=f"ffn_l{layer}",
    )(x, params["ffn_norm"], params["w_gate_up"], params["w_down"], params["final_norm"])


def _rope_tables(positions):
    inv_freq = ROPE_THETA ** (-jnp.arange(0, ROPE_DIM, 2, dtype=F32) / ROPE_DIM)
    r = jnp.arange(LANES) % SWA_HD
    freq_lane = jnp.where(r < ROPE_DIM, inv_freq[r % (ROPE_DIM // 2)], 0.0)
    ang = positions.astype(F32)[..., None] * freq_lane
    return jnp.cos(ang), jnp.sin(ang)


def _pack_params(attn_norm, w_in, sgu_ln_g, sgu_ln_b, sgu_w, sgu_b, attn_sinks, dn_conv_w,
                 dn_a_log, dn_dt_bias, dn_norm, w_branch, w_out, ffn_norm, w_gate_up, w_down,
                 final_norm):
    depth = w_in.shape[0]
    c_decay = C_FRONT_END + DN_HEADS
    c_gate = c_decay + DN_HEADS
    per_head = lambda a: jnp.repeat(a, DN_HD, axis=-1)
    lane_head = jnp.arange(MXU_DIM) // DN_HD
    return {
        "head_ones": (lane_head[:, None] == lane_head[None, :]).astype(BF16),
        "attn_norm": attn_norm[:, None, :],
        "w_front": w_in[:, :, :C_FRONT_END].astype(BF16),
        "w_rep": per_head(w_in[:, :, C_FRONT_END:c_gate]).astype(BF16),
        "w_gates": w_in[:, :, c_gate:].astype(BF16),
        "sgu_ln_g": sgu_ln_g[:, None, :],
        "sgu_ln_b": sgu_ln_b[:, None, :],
        "sgu_w": sgu_w,
        "sgu_b": jnp.swapaxes(sgu_b, 1, 2),
        "attn_sinks": jnp.broadcast_to(attn_sinks[:, :, None], (depth, SWA_HEADS, LANES)),
        "dn_conv_w": dn_conv_w,
        "dn_a_log": per_head(dn_a_log)[:, None, :],
        "dn_dt_bias": per_head(dn_dt_bias)[:, None, :],
        "dn_norm": jnp.tile(dn_norm, (1, DN_HEADS))[:, None, :],
        "w_branch": w_branch.astype(BF16),
        "w_out": w_out.astype(BF16),
        "ffn_norm": ffn_norm[:, None, :],
        "w_gate_up": w_gate_up.astype(BF16),
        "w_down": w_down.astype(BF16),
        "final_norm": final_norm[None, :],
    }


def kernel(x, positions, attn_norm, w_in, sgu_ln_g, sgu_ln_b, sgu_w, sgu_b, attn_sinks,
           dn_conv_w, dn_a_log, dn_dt_bias, dn_norm, w_branch, w_out, ffn_norm, w_gate_up,
           w_down, final_norm):
    assert MIXER_TILE == MXU_DIM == 4 * DN_CHUNK
    assert x.shape[1] % MIXER_TILE == 0 and x.shape[1] % FFN_TILE == 0
    assert x.shape[2] == D_MODEL
    assert w_in.shape[2] == C_FRONT_END + 2 * DN_HEADS + 3 * D_MODEL
    depth = w_in.shape[0]
    params = _pack_params(attn_norm, w_in, sgu_ln_g, sgu_ln_b, sgu_w, sgu_b, attn_sinks,
                          dn_conv_w, dn_a_log, dn_dt_bias, dn_norm, w_branch, w_out, ffn_norm,
                          w_gate_up, w_down, final_norm)
    rope = _rope_tables(positions)
    for layer in range(depth):
        x = _mixer_call(x, rope, params, layer)
        x = _ffn_call(x, params, layer, apply_final_norm=(layer == depth - 1))
    return x
```

```python
import functools

import jax
import jax.numpy as jnp
from jax import lax
from jax.experimental import pallas as pl
from jax.experimental.pallas import tpu as pltpu

D_MODEL = 1024
MIX = 512
NORM_EPS = 1e-6

SGU_GROUPS = 4
BLOCK = 128

SWA_HEADS = 8
SWA_KV = 2
SWA_HD = 64
ROPE_THETA = 500000.0
ROPE_DIM = 16

DN_HEADS = 4
DN_HD = 128
DN_CONV = 4
DN_CHUNK = 64

D_FF = 2816

LANES = 128
SUBLANES = 8
MXU_DIM = 256

C_UV = 0
C_QKVB = C_UV + 2 * MIX
C_QKVC = C_QKVB + MIX + 2 * SWA_KV * SWA_HD
C_Z = C_QKVC + 3 * MIX
C_FRONT_END = C_Z + MIX
GATE_OFF = 2 * DN_HEADS

MIXER_TILE = 256
FFN_TILE = 512
VMEM_LIMIT_BYTES = 56 * 1024 * 1024

BF16 = jnp.bfloat16
F32 = jnp.float32
_GELU_C = 0.7978845608028654


def _dot(a, b):
    return jnp.dot(a.astype(BF16), b.astype(BF16), preferred_element_type=F32)


def _dot_nt(a, b):
    return lax.dot_general(a.astype(BF16), b.astype(BF16), (((1,), (1,)), ((), ())),
                           preferred_element_type=F32)


def _rmsnorm(x, g):
    return x * lax.rsqrt(jnp.mean(x * x, axis=-1, keepdims=True) + NORM_EPS) * g


def _gelu_tanh(x):
    inner = x * (_GELU_C + (_GELU_C * 0.044715) * (x * x))
    return (0.5 * x) * (1.0 + jnp.tanh(inner))


def _silu(x):
    return x * jax.nn.sigmoid(x)


def _softplus(x):
    return jnp.maximum(x, 0.0) + jnp.log(1.0 + jnp.exp(-jnp.abs(x)))


def _iota(shape, dim):
    return lax.broadcasted_iota(jnp.int32, shape, dim)


def _spatial_gating(u_pre, v_pre, ln_g, ln_b, sguw_ref, sgub_ref):
    t = u_pre.shape[0]
    u = _gelu_tanh(u_pre)
    v = _gelu_tanh(v_pre)
    vc = v - jnp.mean(v, axis=-1, keepdims=True)
    vn = (vc * lax.rsqrt(jnp.mean(vc * vc, axis=-1, keepdims=True) + NORM_EPS) * ln_g
          + ln_b).astype(BF16)
    causal = _iota((BLOCK, BLOCK), 0) >= _iota((BLOCK, BLOCK), 1)
    groups = []
    for g in range(SGU_GROUPS):
        w = jnp.where(causal, sguw_ref[g], 0.0).astype(BF16)
        bias = sgub_ref[:, g:g + 1]
        cols = slice(g * LANES, (g + 1) * LANES)
        mixed = [_dot(w, vn[blk * BLOCK:(blk + 1) * BLOCK, cols]) + bias
                 for blk in range(t // BLOCK)]
        groups.append(u[:, cols] * jnp.concatenate(mixed, axis=0))
    return jnp.concatenate(groups, axis=1)


def _rope(x, cos, sin):
    half = ROPE_DIM // 2
    first_half = (_iota((1, LANES), 1) % SWA_HD) < half
    sin_a = jnp.where(first_half, -sin, 0.0)
    sin_b = jnp.where(first_half, 0.0, sin)
    n = x.shape[1] // LANES
    if n > 1:
        cos = jnp.concatenate([cos] * n, axis=1)
        sin_a = jnp.concatenate([sin_a] * n, axis=1)
        sin_b = jnp.concatenate([sin_b] * n, axis=1)
    width = x.shape[1]
    return (x * cos + pltpu.roll(x, width - half, 1) * sin_a + pltpu.roll(x, half, 1) * sin_b)


def _sliding_window_attention(q, k, v, cos, sin, sinks_ref, kv_scr, seq_start):
    t = q.shape[0]
    n_blk = t // BLOCK
    group = SWA_HEADS // SWA_KV
    q = _rope(q, cos, sin) * (SWA_HD ** -0.5)
    k = _rope(k, cos, sin)

    kcat = jnp.concatenate([kv_scr[0], k], axis=0)
    vcat = jnp.concatenate([kv_scr[1], v], axis=0)

    lane = _iota((1, LANES), 1)
    lo = lane < SWA_HD
    krot = pltpu.roll(kcat, SWA_HD, 1)
    vrot = pltpu.roll(vcat, SWA_HD, 1)
    kdup = [jnp.where(lo, kcat, krot).astype(BF16), jnp.where(lo, krot, kcat).astype(BF16)]
    vlo = [jnp.where(lo, vcat, 0.0).astype(BF16), jnp.where(lo, vrot, 0.0).astype(BF16)]
    vhi = [jnp.where(lo, 0.0, vrot).astype(BF16), jnp.where(lo, 0.0, vcat).astype(BF16)]

    rows4 = group * BLOCK
    cur_mask = (_iota((rows4, BLOCK), 0) % BLOCK) >= _iota((rows4, BLOCK), 1)
    neg_inf = jnp.float32(-jnp.inf)
    units = [(blk, kh) for blk in range(n_blk) for kh in range(SWA_KV)]

    lhs = []
    for blk, kh in units:
        rows = slice(blk * BLOCK, (blk + 1) * BLOCK)
        s0 = q[rows, (2 * kh) * LANES:(2 * kh + 1) * LANES]
        s1 = q[rows, (2 * kh + 1) * LANES:(2 * kh + 2) * LANES]
        lhs.append(jnp.concatenate([jnp.where(lo, s0, 0.0), jnp.where(lo, s1, 0.0),
                                    jnp.where(lo, 0.0, s0), jnp.where(lo, 0.0, s1)],
                                   axis=0).astype(BF16))
    logits = [_dot_nt(lhs[i], kdup[kh][blk * BLOCK:(blk + 2) * BLOCK])
              for i, (blk, kh) in enumerate(units)]
    yield
    probs = []
    for i, (blk, kh) in enumerate(units):
        l_prev = logits[i][:, :BLOCK]
        l_cur = logits[i][:, BLOCK:]
        if blk == 0:
            l_prev = jnp.where(seq_start, neg_inf, l_prev)
        act = jnp.where(cur_mask, l_cur, l_prev)
        sink = jnp.concatenate(
            [jnp.broadcast_to(sinks_ref[group * kh + 2 * j + half:group * kh + 2 * j + half + 1,
                                        0:1], (BLOCK, 1))
             for half in range(2) for j in range(2)], axis=0)
        m = jnp.maximum(jnp.max(act, axis=1, keepdims=True), sink)
        p = jnp.exp(act - m)
        den = jnp.sum(p, axis=1, keepdims=True) + jnp.exp(sink - m)
        p = p * (1.0 / den)
        probs.append(jnp.concatenate([jnp.where(cur_mask, 0.0, p), jnp.where(cur_mask, p, 0.0)],
                                     axis=1).astype(BF16))
    out_rows = []
    for blk in range(n_blk):
        slabs = []
        for kh in range(SWA_KV):
            i = blk * SWA_KV + kh
            keys = slice(blk * BLOCK, (blk + 2) * BLOCK)
            res = (_dot(probs[i][:2 * BLOCK], vlo[kh][keys])
                   + _dot(probs[i][2 * BLOCK:], vhi[kh][keys]))
            slabs += [res[:BLOCK], res[BLOCK:]]
        out_rows.append(jnp.concatenate(slabs, axis=1))
    kv_scr[0] = k[t - BLOCK:, :]
    kv_scr[1] = v[t - BLOCK:, :]
    yield jnp.concatenate(out_rows, axis=0)


def _chunk_cumsum(x):
    row = _iota(x.shape, 0) % DN_CHUNK
    shift = 1
    while shift < DN_CHUNK:
        x = x + jnp.where(row >= shift, pltpu.roll(x, shift, 0), 0.0)
        shift *= 2
    return x


def _head_sums(x, ones_bd):
    xx = x * x
    return jnp.concatenate([_dot(xx[:, i * MXU_DIM:(i + 1) * MXU_DIM], ones_bd)
                            for i in range(x.shape[1] // MXU_DIM)], axis=1)


def _gated_deltanet(qkv, z, beta_logit, decay_logit, convw_ref, alog, dtb, dnorm, ones_bd,
                    conv_scr, state_scr):
    t = qkv.shape[0]
    c = DN_CHUNK
    n_chunk = t // c
    pad = SUBLANES
    heads = range(DN_HEADS)
    hs = [slice(h * DN_HD, (h + 1) * DN_HD) for h in heads]

    xa = jnp.concatenate([conv_scr[0:pad, :], qkv], axis=0)
    y = qkv * convw_ref[DN_CONV - 1:DN_CONV, :]
    for i in range(DN_CONV - 1):
        y = y + pltpu.roll(xa, DN_CONV - 1 - i, 0)[pad:, :] * convw_ref[i:i + 1, :]
    conv_scr[0:pad, :] = qkv[t - pad:, :]
    y = _silu(y)

    q = y[:, :MIX]
    k = y[:, MIX:2 * MIX]
    v = y[:, 2 * MIX:]
    q = q * lax.rsqrt(_head_sums(q, ones_bd) + NORM_EPS) * (DN_HD ** -0.5)
    k = k * lax.rsqrt(_head_sums(k, ones_bd) + NORM_EPS)

    beta = jax.nn.sigmoid(beta_logit)
    g = -jnp.exp(alog) * _softplus(decay_logit + dtb)
    gc = _chunk_cumsum(g)
    egc = jnp.exp(gc)
    g_last = jnp.concatenate(
        [jnp.broadcast_to(gc[(ci + 1) * c - 1:(ci + 1) * c, :], (c, MIX)) for ci in range(n_chunk)],
        axis=0)
    kb = k * beta
    vb = v * beta
    kbe = kb * egc
    qd = q * egc
    kd = k * jnp.exp(g_last - gc)

    lane = _iota((1, LANES), 1)
    g4 = gc[:, hs[DN_HEADS - 1]]
    for h in reversed(range(DN_HEADS - 1)):
        g4 = jnp.where(lane == h, gc[:, hs[h]], g4)
    g4_t = g4.T
    kd_t = [kd[:, hs[h]].T.astype(BF16) for h in heads]
    yield

    ri = _iota((t, t), 0)
    ci_ = _iota((t, t), 1)
    same = (ri // c) == (ci_ // c)
    incl = jnp.logical_and(same, ri >= ci_)
    strict = jnp.logical_and(same, ri > ci_)
    neg_inf = jnp.float32(-jnp.inf)

    aq = [_dot_nt(jnp.concatenate([kb[:, hs[h]], q[:, hs[h]]], axis=0), k[:, hs[h]])
          for h in heads]
    lower, attn, sol = [], [], []
    for h in heads:
        g_col = gc[:, hs[h]]
        diff = jnp.concatenate([g_col] * (t // DN_HD), axis=1) - g4_t[h:h + 1, :]
        decay = jnp.exp(jnp.where(incl, diff, neg_inf))
        lower.append(jnp.where(strict, aq[h][:t] * decay, 0.0).astype(BF16))
        attn.append((aq[h][t:] * decay).astype(BF16))
        sol.append(jnp.concatenate([vb[:, hs[h]], kbe[:, hs[h]]], axis=1))
    yield

    sol = [sol[h] - _dot(lower[h], sol[h]) for h in heads]
    power = [_dot(lower[h], lower[h]).astype(BF16) for h in heads]
    n_sq = 2
    while True:
        sol = [sol[h] + _dot(power[h], sol[h]) for h in heads]
        if 2 * n_sq >= c:
            break
        power = [_dot(power[h], power[h]).astype(BF16) for h in heads]
        n_sq *= 2

    row_chunk = _iota((2 * c, 1), 0) // c
    gn = []
    for h in heads:
        per_chunk = []
        for ci in range(n_chunk):
            pair = slice((ci // 2) * 2 * c, (ci // 2 + 1) * 2 * c)
            rhs = jnp.where(row_chunk == (ci % 2), sol[h][pair], 0.0)
            per_chunk.append(_dot(kd_t[h][:, pair], rhs))
        gn.append(per_chunk)

    state = [state_scr[h] for h in heads]
    starts = [[] for _ in heads]
    for ci in range(n_chunk):
        for h in heads:
            starts[h].append(state[h])
        upd = [_dot(gn[h][ci][:, DN_HD:], state[h]) for h in heads]
        state = [state[h] * jnp.exp(g_last[ci * c:ci * c + 1, hs[h]])
                 + (gn[h][ci][:, :DN_HD] - upd[h]) for h in heads]

    xs = [[_dot(jnp.concatenate([qd[ci * c:(ci + 1) * c, hs[h]],
                                 sol[h][ci * c:(ci + 1) * c, DN_HD:]], axis=0), starts[h][ci])
           for ci in range(n_chunk)] for h in heads]
    outs = []
    for h in heads:
        xq = jnp.concatenate([xs[h][ci][:c] for ci in range(n_chunk)], axis=0)
        xw = jnp.concatenate([xs[h][ci][c:] for ci in range(n_chunk)], axis=0)
        outs.append(xq + _dot(attn[h], sol[h][:, :DN_HD] - xw))
    o = jnp.concatenate(outs, axis=1)
    o = o * lax.rsqrt(_head_sums(o, ones_bd) * (1.0 / DN_HD) + NORM_EPS) * dnorm * _silu(z)
    for h in heads:
        state_scr[h] = state[h]
    yield o


def _mixer_kernel(x_ref, cos_ref, sin_ref, ones_ref, anorm_ref, wfront_ref, wrep_ref, wtail_ref,
                  lng_ref, lnb_ref, sguw_ref, sgub_ref, sinks_ref, convw_ref, alog_ref, dtb_ref,
                  dnorm_ref, wbr_ref, wout_ref, o_ref, kv_scr, conv_scr, state_scr, merged_scr,
                  gate_scr, wgate_scr):
    seq_start = pl.program_id(1) == 0

    @pl.when(jnp.logical_and(pl.program_id(0) == 0, seq_start))
    def _():
        for n in range(3):
            window = wtail_ref[:, n * D_MODEL:(n + 1) * D_MODEL + GATE_OFF].astype(F32)
            wgate_scr[:, n * D_MODEL:(n + 1) * D_MODEL] = window[:, GATE_OFF:].astype(BF16)

    @pl.when(seq_start)
    def _():
        kv_scr[...] = jnp.zeros_like(kv_scr)
        conv_scr[0:SUBLANES, :] = jnp.zeros((SUBLANES, 3 * MIX), F32)
        state_scr[...] = jnp.zeros_like(state_scr)

    x = x_ref[...]
    h = _rmsnorm(x, anorm_ref[...]).astype(BF16)

    def proj(w_ref, lo, width):
        return jnp.dot(h, w_ref[:, lo:lo + width], preferred_element_type=F32)

    dn = _gated_deltanet(proj(wfront_ref, C_QKVC, 3 * MIX), proj(wfront_ref, C_Z, MIX),
                         proj(wrep_ref, 0, MIX), proj(wrep_ref, MIX, MIX), convw_ref,
                         alog_ref[...], dtb_ref[...], dnorm_ref[...], ones_ref[...],
                         conv_scr, state_scr)
    next(dn)

    qkv_b = proj(wfront_ref, C_QKVB, MIX + 2 * SWA_KV * SWA_HD)
    swa = _sliding_window_attention(
        qkv_b[:, :MIX], qkv_b[:, MIX:MIX + LANES], qkv_b[:, MIX + LANES:],
        cos_ref[...], sin_ref[...], sinks_ref, kv_scr, seq_start)
    next(swa)

    def gated_branch(n, branch):
        gate = jax.nn.sigmoid(proj(wgate_scr, n * D_MODEL, D_MODEL))
        return gate * _dot(branch, wbr_ref[n])

    uv = proj(wfront_ref, C_UV, 2 * MIX)
    out_a = _spatial_gating(uv[:, :MIX], uv[:, MIX:], lng_ref[...], lnb_ref[...],
                            sguw_ref, sgub_ref)
    merged_scr[...] = gated_branch(0, out_a)

    out_b = next(swa)
    next(dn)
    merged_scr[...] += gated_branch(1, out_b)

    gate_scr[...] = jax.nn.sigmoid(proj(wgate_scr, 2 * D_MODEL, D_MODEL))
    out_c = next(dn)
    merged = merged_scr[...] + gate_scr[...] * _dot(out_c, wbr_ref[2])
    o_ref[...] = x + _dot(merged, wout_ref[...])


def _ffn_kernel(x_ref, fnorm_ref, wgu_ref, wd_ref, final_ref, o_ref, *, apply_final_norm):
    x = x_ref[...]
    h = _rmsnorm(x, fnorm_ref[...]).astype(BF16)
    gu = jnp.dot(h, wgu_ref[...], preferred_element_type=F32)
    act = _silu(gu[:, :D_FF]) * gu[:, D_FF:]
    y = x + _dot(act, wd_ref[...])
    if apply_final_norm:
        y = _rmsnorm(y, final_ref[...])
    o_ref[...] = y


def _resident(shape_tail, layer):
    n = len(shape_tail)
    return pl.BlockSpec((None,) + tuple(shape_tail), lambda b, i: (layer,) + (0,) * n,
                        pipeline_mode=pl.Buffered(1))


def _mixer_call(x, rope, params, layer):
    batch, seq, _ = x.shape
    t = MIXER_TILE
    tok = lambda w: pl.BlockSpec((None, t, w), lambda b, i: (b, i, 0))
    in_specs = [
        tok(D_MODEL), tok(LANES), tok(LANES),
        pl.BlockSpec((MXU_DIM, MXU_DIM), lambda b, i: (0, 0), pipeline_mode=pl.Buffered(1)),
        _resident((1, D_MODEL), layer),
        _resident((D_MODEL, C_FRONT_END), layer),
        _resident((D_MODEL, 2 * MIX), layer),
        _resident((D_MODEL, GATE_OFF + 3 * D_MODEL), layer),
        _resident((1, MIX), layer), _resident((1, MIX), layer),
        _resident((SGU_GROUPS, BLOCK, BLOCK), layer), _resident((BLOCK, SGU_GROUPS), layer),
        _resident((SWA_HEADS, LANES), layer),
        _resident((DN_CONV, 3 * MIX), layer),
        _resident((1, MIX), layer), _resident((1, MIX), layer), _resident((1, MIX), layer),
        _resident((3, MIX, D_MODEL), layer), _resident((D_MODEL, D_MODEL), layer),
    ]
    return pl.pallas_call(
        _mixer_kernel,
        out_shape=jax.ShapeDtypeStruct(x.shape, F32),
        grid=(batch, seq // t),
        in_specs=in_specs,
        out_specs=tok(D_MODEL),
        scratch_shapes=[
            pltpu.VMEM((2, BLOCK, LANES), F32),
            pltpu.VMEM((SUBLANES, 3 * MIX), F32),
            pltpu.VMEM((DN_HEADS, DN_HD, DN_HD), F32),
            pltpu.VMEM((t, D_MODEL), F32),
            pltpu.VMEM((t, D_MODEL), F32),
            pltpu.VMEM((D_MODEL, 3 * D_MODEL), BF16),
        ],
        compiler_params=pltpu.CompilerParams(
            dimension_semantics=("arbitrary", "arbitrary"),
            vmem_limit_bytes=VMEM_LIMIT_BYTES),
        name=f"mixer_l{layer}",
    )(x, *rope, params["head_ones"], params["attn_norm"], params["w_in"], params["w_rep"],
      params["w_tail"], params["sgu_ln_g"], params["sgu_ln_b"],
      params["sgu_w"], params["sgu_b"], params["attn_sinks"], params["dn_conv_w"],
      params["dn_a_log"], params["dn_dt_bias"], params["dn_norm"], params["w_branch"],
      params["w_out"])


def _ffn_call(x, params, layer, apply_final_norm):
    batch, seq, _ = x.shape
    t = FFN_TILE
    tok = pl.BlockSpec((None, t, D_MODEL), lambda b, i: (b, i, 0))
    return pl.pallas_call(
        functools.partial(_ffn_kernel, apply_final_norm=apply_final_norm),
        out_shape=jax.ShapeDtypeStruct(x.shape, F32),
        grid=(batch, seq // t),
        in_specs=[
            tok,
            _resident((1, D_MODEL), layer),
            _resident((D_MODEL, 2 * D_FF), layer),
            _resident((D_FF, D_MODEL), layer),
            pl.BlockSpec((1, D_MODEL), lambda b, i: (0, 0)),
        ],
        out_specs=tok,
        compiler_params=pltpu.CompilerParams(
            dimension_semantics=("arbitrary", "arbitrary"),
            vmem_limit_bytes=VMEM_LIMIT_BYTES),
        name=f"ffn_l{layer}",
    )(x, params["ffn_norm"], params["w_gate_up"], params["w_down"], params["final_norm"])


def _rope_tables(positions):
    half = ROPE_DIM // 2
    inv_freq = ROPE_THETA ** (-jnp.arange(0, ROPE_DIM, 2, dtype=F32) / ROPE_DIM)
    ang = positions.astype(F32)[..., None] * inv_freq
    cos, sin = lax.optimization_barrier((jnp.cos(ang), jnp.sin(ang)))
    rotary = (jnp.arange(LANES) % SWA_HD) < ROPE_DIM
    reps = (1,) * (ang.ndim - 1) + (LANES // half,)
    return jnp.where(rotary, jnp.tile(cos, reps), 1.0), jnp.where(rotary, jnp.tile(sin, reps), 0.0)


def _pack_params(attn_norm, w_in, sgu_ln_g, sgu_ln_b, sgu_w, sgu_b, attn_sinks, dn_conv_w,
                 dn_a_log, dn_dt_bias, dn_norm, w_branch, w_out, ffn_norm, w_gate_up, w_down,
                 final_norm):
    depth = w_in.shape[0]
    c_gate = C_FRONT_END + GATE_OFF
    per_head = lambda a: jnp.repeat(a, DN_HD, axis=-1)
    lane_head = jnp.arange(MXU_DIM) // DN_HD
    w_in16 = w_in.astype(BF16)
    return {
        "head_ones": (lane_head[:, None] == lane_head[None, :]).astype(BF16),
        "attn_norm": attn_norm[:, None, :],
        "w_in": w_in16,
        "w_rep": per_head(w_in16[:, :, C_FRONT_END:c_gate]),
        "w_tail": w_in16[:, :, C_FRONT_END:],
        "sgu_ln_g": sgu_ln_g[:, None, :],
        "sgu_ln_b": sgu_ln_b[:, None, :],
        "sgu_w": sgu_w,
        "sgu_b": jnp.swapaxes(sgu_b, 1, 2),
        "attn_sinks": jnp.broadcast_to(attn_sinks[:, :, None], (depth, SWA_HEADS, LANES)),
        "dn_conv_w": dn_conv_w,
        "dn_a_log": per_head(dn_a_log)[:, None, :],
        "dn_dt_bias": per_head(dn_dt_bias)[:, None, :],
        "dn_norm": jnp.tile(dn_norm, (1, DN_HEADS))[:, None, :],
        "w_branch": w_branch.astype(BF16),
        "w_out": w_out.astype(BF16),
        "ffn_norm": ffn_norm[:, None, :],
        "w_gate_up": w_gate_up.astype(BF16),
        "w_down": w_down.astype(BF16),
        "final_norm": final_norm[None, :],
    }


def kernel(x, positions, attn_norm, w_in, sgu_ln_g, sgu_ln_b, sgu_w, sgu_b, attn_sinks,
           dn_conv_w, dn_a_log, dn_dt_bias, dn_norm, w_branch, w_out, ffn_norm, w_gate_up,
           w_down, final_norm):
    assert MIXER_TILE == MXU_DIM == 4 * DN_CHUNK
    assert x.shape[1] % MIXER_TILE == 0 and x.shape[1] % FFN_TILE == 0
    assert x.shape[2] == D_MODEL
    assert w_in.shape[2] == C_FRONT_END + GATE_OFF + 3 * D_MODEL
    depth = w_in.shape[0]
    params = _pack_params(attn_norm, w_in, sgu_ln_g, sgu_ln_b, sgu_w, sgu_b, attn_sinks,
                          dn_conv_w, dn_a_log, dn_dt_bias, dn_norm, w_branch, w_out, ffn_norm,
                          w_gate_up, w_down, final_norm)
    rope = _rope_tables(positions)
    for layer in range(depth):
        x = _mixer_call(x, rope, params, layer)
        x = _ffn_call(x, params, layer, apply_final_norm=(layer == depth - 1))
    return x
```

```python
import functools

import jax
import jax.numpy as jnp
from jax import lax
from jax.experimental import pallas as pl
from jax.experimental.pallas import tpu as pltpu

D_MODEL = 1024
MIX = 512
NORM_EPS = 1e-6

SGU_GROUPS = 4
BLOCK = 128

SWA_HEADS = 8
SWA_KV = 2
SWA_HD = 64
ROPE_THETA = 500000.0
ROPE_DIM = 16

DN_HEADS = 4
DN_HD = 128
DN_CONV = 4
DN_CHUNK = 64

D_FF = 2816

LANES = 128
SUBLANES = 8
MXU_DIM = 256

C_UV = 0
C_QKVB = C_UV + 2 * MIX
C_QKVC = C_QKVB + MIX + 2 * SWA_KV * SWA_HD
C_Z = C_QKVC + 3 * MIX
C_FRONT_END = C_Z + MIX
GATE_OFF = 2 * DN_HEADS

MIXER_TILE = 256
FFN_TILE = 512
VMEM_LIMIT_BYTES = 56 * 1024 * 1024

BF16 = jnp.bfloat16
F32 = jnp.float32
_GELU_C = 0.7978845608028654


def _dot(a, b):
    return jnp.dot(a.astype(BF16), b.astype(BF16), preferred_element_type=F32)


def _dot_nt(a, b):
    return lax.dot_general(a.astype(BF16), b.astype(BF16), (((1,), (1,)), ((), ())),
                           preferred_element_type=F32)


def _rmsnorm(x, g):
    return x * lax.rsqrt(jnp.mean(x * x, axis=-1, keepdims=True) + NORM_EPS) * g


def _gelu_tanh(x):
    inner = x * (_GELU_C + (_GELU_C * 0.044715) * (x * x))
    return (0.5 * x) * (1.0 + jnp.tanh(inner))


def _silu(x):
    return x * jax.nn.sigmoid(x)


def _softplus(x):
    return jnp.maximum(x, 0.0) + jnp.log(1.0 + jnp.exp(-jnp.abs(x)))


def _iota(shape, dim):
    return lax.broadcasted_iota(jnp.int32, shape, dim)


def _spatial_gating_prep(u_pre, v_pre, ln_g, ln_b, u_scr, vn_scr):
    u_scr[...] = _gelu_tanh(u_pre)
    v = _gelu_tanh(v_pre)
    vc = v - jnp.mean(v, axis=-1, keepdims=True)
    vn_scr[...] = (vc * lax.rsqrt(jnp.mean(vc * vc, axis=-1, keepdims=True) + NORM_EPS) * ln_g
                   + ln_b).astype(BF16)


def _spatial_gating_mix(u_scr, vn_scr, sguw_ref, sgub_ref):
    u = u_scr[...]
    vn = vn_scr[...]
    t = u.shape[0]
    causal = _iota((BLOCK, BLOCK), 0) >= _iota((BLOCK, BLOCK), 1)
    groups = []
    for g in range(SGU_GROUPS):
        w = jnp.where(causal, sguw_ref[g], 0.0).astype(BF16)
        bias = sgub_ref[:, g:g + 1]
        cols = slice(g * LANES, (g + 1) * LANES)
        mixed = [_dot(w, vn[blk * BLOCK:(blk + 1) * BLOCK, cols]) + bias
                 for blk in range(t // BLOCK)]
        groups.append(u[:, cols] * jnp.concatenate(mixed, axis=0))
    return jnp.concatenate(groups, axis=1)


def _rope(x, cos, sin):
    half = ROPE_DIM // 2
    first_half = (_iota((1, LANES), 1) % SWA_HD) < half
    sin_a = jnp.where(first_half, -sin, 0.0)
    sin_b = jnp.where(first_half, 0.0, sin)
    n = x.shape[1] // LANES
    if n > 1:
        cos = jnp.concatenate([cos] * n, axis=1)
        sin_a = jnp.concatenate([sin_a] * n, axis=1)
        sin_b = jnp.concatenate([sin_b] * n, axis=1)
    width = x.shape[1]
    return (x * cos + pltpu.roll(x, width - half, 1) * sin_a + pltpu.roll(x, half, 1) * sin_b)


def _sliding_window_attention(q, k, v, cos, sin, sinks_ref, kv_scr, probs_scr, seq_start):
    t = q.shape[0]
    n_blk = t // BLOCK
    group = SWA_HEADS // SWA_KV
    q = _rope(q, cos, sin) * (SWA_HD ** -0.5)
    k = _rope(k, cos, sin)

    kcat = jnp.concatenate([kv_scr[0], k], axis=0)
    vcat = jnp.concatenate([kv_scr[1], v], axis=0)

    lane = _iota((1, LANES), 1)
    lo = lane < SWA_HD
    krot = pltpu.roll(kcat, SWA_HD, 1)
    vrot = pltpu.roll(vcat, SWA_HD, 1)
    kdup = [jnp.where(lo, kcat, krot).astype(BF16), jnp.where(lo, krot, kcat).astype(BF16)]
    vlo = [jnp.where(lo, vcat, 0.0).astype(BF16), jnp.where(lo, vrot, 0.0).astype(BF16)]
    vhi = [jnp.where(lo, 0.0, vrot).astype(BF16), jnp.where(lo, 0.0, vcat).astype(BF16)]

    rows4 = group * BLOCK
    cur_mask = (_iota((rows4, BLOCK), 0) % BLOCK) >= _iota((rows4, BLOCK), 1)
    neg_inf = jnp.float32(-jnp.inf)
    units = [(blk, kh) for blk in range(n_blk) for kh in range(SWA_KV)]

    lhs = []
    for blk, kh in units:
        rows = slice(blk * BLOCK, (blk + 1) * BLOCK)
        s0 = q[rows, (2 * kh) * LANES:(2 * kh + 1) * LANES]
        s1 = q[rows, (2 * kh + 1) * LANES:(2 * kh + 2) * LANES]
        lhs.append(jnp.concatenate([jnp.where(lo, s0, 0.0), jnp.where(lo, s1, 0.0),
                                    jnp.where(lo, 0.0, s0), jnp.where(lo, 0.0, s1)],
                                   axis=0).astype(BF16))
    logits = [_dot_nt(lhs[i], kdup[kh][blk * BLOCK:(blk + 2) * BLOCK])
              for i, (blk, kh) in enumerate(units)]
    yield
    n_unit = len(units)
    sinks = [jnp.concatenate(
        [jnp.broadcast_to(sinks_ref[group * kh + 2 * j + half:group * kh + 2 * j + half + 1, 0:1],
                          (BLOCK, 1)) for half in range(2) for j in range(2)], axis=0)
        for kh in range(SWA_KV)]
    act = []
    for i, (blk, kh) in enumerate(units):
        l_prev = logits[i][:, :BLOCK]
        if blk == 0:
            l_prev = jnp.where(seq_start, neg_inf, l_prev)
        act.append(jnp.where(cur_mask, logits[i][:, BLOCK:], l_prev))
    m = [jnp.maximum(jnp.max(act[i], axis=1, keepdims=True), sinks[units[i][1]])
         for i in range(n_unit)]
    p = [jnp.exp(act[i] - m[i]) for i in range(n_unit)]
    den = [jnp.sum(p[i], axis=1, keepdims=True) + jnp.exp(sinks[units[i][1]] - m[i])
           for i in range(n_unit)]
    for i in range(n_unit):
        pn = p[i] * (1.0 / den[i])
        probs_scr[i] = jnp.concatenate([jnp.where(cur_mask, 0.0, pn), jnp.where(cur_mask, pn, 0.0)],
                                       axis=1).astype(BF16)
    yield
    probs = [probs_scr[i] for i in range(len(units))]
    out_rows = []
    for blk in range(n_blk):
        slabs = []
        for kh in range(SWA_KV):
            i = blk * SWA_KV + kh
            keys = slice(blk * BLOCK, (blk + 2) * BLOCK)
            res = (_dot(probs[i][:2 * BLOCK], vlo[kh][keys])
                   + _dot(probs[i][2 * BLOCK:], vhi[kh][keys]))
            slabs += [res[:BLOCK], res[BLOCK:]]
        out_rows.append(jnp.concatenate(slabs, axis=1))
    kv_scr[0] = k[t - BLOCK:, :]
    kv_scr[1] = v[t - BLOCK:, :]
    yield jnp.concatenate(out_rows, axis=0)


def _chunk_cumsum(x):
    row = _iota(x.shape, 0) % DN_CHUNK
    shift = 1
    while shift < DN_CHUNK:
        x = x + jnp.where(row >= shift, pltpu.roll(x, shift, 0), 0.0)
        shift *= 2
    return x


def _head_sums(x, ones_bd):
    xx = x * x
    return jnp.concatenate([_dot(xx[:, i * MXU_DIM:(i + 1) * MXU_DIM], ones_bd)
                            for i in range(x.shape[1] // MXU_DIM)], axis=1)


def _gated_deltanet(qkv, get_z, beta_logit, decay_logit, convw_ref, alog, dtb, dnorm, ones_bd,
                    conv_scr, state_scr, fill):
    t = qkv.shape[0]
    c = DN_CHUNK
    n_chunk = t // c
    pad = SUBLANES
    heads = range(DN_HEADS)
    hs = [slice(h * DN_HD, (h + 1) * DN_HD) for h in heads]

    xa = jnp.concatenate([conv_scr[0:pad, :], qkv], axis=0)
    y = qkv * convw_ref[DN_CONV - 1:DN_CONV, :]
    for i in range(DN_CONV - 1):
        y = y + pltpu.roll(xa, DN_CONV - 1 - i, 0)[pad:, :] * convw_ref[i:i + 1, :]
    conv_scr[0:pad, :] = qkv[t - pad:, :]
    y = _silu(y)

    q = y[:, :MIX]
    k = y[:, MIX:2 * MIX]
    v = y[:, 2 * MIX:]
    fill("prologue")
    q = q * lax.rsqrt(_head_sums(q, ones_bd) + NORM_EPS) * (DN_HD ** -0.5)
    k = k * lax.rsqrt(_head_sums(k, ones_bd) + NORM_EPS)

    beta = jax.nn.sigmoid(beta_logit)
    g = -jnp.exp(alog) * _softplus(decay_logit + dtb)
    gc = _chunk_cumsum(g)
    egc = jnp.exp(gc)
    g_last = jnp.concatenate(
        [jnp.broadcast_to(gc[(ci + 1) * c - 1:(ci + 1) * c, :], (c, MIX)) for ci in range(n_chunk)],
        axis=0)
    kb = k * beta
    vb = v * beta
    kbe = kb * egc
    qd = q * egc
    kd = k * jnp.exp(g_last - gc)
    c_dec = [jnp.exp(g_last[ci * c:ci * c + 1, :]) for ci in range(n_chunk)]

    lane = _iota((1, LANES), 1)
    g4 = gc[:, hs[DN_HEADS - 1]]
    for h in reversed(range(DN_HEADS - 1)):
        g4 = jnp.where(lane == h, gc[:, hs[h]], g4)
    g4_t = g4.T
    kd_t = [kd[:, hs[h]].T.astype(BF16) for h in heads]
    fill("norms")

    ri = _iota((t, t), 0)
    ci_ = _iota((t, t), 1)
    same = (ri // c) == (ci_ // c)
    incl = jnp.logical_and(same, ri >= ci_)
    strict = jnp.logical_and(same, ri > ci_)
    neg_inf = jnp.float32(-jnp.inf)

    aq = [_dot_nt(jnp.concatenate([kb[:, hs[h]], q[:, hs[h]]], axis=0), k[:, hs[h]])
          for h in heads]
    lower, attn, sol = [], [], []
    for h in heads:
        g_col = gc[:, hs[h]]
        diff = jnp.concatenate([g_col] * (t // DN_HD), axis=1) - g4_t[h:h + 1, :]
        decay = jnp.exp(jnp.where(incl, diff, neg_inf))
        lower.append(jnp.where(strict, aq[h][:t] * decay, 0.0).astype(BF16))
        attn.append((aq[h][t:] * decay).astype(BF16))
        sol.append(jnp.concatenate([vb[:, hs[h]], kbe[:, hs[h]]], axis=1))
    fill("decay")

    sol = [sol[h] - _dot(lower[h], sol[h]) for h in heads]
    power = [_dot(lower[h], lower[h]).astype(BF16) for h in heads]
    n_sq = 2
    while True:
        sol = [sol[h] + _dot(power[h], sol[h]) for h in heads]
        if 2 * n_sq >= c:
            break
        power = [_dot(power[h], power[h]).astype(BF16) for h in heads]
        n_sq *= 2

    row_chunk = _iota((2 * c, 1), 0) // c
    gn = []
    for h in heads:
        per_chunk = []
        for ci in range(n_chunk):
            pair = slice((ci // 2) * 2 * c, (ci // 2 + 1) * 2 * c)
            rhs = jnp.where(row_chunk == (ci % 2), sol[h][pair], 0.0)
            per_chunk.append(_dot(kd_t[h][:, pair], rhs))
        gn.append(per_chunk)

    def scan_and_outputs():
        state = [state_scr[h] for h in heads]
        starts = [[] for _ in heads]
        for ci in range(n_chunk):
            for h in heads:
                starts[h].append(state[h])
            fill(("scan", ci))
            upd = [_dot(gn[h][ci][:, DN_HD:], state[h]) for h in heads]
            state = [state[h] * c_dec[ci][:, hs[h]]
                     + (gn[h][ci][:, :DN_HD] - upd[h]) for h in heads]

        fill("outputs")
        xs = [[_dot(jnp.concatenate([qd[ci * c:(ci + 1) * c, hs[h]],
                                     sol[h][ci * c:(ci + 1) * c, DN_HD:]], axis=0), starts[h][ci])
               for ci in range(n_chunk)] for h in heads]
        outs = []
        for h in heads:
            xq = jnp.concatenate([xs[h][ci][:c] for ci in range(n_chunk)], axis=0)
            xw = jnp.concatenate([xs[h][ci][c:] for ci in range(n_chunk)], axis=0)
            outs.append(xq + _dot(attn[h], sol[h][:, :DN_HD] - xw))
        o = jnp.concatenate(outs, axis=1)
        o = (o * lax.rsqrt(_head_sums(o, ones_bd) * (1.0 / DN_HD) + NORM_EPS) * dnorm
             * _silu(get_z()))
        for h in heads:
            state_scr[h] = state[h]
        return o

    return scan_and_outputs


def _mixer_kernel(x_ref, cos_ref, sin_ref, ones_ref, anorm_ref, wfront_ref, wrep_ref, wtail_ref,
                  lng_ref, lnb_ref, sguw_ref, sgub_ref, sinks_ref, convw_ref, alog_ref, dtb_ref,
                  dnorm_ref, wbr_ref, wout_ref, o_ref, kv_scr, conv_scr, state_scr, merged_scr,
                  gate_scr, wgate_scr, u_scr, vn_scr, probs_scr):
    seq_start = pl.program_id(1) == 0

    @pl.when(jnp.logical_and(pl.program_id(0) == 0, seq_start))
    def _():
        for n in range(3):
            window = wtail_ref[:, n * D_MODEL:(n + 1) * D_MODEL + GATE_OFF].astype(F32)
            wgate_scr[:, n * D_MODEL:(n + 1) * D_MODEL] = window[:, GATE_OFF:].astype(BF16)

    @pl.when(seq_start)
    def _():
        kv_scr[...] = jnp.zeros_like(kv_scr)
        conv_scr[0:SUBLANES, :] = jnp.zeros((SUBLANES, 3 * MIX), F32)
        state_scr[...] = jnp.zeros_like(state_scr)

    x = x_ref[...]
    h = _rmsnorm(x, anorm_ref[...]).astype(BF16)

    def proj(w_ref, lo, width):
        return jnp.dot(h, w_ref[:, lo:lo + width], preferred_element_type=F32)

    def gate(n, lo=0, width=D_MODEL):
        return jax.nn.sigmoid(proj(wgate_scr, n * D_MODEL + lo, width))

    park = {}

    def fill(stage):
        if stage == "prologue":
            qkv_b = proj(wfront_ref, C_QKVB, MIX + 2 * SWA_KV * SWA_HD)
            park["swa"] = _sliding_window_attention(
                qkv_b[:, :MIX], qkv_b[:, MIX:MIX + LANES], qkv_b[:, MIX + LANES:],
                cos_ref[...], sin_ref[...], sinks_ref, kv_scr, probs_scr, seq_start)
            park["uv"] = proj(wfront_ref, C_UV, 2 * MIX)
            gate_scr[0] = gate(0)
        elif stage == "norms":
            gate_scr[1] = gate(1)
            uv = park.pop("uv")
            _spatial_gating_prep(uv[:, :MIX], uv[:, MIX:], lng_ref[...], lnb_ref[...],
                                 u_scr, vn_scr)
        elif stage == "decay":
            park["z"] = proj(wfront_ref, C_Z, MIX)
            next(park["swa"])
            next(park["swa"])
        elif stage[0] == "scan":
            lo = stage[1] * 2 * LANES
            gate_scr[2, :, lo:lo + 2 * LANES] = gate(2, lo, 2 * LANES)

    dn_tail = _gated_deltanet(proj(wfront_ref, C_QKVC, 3 * MIX), lambda: park.pop("z"),
                              proj(wrep_ref, 0, MIX), proj(wrep_ref, MIX, MIX), convw_ref,
                              alog_ref[...], dtb_ref[...], dnorm_ref[...], ones_ref[...],
                              conv_scr, state_scr, fill)

    out_c = dn_tail()
    out_a = _spatial_gating_mix(u_scr, vn_scr, sguw_ref, sgub_ref)
    out_b = next(park.pop("swa"))
    merged = (gate_scr[0] * _dot(out_a, wbr_ref[0]) + gate_scr[1] * _dot(out_b, wbr_ref[1])
              + gate_scr[2] * _dot(out_c, wbr_ref[2]))
    o_ref[...] = x + _dot(merged, wout_ref[...])


def _ffn_kernel(x_ref, fnorm_ref, wgu_ref, wd_ref, final_ref, o_ref, *, apply_final_norm):
    x = x_ref[...]
    h = _rmsnorm(x, fnorm_ref[...]).astype(BF16)
    gu = jnp.dot(h, wgu_ref[...], preferred_element_type=F32)
    act = _silu(gu[:, :D_FF]) * gu[:, D_FF:]
    y = x + _dot(act, wd_ref[...])
    if apply_final_norm:
        y = _rmsnorm(y, final_ref[...])
    o_ref[...] = y


def _resident(shape_tail, layer):
    n = len(shape_tail)
    return pl.BlockSpec((None,) + tuple(shape_tail), lambda *_: (layer,) + (0,) * n,
                        pipeline_mode=pl.Buffered(1))


def _mixer_call(x, rope, params, layer):
    batch, seq, _ = x.shape
    t = MIXER_TILE
    tok = lambda w: pl.BlockSpec((None, t, w), lambda b, i: (b, i, 0))
    in_specs = [
        tok(D_MODEL), tok(LANES), tok(LANES),
        pl.BlockSpec((MXU_DIM, MXU_DIM), lambda b, i: (0, 0), pipeline_mode=pl.Buffered(1)),
        _resident((1, D_MODEL), layer),
        _resident((D_MODEL, C_FRONT_END), layer),
        _resident((D_MODEL, 2 * MIX), layer),
        _resident((D_MODEL, GATE_OFF + 3 * D_MODEL), layer),
        _resident((1, MIX), layer), _resident((1, MIX), layer),
        _resident((SGU_GROUPS, BLOCK, BLOCK), layer), _resident((BLOCK, SGU_GROUPS), layer),
        _resident((SWA_HEADS, LANES), layer),
        _resident((DN_CONV, 3 * MIX), layer),
        _resident((1, MIX), layer), _resident((1, MIX), layer), _resident((1, MIX), layer),
        _resident((3, MIX, D_MODEL), layer), _resident((D_MODEL, D_MODEL), layer),
    ]
    return pl.pallas_call(
        _mixer_kernel,
        out_shape=jax.ShapeDtypeStruct(x.shape, F32),
        grid=(batch, seq // t),
        in_specs=in_specs,
        out_specs=tok(D_MODEL),
        scratch_shapes=[
            pltpu.VMEM((2, BLOCK, LANES), F32),
            pltpu.VMEM((SUBLANES, 3 * MIX), F32),
            pltpu.VMEM((DN_HEADS, DN_HD, DN_HD), F32),
            pltpu.VMEM((t, D_MODEL), F32),
            pltpu.VMEM((3, t, D_MODEL), F32),
            pltpu.VMEM((D_MODEL, 3 * D_MODEL), BF16),
            pltpu.VMEM((t, MIX), F32),
            pltpu.VMEM((t, MIX), BF16),
            pltpu.VMEM((t // BLOCK * SWA_KV, SWA_HEADS // SWA_KV * BLOCK, 2 * BLOCK), BF16),
        ],
        compiler_params=pltpu.CompilerParams(
            dimension_semantics=("arbitrary", "arbitrary"),
            vmem_limit_bytes=VMEM_LIMIT_BYTES),
        name=f"mixer_l{layer}",
    )(x, *rope, params["head_ones"], params["attn_norm"], params["w_in"], params["w_rep"],
      params["w_tail"], params["sgu_ln_g"], params["sgu_ln_b"],
      params["sgu_w"], params["sgu_b"], params["attn_sinks"], params["dn_conv_w"],
      params["dn_a_log"], params["dn_dt_bias"], params["dn_norm"], params["w_branch"],
      params["w_out"])


def _ffn_call(x, params, layer, apply_final_norm):
    batch, seq, _ = x.shape
    t = FFN_TILE
    tok = pl.BlockSpec((None, t, D_MODEL), lambda b, i: (b, i, 0))
    return pl.pallas_call(
        functools.partial(_ffn_kernel, apply_final_norm=apply_final_norm),
        out_shape=jax.ShapeDtypeStruct(x.shape, F32),
        grid=(batch, seq // t),
        in_specs=[
            tok,
            _resident((1, D_MODEL), layer),
            _resident((D_MODEL, 2 * D_FF), layer),
            _resident((D_FF, D_MODEL), layer),
            pl.BlockSpec((1, D_MODEL), lambda b, i: (0, 0)),
        ],
        out_specs=tok,
        compiler_params=pltpu.CompilerParams(
            dimension_semantics=("arbitrary", "arbitrary"),
            vmem_limit_bytes=VMEM_LIMIT_BYTES),
        name=f"ffn_l{layer}",
    )(x, params["ffn_norm"], params["w_gate_up"], params["w_down"], params["final_norm"])


def _rope_tables(positions):
    half = ROPE_DIM // 2
    inv_freq = ROPE_THETA ** (-jnp.arange(0, ROPE_DIM, 2, dtype=F32) / ROPE_DIM)
    ang = positions.astype(F32)[..., None] * inv_freq
    cos, sin = lax.optimization_barrier((jnp.cos(ang), jnp.sin(ang)))
    rotary = (jnp.arange(LANES) % SWA_HD) < ROPE_DIM
    reps = (1,) * (ang.ndim - 1) + (LANES // half,)
    return jnp.where(rotary, jnp.tile(cos, reps), 1.0), jnp.where(rotary, jnp.tile(sin, reps), 0.0)


def _pack_params(attn_norm, w_in, sgu_ln_g, sgu_ln_b, sgu_w, sgu_b, attn_sinks, dn_conv_w,
                 dn_a_log, dn_dt_bias, dn_norm, w_branch, w_out, ffn_norm, w_gate_up, w_down,
                 final_norm):
    depth = w_in.shape[0]
    c_gate = C_FRONT_END + GATE_OFF
    per_head = lambda a: jnp.repeat(a, DN_HD, axis=-1)
    lane_head = jnp.arange(MXU_DIM) // DN_HD
    w_in16 = w_in.astype(BF16)
    return {
        "head_ones": (lane_head[:, None] == lane_head[None, :]).astype(BF16),
        "attn_norm": attn_norm[:, None, :],
        "w_in": w_in16,
        "w_rep": per_head(w_in16[:, :, C_FRONT_END:c_gate]),
        "w_tail": w_in16[:, :, C_FRONT_END:],
        "sgu_ln_g": sgu_ln_g[:, None, :],
        "sgu_ln_b": sgu_ln_b[:, None, :],
        "sgu_w": sgu_w,
        "sgu_b": jnp.swapaxes(sgu_b, 1, 2),
        "attn_sinks": jnp.broadcast_to(attn_sinks[:, :, None], (depth, SWA_HEADS, LANES)),
        "dn_conv_w": dn_conv_w,
        "dn_a_log": per_head(dn_a_log)[:, None, :],
        "dn_dt_bias": per_head(dn_dt_bias)[:, None, :],
        "dn_norm": jnp.tile(dn_norm, (1, DN_HEADS))[:, None, :],
        "w_branch": w_branch.astype(BF16),
        "w_out": w_out.astype(BF16),
        "ffn_norm": ffn_norm[:, None, :],
        "w_gate_up": w_gate_up.astype(BF16),
        "w_down": w_down.astype(BF16),
        "final_norm": final_norm[None, :],
    }


def kernel(x, positions, attn_norm, w_in, sgu_ln_g, sgu_ln_b, sgu_w, sgu_b, attn_sinks,
           dn_conv_w, dn_a_log, dn_dt_bias, dn_norm, w_branch, w_out, ffn_norm, w_gate_up,
           w_down, final_norm):
    assert MIXER_TILE == MXU_DIM == 4 * DN_CHUNK
    assert x.shape[1] % MIXER_TILE == 0 and x.shape[1] % FFN_TILE == 0
    assert x.shape[2] == D_MODEL
    assert w_in.shape[2] == C_FRONT_END + GATE_OFF + 3 * D_MODEL
    depth = w_in.shape[0]
    params = _pack_params(attn_norm, w_in, sgu_ln_g, sgu_ln_b, sgu_w, sgu_b, attn_sinks,
                          dn_conv_w, dn_a_log, dn_dt_bias, dn_norm, w_branch, w_out, ffn_norm,
                          w_gate_up, w_down, final_norm)
    rope = _rope_tables(positions)
    for layer in range(depth):
        x = _mixer_call(x, rope, params, layer)
        x = _ffn_call(x, params, layer, apply_final_norm=(layer == depth - 1))
    return x
```

```python
import functools

import jax
import jax.numpy as jnp
from jax import lax
from jax.experimental import pallas as pl
from jax.experimental.pallas import tpu as pltpu

D_MODEL = 1024
MIX = 512
NORM_EPS = 1e-6

SGU_GROUPS = 4
BLOCK = 128

SWA_HEADS = 8
SWA_KV = 2
SWA_HD = 64
ROPE_THETA = 500000.0
ROPE_DIM = 16

DN_HEADS = 4
DN_HD = 128
DN_CONV = 4
DN_CHUNK = 64

D_FF = 2816

LANES = 128
SUBLANES = 8
MXU_DIM = 256

C_UV = 0
C_QKVB = C_UV + 2 * MIX
C_QKVC = C_QKVB + MIX + 2 * SWA_KV * SWA_HD
C_Z = C_QKVC + 3 * MIX
C_FRONT_END = C_Z + MIX
GATE_OFF = 2 * DN_HEADS

MIXER_TILE = 256
FFN_TILE = 512
VMEM_LIMIT_BYTES = 56 * 1024 * 1024

BF16 = jnp.bfloat16
F32 = jnp.float32
_GELU_C = 0.7978845608028654


def _dot(a, b):
    return jnp.dot(a.astype(BF16), b.astype(BF16), preferred_element_type=F32)


def _dot_nt(a, b):
    return lax.dot_general(a.astype(BF16), b.astype(BF16), (((1,), (1,)), ((), ())),
                           preferred_element_type=F32)


def _rmsnorm(x, g):
    return x * lax.rsqrt(jnp.mean(x * x, axis=-1, keepdims=True) + NORM_EPS) * g


def _gelu_tanh(x):
    inner = x * (_GELU_C + (_GELU_C * 0.044715) * (x * x))
    return (0.5 * x) * (1.0 + jnp.tanh(inner))


def _silu(x):
    return x * jax.nn.sigmoid(x)


def _softplus(x):
    return jnp.maximum(x, 0.0) + jnp.log(1.0 + jnp.exp(-jnp.abs(x)))


def _iota(shape, dim):
    return lax.broadcasted_iota(jnp.int32, shape, dim)


def _spatial_gating_prep(u_pre, v_pre, ln_g, ln_b, u_scr, vn_scr):
    u_scr[...] = _gelu_tanh(u_pre)
    v = _gelu_tanh(v_pre)
    vc = v - jnp.mean(v, axis=-1, keepdims=True)
    vn_scr[...] = (vc * lax.rsqrt(jnp.mean(vc * vc, axis=-1, keepdims=True) + NORM_EPS) * ln_g
                   + ln_b).astype(BF16)


def _spatial_gating_mix(u_scr, vn_scr, sguw_ref, sgub_ref):
    u = u_scr[...]
    vn = vn_scr[...]
    t = u.shape[0]
    causal = _iota((BLOCK, BLOCK), 0) >= _iota((BLOCK, BLOCK), 1)
    groups = []
    for g in range(SGU_GROUPS):
        w = jnp.where(causal, sguw_ref[g], 0.0).astype(BF16)
        bias = sgub_ref[:, g:g + 1]
        cols = slice(g * LANES, (g + 1) * LANES)
        mixed = [_dot(w, vn[blk * BLOCK:(blk + 1) * BLOCK, cols]) + bias
                 for blk in range(t // BLOCK)]
        groups.append(u[:, cols] * jnp.concatenate(mixed, axis=0))
    return jnp.concatenate(groups, axis=1)


def _rope(x, cos, sin):
    half = ROPE_DIM // 2
    first_half = (_iota((1, LANES), 1) % SWA_HD) < half
    sin_a = jnp.where(first_half, -sin, 0.0)
    sin_b = jnp.where(first_half, 0.0, sin)
    n = x.shape[1] // LANES
    if n > 1:
        cos = jnp.concatenate([cos] * n, axis=1)
        sin_a = jnp.concatenate([sin_a] * n, axis=1)
        sin_b = jnp.concatenate([sin_b] * n, axis=1)
    width = x.shape[1]
    return (x * cos + pltpu.roll(x, width - half, 1) * sin_a + pltpu.roll(x, half, 1) * sin_b)


def _sliding_window_attention(q, k, v, cos, sin, sinks_ref, ones, kv_scr, probs_scr, seq_start):
    t = q.shape[0]
    n_blk = t // BLOCK
    group = SWA_HEADS // SWA_KV
    q = _rope(q, cos, sin) * (SWA_HD ** -0.5)
    k = _rope(k, cos, sin)

    kcat = jnp.concatenate([kv_scr[0], k], axis=0)
    vcat = jnp.concatenate([kv_scr[1], v], axis=0)

    lane = _iota((1, LANES), 1)
    lo = lane < SWA_HD
    krot = pltpu.roll(kcat, SWA_HD, 1)
    vrot = pltpu.roll(vcat, SWA_HD, 1)
    kdup = [jnp.where(lo, kcat, krot).astype(BF16), jnp.where(lo, krot, kcat).astype(BF16)]
    vlo = [jnp.where(lo, vcat, 0.0).astype(BF16), jnp.where(lo, vrot, 0.0).astype(BF16)]
    vhi = [jnp.where(lo, 0.0, vrot).astype(BF16), jnp.where(lo, 0.0, vcat).astype(BF16)]

    rows4 = group * BLOCK
    cur_mask = (_iota((rows4, BLOCK), 0) % BLOCK) >= _iota((rows4, BLOCK), 1)
    neg_inf = jnp.float32(-jnp.inf)
    units = [(blk, kh) for blk in range(n_blk) for kh in range(SWA_KV)]

    lhs = []
    for blk, kh in units:
        rows = slice(blk * BLOCK, (blk + 1) * BLOCK)
        s0 = q[rows, (2 * kh) * LANES:(2 * kh + 1) * LANES]
        s1 = q[rows, (2 * kh + 1) * LANES:(2 * kh + 2) * LANES]
        lhs.append(jnp.concatenate([jnp.where(lo, s0, 0.0), jnp.where(lo, s1, 0.0),
                                    jnp.where(lo, 0.0, s0), jnp.where(lo, 0.0, s1)],
                                   axis=0).astype(BF16))
    logits = [_dot_nt(lhs[i], kdup[kh][blk * BLOCK:(blk + 2) * BLOCK])
              for i, (blk, kh) in enumerate(units)]
    yield
    n_unit = len(units)
    sinks = [jnp.concatenate(
        [jnp.broadcast_to(sinks_ref[group * kh + 2 * j + half:group * kh + 2 * j + half + 1, :],
                          (BLOCK, LANES)) for half in range(2) for j in range(2)], axis=0)
        for kh in range(SWA_KV)]
    act = []
    for i, (blk, kh) in enumerate(units):
        l_prev = logits[i][:, :BLOCK]
        if blk == 0:
            l_prev = jnp.where(seq_start, neg_inf, l_prev)
        act.append(jnp.where(cur_mask, logits[i][:, BLOCK:], l_prev))
    m = [jnp.maximum(jnp.broadcast_to(jnp.max(act[i], axis=1, keepdims=True), act[i].shape),
                     sinks[units[i][1]]) for i in range(n_unit)]
    p = [jnp.exp(act[i] - m[i]) for i in range(n_unit)]
    den = [_dot(p[i], ones) + jnp.exp(sinks[units[i][1]] - m[i]) for i in range(n_unit)]
    for i in range(n_unit):
        pn = p[i] * (1.0 / den[i])
        probs_scr[i] = jnp.concatenate([jnp.where(cur_mask, 0.0, pn), jnp.where(cur_mask, pn, 0.0)],
                                       axis=1).astype(BF16)
    yield
    probs = [probs_scr[i] for i in range(len(units))]
    out_rows = []
    for blk in range(n_blk):
        slabs = []
        for kh in range(SWA_KV):
            i = blk * SWA_KV + kh
            keys = slice(blk * BLOCK, (blk + 2) * BLOCK)
            res = (_dot(probs[i][:2 * BLOCK], vlo[kh][keys])
                   + _dot(probs[i][2 * BLOCK:], vhi[kh][keys]))
            slabs += [res[:BLOCK], res[BLOCK:]]
        out_rows.append(jnp.concatenate(slabs, axis=1))
    kv_scr[0] = k[t - BLOCK:, :]
    kv_scr[1] = v[t - BLOCK:, :]
    yield jnp.concatenate(out_rows, axis=0)


def _chunk_cumsum(x):
    row = _iota(x.shape, 0) % DN_CHUNK
    shift = 1
    while shift < DN_CHUNK:
        x = x + jnp.where(row >= shift, pltpu.roll(x, shift, 0), 0.0)
        shift *= 2
    return x


def _head_sums(x, ones_bd):
    xx = x * x
    return jnp.concatenate([_dot(xx[:, i * MXU_DIM:(i + 1) * MXU_DIM], ones_bd)
                            for i in range(x.shape[1] // MXU_DIM)], axis=1)


def _gated_deltanet(qkv, get_z, beta_logit, decay_logit, convw_ref, alog, dtb, dnorm, ones_bd,
                    conv_scr, state_scr, fill):
    t = qkv.shape[0]
    c = DN_CHUNK
    n_chunk = t // c
    pad = SUBLANES
    heads = range(DN_HEADS)
    hs = [slice(h * DN_HD, (h + 1) * DN_HD) for h in heads]

    xa = jnp.concatenate([conv_scr[0:pad, :], qkv], axis=0)
    y = qkv * convw_ref[DN_CONV - 1:DN_CONV, :]
    for i in range(DN_CONV - 1):
        y = y + pltpu.roll(xa, DN_CONV - 1 - i, 0)[pad:, :] * convw_ref[i:i + 1, :]
    conv_scr[0:pad, :] = qkv[t - pad:, :]
    y = _silu(y)

    q = y[:, :MIX]
    k = y[:, MIX:2 * MIX]
    v = y[:, 2 * MIX:]
    fill("prologue")
    q = q * lax.rsqrt(_head_sums(q, ones_bd) + NORM_EPS) * (DN_HD ** -0.5)
    k = k * lax.rsqrt(_head_sums(k, ones_bd) + NORM_EPS)

    beta = jax.nn.sigmoid(beta_logit)
    g = -jnp.exp(alog) * _softplus(decay_logit + dtb)
    gc = _chunk_cumsum(g)
    egc = jnp.exp(gc)
    g_last = jnp.concatenate(
        [jnp.broadcast_to(gc[(ci + 1) * c - 1:(ci + 1) * c, :], (c, MIX)) for ci in range(n_chunk)],
        axis=0)
    kb = k * beta
    vb = v * beta
    kbe = kb * egc
    qd = q * egc
    kd = k * jnp.exp(g_last - gc)
    c_dec = [jnp.exp(g_last[ci * c:ci * c + 1, :]) for ci in range(n_chunk)]

    lane = _iota((1, LANES), 1)
    g4 = gc[:, hs[DN_HEADS - 1]]
    for h in reversed(range(DN_HEADS - 1)):
        g4 = jnp.where(lane == h, gc[:, hs[h]], g4)
    g4_t = g4.T
    kd_t = [kd[:, hs[h]].T.astype(BF16) for h in heads]
    fill("norms")

    ri = _iota((t, t), 0)
    ci_ = _iota((t, t), 1)
    same = (ri // c) == (ci_ // c)
    incl = jnp.logical_and(same, ri >= ci_)
    strict = jnp.logical_and(same, ri > ci_)
    neg_inf = jnp.float32(-jnp.inf)

    aq = [_dot_nt(jnp.concatenate([kb[:, hs[h]], q[:, hs[h]]], axis=0), k[:, hs[h]])
          for h in heads]
    lower, attn, sol = [], [], []
    for h in heads:
        g_col = gc[:, hs[h]]
        diff = jnp.concatenate([g_col] * (t // DN_HD), axis=1) - g4_t[h:h + 1, :]
        decay = jnp.exp(jnp.where(incl, diff, neg_inf))
        lower.append(jnp.where(strict, aq[h][:t] * decay, 0.0).astype(BF16))
        attn.append((aq[h][t:] * decay).astype(BF16))
        sol.append(jnp.concatenate([vb[:, hs[h]], kbe[:, hs[h]]], axis=1))
    fill("decay")

    sol = [sol[h] - _dot(lower[h], sol[h]) for h in heads]
    power = [_dot(lower[h], lower[h]).astype(BF16) for h in heads]
    n_sq = 2
    while True:
        sol = [sol[h] + _dot(power[h], sol[h]) for h in heads]
        if 2 * n_sq >= c:
            break
        power = [_dot(power[h], power[h]).astype(BF16) for h in heads]
        n_sq *= 2

    row_chunk = _iota((2 * c, 1), 0) // c
    gn = []
    for h in heads:
        per_chunk = []
        for ci in range(n_chunk):
            pair = slice((ci // 2) * 2 * c, (ci // 2 + 1) * 2 * c)
            rhs = jnp.where(row_chunk == (ci % 2), sol[h][pair], 0.0)
            per_chunk.append(_dot(kd_t[h][:, pair], rhs))
        gn.append(per_chunk)

    def scan_and_outputs():
        state = [state_scr[h] for h in heads]
        starts = [[] for _ in heads]
        for ci in range(n_chunk):
            for h in heads:
                starts[h].append(state[h])
            fill(("scan", ci))
            upd = [_dot(gn[h][ci][:, DN_HD:], state[h]) for h in heads]
            state = [state[h] * c_dec[ci][:, hs[h]]
                     + (gn[h][ci][:, :DN_HD] - upd[h]) for h in heads]

        fill("outputs")
        xs = [[_dot(jnp.concatenate([qd[ci * c:(ci + 1) * c, hs[h]],
                                     sol[h][ci * c:(ci + 1) * c, DN_HD:]], axis=0), starts[h][ci])
               for ci in range(n_chunk)] for h in heads]
        outs = []
        for h in heads:
            xq = jnp.concatenate([xs[h][ci][:c] for ci in range(n_chunk)], axis=0)
            xw = jnp.concatenate([xs[h][ci][c:] for ci in range(n_chunk)], axis=0)
            outs.append(xq + _dot(attn[h], sol[h][:, :DN_HD] - xw))
        o = jnp.concatenate(outs, axis=1)
        o = (o * lax.rsqrt(_head_sums(o, ones_bd) * (1.0 / DN_HD) + NORM_EPS) * dnorm
             * _silu(get_z()))
        for h in heads:
            state_scr[h] = state[h]
        return o

    return scan_and_outputs


def _mixer_kernel(x_ref, cos_ref, sin_ref, ones_ref, anorm_ref, wfront_ref, wrep_ref, wtail_ref,
                  lng_ref, lnb_ref, sguw_ref, sgub_ref, sinks_ref, convw_ref, alog_ref, dtb_ref,
                  dnorm_ref, wbr_ref, wout_ref, o_ref, kv_scr, conv_scr, state_scr, merged_scr,
                  gate_scr, wgate_scr, u_scr, vn_scr, probs_scr):
    seq_start = pl.program_id(1) == 0

    @pl.when(jnp.logical_and(pl.program_id(0) == 0, seq_start))
    def _():
        for n in range(3):
            window = wtail_ref[:, n * D_MODEL:(n + 1) * D_MODEL + GATE_OFF].astype(F32)
            wgate_scr[:, n * D_MODEL:(n + 1) * D_MODEL] = window[:, GATE_OFF:].astype(BF16)

    @pl.when(seq_start)
    def _():
        kv_scr[...] = jnp.zeros_like(kv_scr)
        conv_scr[0:SUBLANES, :] = jnp.zeros((SUBLANES, 3 * MIX), F32)
        state_scr[...] = jnp.zeros_like(state_scr)

    x = x_ref[...]
    h = _rmsnorm(x, anorm_ref[...]).astype(BF16)

    def proj(w_ref, lo, width):
        return jnp.dot(h, w_ref[:, lo:lo + width], preferred_element_type=F32)

    def gate(n, lo=0, width=D_MODEL):
        return jax.nn.sigmoid(proj(wgate_scr, n * D_MODEL + lo, width))

    park = {}

    def fill(stage):
        if stage == "prologue":
            qkv_b = proj(wfront_ref, C_QKVB, MIX + 2 * SWA_KV * SWA_HD)
            park["swa"] = _sliding_window_attention(
                qkv_b[:, :MIX], qkv_b[:, MIX:MIX + LANES], qkv_b[:, MIX + LANES:],
                cos_ref[...], sin_ref[...], sinks_ref, ones_ref[0:LANES, 0:LANES], kv_scr,
                probs_scr, seq_start)
            park["uv"] = proj(wfront_ref, C_UV, 2 * MIX)
            gate_scr[0] = gate(0)
            next(park["swa"])
        elif stage == "norms":
            gate_scr[1] = gate(1)
        elif stage == "decay":
            park["z"] = proj(wfront_ref, C_Z, MIX)
            next(park["swa"])
            uv = park.pop("uv")
            _spatial_gating_prep(uv[:, :MIX], uv[:, MIX:], lng_ref[...], lnb_ref[...],
                                 u_scr, vn_scr)
        elif stage[0] == "scan":
            lo = stage[1] * 2 * LANES
            gate_scr[2, :, lo:lo + 2 * LANES] = gate(2, lo, 2 * LANES)

    dn_tail = _gated_deltanet(proj(wfront_ref, C_QKVC, 3 * MIX), lambda: park.pop("z"),
                              proj(wrep_ref, 0, MIX), proj(wrep_ref, MIX, MIX), convw_ref,
                              alog_ref[...], dtb_ref[...], dnorm_ref[...], ones_ref[...],
                              conv_scr, state_scr, fill)

    out_c = dn_tail()
    out_a = _spatial_gating_mix(u_scr, vn_scr, sguw_ref, sgub_ref)
    out_b = next(park.pop("swa"))
    merged = (gate_scr[0] * _dot(out_a, wbr_ref[0]) + gate_scr[1] * _dot(out_b, wbr_ref[1])
              + gate_scr[2] * _dot(out_c, wbr_ref[2]))
    o_ref[...] = x + _dot(merged, wout_ref[...])


def _ffn_kernel(x_ref, fnorm_ref, wgu_ref, wd_ref, final_ref, o_ref, *, apply_final_norm):
    x = x_ref[...]
    h = _rmsnorm(x, fnorm_ref[...]).astype(BF16)
    gu = jnp.dot(h, wgu_ref[...], preferred_element_type=F32)
    act = _silu(gu[:, :D_FF]) * gu[:, D_FF:]
    y = x + _dot(act, wd_ref[...])
    if apply_final_norm:
        y = _rmsnorm(y, final_ref[...])
    o_ref[...] = y


def _resident(shape_tail, layer):
    n = len(shape_tail)
    return pl.BlockSpec((None,) + tuple(shape_tail), lambda *_: (layer,) + (0,) * n,
                        pipeline_mode=pl.Buffered(1))


def _mixer_call(x, rope, params, layer):
    batch, seq, _ = x.shape
    t = MIXER_TILE
    tok = lambda w: pl.BlockSpec((None, t, w), lambda b, i: (b, i, 0))
    in_specs = [
        tok(D_MODEL), tok(LANES), tok(LANES),
        pl.BlockSpec((MXU_DIM, MXU_DIM), lambda b, i: (0, 0), pipeline_mode=pl.Buffered(1)),
        _resident((1, D_MODEL), layer),
        _resident((D_MODEL, C_FRONT_END), layer),
        _resident((D_MODEL, 2 * MIX), layer),
        _resident((D_MODEL, GATE_OFF + 3 * D_MODEL), layer),
        _resident((1, MIX), layer), _resident((1, MIX), layer),
        _resident((SGU_GROUPS, BLOCK, BLOCK), layer), _resident((BLOCK, SGU_GROUPS), layer),
        _resident((SWA_HEADS, LANES), layer),
        _resident((DN_CONV, 3 * MIX), layer),
        _resident((1, MIX), layer), _resident((1, MIX), layer), _resident((1, MIX), layer),
        _resident((3, MIX, D_MODEL), layer), _resident((D_MODEL, D_MODEL), layer),
    ]
    return pl.pallas_call(
        _mixer_kernel,
        out_shape=jax.ShapeDtypeStruct(x.shape, F32),
        grid=(batch, seq // t),
        in_specs=in_specs,
        out_specs=tok(D_MODEL),
        scratch_shapes=[
            pltpu.VMEM((2, BLOCK, LANES), F32),
            pltpu.VMEM((SUBLANES, 3 * MIX), F32),
            pltpu.VMEM((DN_HEADS, DN_HD, DN_HD), F32),
            pltpu.VMEM((t, D_MODEL), F32),
            pltpu.VMEM((3, t, D_MODEL), F32),
            pltpu.VMEM((D_MODEL, 3 * D_MODEL), BF16),
            pltpu.VMEM((t, MIX), F32),
            pltpu.VMEM((t, MIX), BF16),
            pltpu.VMEM((t // BLOCK * SWA_KV, SWA_HEADS // SWA_KV * BLOCK, 2 * BLOCK), BF16),
        ],
        compiler_params=pltpu.CompilerParams(
            dimension_semantics=("arbitrary", "arbitrary"),
            vmem_limit_bytes=VMEM_LIMIT_BYTES),
        name=f"mixer_l{layer}",
    )(x, *rope, params["head_ones"], params["attn_norm"], params["w_in"], params["w_rep"],
      params["w_tail"], params["sgu_ln_g"], params["sgu_ln_b"],
      params["sgu_w"], params["sgu_b"], params["attn_sinks"], params["dn_conv_w"],
      params["dn_a_log"], params["dn_dt_bias"], params["dn_norm"], params["w_branch"],
      params["w_out"])


def _ffn_call(x, params, layer, apply_final_norm):
    batch, seq, _ = x.shape
    t = FFN_TILE
    tok = pl.BlockSpec((None, t, D_MODEL), lambda b, i: (b, i, 0))
    return pl.pallas_call(
        functools.partial(_ffn_kernel, apply_final_norm=apply_final_norm),
        out_shape=jax.ShapeDtypeStruct(x.shape, F32),
        grid=(batch, seq // t),
        in_specs=[
            tok,
            _resident((1, D_MODEL), layer),
            _resident((D_MODEL, 2 * D_FF), layer),
            _resident((D_FF, D_MODEL), layer),
            pl.BlockSpec((1, D_MODEL), lambda b, i: (0, 0)),
        ],
        out_specs=tok,
        compiler_params=pltpu.CompilerParams(
            dimension_semantics=("arbitrary", "arbitrary"),
            vmem_limit_bytes=VMEM_LIMIT_BYTES),
        name=f"ffn_l{layer}",
    )(x, params["ffn_norm"], params["w_gate_up"], params["w_down"], params["final_norm"])


def _rope_tables(positions):
    half = ROPE_DIM // 2
    inv_freq = ROPE_THETA ** (-jnp.arange(0, ROPE_DIM, 2, dtype=F32) / ROPE_DIM)
    ang = positions.astype(F32)[..., None] * inv_freq
    cos, sin = lax.optimization_barrier((jnp.cos(ang), jnp.sin(ang)))
    rotary = (jnp.arange(LANES) % SWA_HD) < ROPE_DIM
    reps = (1,) * (ang.ndim - 1) + (LANES // half,)
    return jnp.where(rotary, jnp.tile(cos, reps), 1.0), jnp.where(rotary, jnp.tile(sin, reps), 0.0)


def _pack_params(attn_norm, w_in, sgu_ln_g, sgu_ln_b, sgu_w, sgu_b, attn_sinks, dn_conv_w,
                 dn_a_log, dn_dt_bias, dn_norm, w_branch, w_out, ffn_norm, w_gate_up, w_down,
                 final_norm):
    depth = w_in.shape[0]
    c_gate = C_FRONT_END + GATE_OFF
    per_head = lambda a: jnp.repeat(a, DN_HD, axis=-1)
    lane_head = jnp.arange(MXU_DIM) // DN_HD
    w_tail = w_in[:, :, C_FRONT_END:].astype(BF16)
    return {
        "head_ones": (lane_head[:, None] == lane_head[None, :]).astype(BF16),
        "attn_norm": attn_norm[:, None, :],
        "w_in": w_in[:, :, :C_FRONT_END].astype(BF16),
        "w_rep": per_head(w_tail[:, :, :GATE_OFF]),
        "w_tail": w_tail,
        "sgu_ln_g": sgu_ln_g[:, None, :],
        "sgu_ln_b": sgu_ln_b[:, None, :],
        "sgu_w": sgu_w,
        "sgu_b": jnp.swapaxes(sgu_b, 1, 2),
        "attn_sinks": jnp.broadcast_to(attn_sinks[:, :, None], (depth, SWA_HEADS, LANES)),
        "dn_conv_w": dn_conv_w,
        "dn_a_log": per_head(dn_a_log)[:, None, :],
        "dn_dt_bias": per_head(dn_dt_bias)[:, None, :],
        "dn_norm": jnp.tile(dn_norm, (1, DN_HEADS))[:, None, :],
        "w_branch": w_branch.astype(BF16),
        "w_out": w_out.astype(BF16),
        "ffn_norm": ffn_norm[:, None, :],
        "w_gate_up": w_gate_up.astype(BF16),
        "w_down": w_down.astype(BF16),
        "final_norm": final_norm[None, :],
    }


def kernel(x, positions, attn_norm, w_in, sgu_ln_g, sgu_ln_b, sgu_w, sgu_b, attn_sinks,
           dn_conv_w, dn_a_log, dn_dt_bias, dn_norm, w_branch, w_out, ffn_norm, w_gate_up,
           w_down, final_norm):
    assert MIXER_TILE == MXU_DIM == 4 * DN_CHUNK
    assert x.shape[1] % MIXER_TILE == 0 and x.shape[1] % FFN_TILE == 0
    assert x.shape[2] == D_MODEL
    assert w_in.shape[2] == C_FRONT_END + GATE_OFF + 3 * D_MODEL
    depth = w_in.shape[0]
    params = _pack_params(attn_norm, w_in, sgu_ln_g, sgu_ln_b, sgu_w, sgu_b, attn_sinks,
                          dn_conv_w, dn_a_log, dn_dt_bias, dn_norm, w_branch, w_out, ffn_norm,
                          w_gate_up, w_down, final_norm)
    rope = _rope_tables(positions)
    for layer in range(depth):
        x = _mixer_call(x, rope, params, layer)
        x = _ffn_call(x, params, layer, apply_final_norm=(layer == depth - 1))
    return x
```

```python
import functools

import jax
import jax.numpy as jnp
from jax import lax
from jax.experimental import pallas as pl
from jax.experimental.pallas import tpu as pltpu

D_MODEL = 1024
MIX = 512
NORM_EPS = 1e-6

SGU_GROUPS = 4
BLOCK = 128

SWA_HEADS = 8
SWA_KV = 2
SWA_HD = 64
ROPE_THETA = 500000.0
ROPE_DIM = 16

DN_HEADS = 4
DN_HD = 128
DN_CONV = 4
DN_CHUNK = 64

D_FF = 2816

LANES = 128
SUBLANES = 8
MXU_DIM = 256

C_UV = 0
C_QKVB = C_UV + 2 * MIX
C_QKVC = C_QKVB + MIX + 2 * SWA_KV * SWA_HD
C_Z = C_QKVC + 3 * MIX
C_FRONT_END = C_Z + MIX
GATE_OFF = 2 * DN_HEADS

MIXER_TILE = 256
FFN_TILE = 512
VMEM_LIMIT_BYTES = 56 * 1024 * 1024

BF16 = jnp.bfloat16
F32 = jnp.float32
_GELU_C = 0.7978845608028654


def _dot(a, b):
    return jnp.dot(a.astype(BF16), b.astype(BF16), preferred_element_type=F32)


def _dot_nt(a, b):
    return lax.dot_general(a.astype(BF16), b.astype(BF16), (((1,), (1,)), ((), ())),
                           preferred_element_type=F32)


def _rmsnorm(x, g):
    return x * lax.rsqrt(jnp.mean(x * x, axis=-1, keepdims=True) + NORM_EPS) * g


def _gelu_tanh(x):
    inner = x * (_GELU_C + (_GELU_C * 0.044715) * (x * x))
    return (0.5 * x) * (1.0 + jnp.tanh(inner))


def _silu(x):
    return x * jax.nn.sigmoid(x)


def _softplus(x):
    return jnp.maximum(x, 0.0) + jnp.log(1.0 + jnp.exp(-jnp.abs(x)))


def _iota(shape, dim):
    return lax.broadcasted_iota(jnp.int32, shape, dim)


def _spatial_gating_prep(u_pre, v_pre, ln_g, ln_b, u_scr, vn_scr):
    u_scr[...] = _gelu_tanh(u_pre)
    v = _gelu_tanh(v_pre)
    vc = v - jnp.mean(v, axis=-1, keepdims=True)
    vn_scr[...] = (vc * lax.rsqrt(jnp.mean(vc * vc, axis=-1, keepdims=True) + NORM_EPS) * ln_g
                   + ln_b).astype(BF16)


def _spatial_gating_mix(u_scr, vn_scr, sguw_ref, sgub_ref):
    u = u_scr[...]
    vn = vn_scr[...]
    t = u.shape[0]
    causal = _iota((BLOCK, BLOCK), 0) >= _iota((BLOCK, BLOCK), 1)
    groups = []
    for g in range(SGU_GROUPS):
        w = jnp.where(causal, sguw_ref[g], 0.0).astype(BF16)
        bias = sgub_ref[:, g:g + 1]
        cols = slice(g * LANES, (g + 1) * LANES)
        mixed = [_dot(w, vn[blk * BLOCK:(blk + 1) * BLOCK, cols]) + bias
                 for blk in range(t // BLOCK)]
        groups.append(u[:, cols] * jnp.concatenate(mixed, axis=0))
    return jnp.concatenate(groups, axis=1)


def _rope(x, cos, sin):
    half = ROPE_DIM // 2
    first_half = (_iota((1, LANES), 1) % SWA_HD) < half
    sin_a = jnp.where(first_half, -sin, 0.0)
    sin_b = jnp.where(first_half, 0.0, sin)
    n = x.shape[1] // LANES
    if n > 1:
        cos = jnp.concatenate([cos] * n, axis=1)
        sin_a = jnp.concatenate([sin_a] * n, axis=1)
        sin_b = jnp.concatenate([sin_b] * n, axis=1)
    width = x.shape[1]
    return (x * cos + pltpu.roll(x, width - half, 1) * sin_a + pltpu.roll(x, half, 1) * sin_b)


def _sliding_window_attention(q, k, v, cos, sin, sinks_ref, ones, kv_scr, probs_scr, seq_start):
    t = q.shape[0]
    n_blk = t // BLOCK
    group = SWA_HEADS // SWA_KV
    q = _rope(q, cos, sin) * (SWA_HD ** -0.5)
    k = _rope(k, cos, sin)

    kcat = jnp.concatenate([kv_scr[0], k], axis=0)
    vcat = jnp.concatenate([kv_scr[1], v], axis=0)

    lane = _iota((1, LANES), 1)
    lo = lane < SWA_HD
    krot = pltpu.roll(kcat, SWA_HD, 1)
    vrot = pltpu.roll(vcat, SWA_HD, 1)
    kdup = [jnp.where(lo, kcat, krot).astype(BF16), jnp.where(lo, krot, kcat).astype(BF16)]
    vlo = [jnp.where(lo, vcat, 0.0).astype(BF16), jnp.where(lo, vrot, 0.0).astype(BF16)]
    vhi = [jnp.where(lo, 0.0, vrot).astype(BF16), jnp.where(lo, 0.0, vcat).astype(BF16)]

    rows4 = group * BLOCK
    cur_mask = (_iota((rows4, BLOCK), 0) % BLOCK) >= _iota((rows4, BLOCK), 1)
    neg_inf = jnp.float32(-jnp.inf)
    units = [(blk, kh) for blk in range(n_blk) for kh in range(SWA_KV)]

    lhs = []
    for blk, kh in units:
        rows = slice(blk * BLOCK, (blk + 1) * BLOCK)
        s0 = q[rows, (2 * kh) * LANES:(2 * kh + 1) * LANES]
        s1 = q[rows, (2 * kh + 1) * LANES:(2 * kh + 2) * LANES]
        lhs.append(jnp.concatenate([jnp.where(lo, s0, 0.0), jnp.where(lo, s1, 0.0),
                                    jnp.where(lo, 0.0, s0), jnp.where(lo, 0.0, s1)],
                                   axis=0).astype(BF16))
    logits = [_dot_nt(lhs[i], kdup[kh][blk * BLOCK:(blk + 2) * BLOCK])
              for i, (blk, kh) in enumerate(units)]
    yield
    n_unit = len(units)
    sinks = [jnp.concatenate(
        [jnp.broadcast_to(sinks_ref[group * kh + 2 * j + half:group * kh + 2 * j + half + 1, :],
                          (BLOCK, LANES)) for half in range(2) for j in range(2)], axis=0)
        for kh in range(SWA_KV)]
    act = []
    for i, (blk, kh) in enumerate(units):
        l_prev = logits[i][:, :BLOCK]
        if blk == 0:
            l_prev = jnp.where(seq_start, neg_inf, l_prev)
        act.append(jnp.where(cur_mask, logits[i][:, BLOCK:], l_prev))
    m = [jnp.maximum(jnp.broadcast_to(jnp.max(act[i], axis=1, keepdims=True), act[i].shape),
                     sinks[units[i][1]]) for i in range(n_unit)]
    p = [jnp.exp(act[i] - m[i]) for i in range(n_unit)]
    den = [_dot(p[i], ones) + jnp.exp(sinks[units[i][1]] - m[i]) for i in range(n_unit)]
    for i in range(n_unit):
        pn = p[i] * (1.0 / den[i])
        probs_scr[i] = jnp.concatenate([jnp.where(cur_mask, 0.0, pn), jnp.where(cur_mask, pn, 0.0)],
                                       axis=1).astype(BF16)
    yield
    probs = [probs_scr[i] for i in range(len(units))]
    out_rows = []
    for blk in range(n_blk):
        slabs = []
        for kh in range(SWA_KV):
            i = blk * SWA_KV + kh
            keys = slice(blk * BLOCK, (blk + 2) * BLOCK)
            res = (_dot(probs[i][:2 * BLOCK], vlo[kh][keys])
                   + _dot(probs[i][2 * BLOCK:], vhi[kh][keys]))
            slabs += [res[:BLOCK], res[BLOCK:]]
        out_rows.append(jnp.concatenate(slabs, axis=1))
    kv_scr[0] = k[t - BLOCK:, :]
    kv_scr[1] = v[t - BLOCK:, :]
    yield jnp.concatenate(out_rows, axis=0)


def _chunk_cumsum(x):
    row = _iota(x.shape, 0) % DN_CHUNK
    shift = 1
    while shift < DN_CHUNK:
        x = x + jnp.where(row >= shift, pltpu.roll(x, shift, 0), 0.0)
        shift *= 2
    return x


def _head_sums(x, ones_bd):
    xx = x * x
    return jnp.concatenate([_dot(xx[:, i * MXU_DIM:(i + 1) * MXU_DIM], ones_bd)
                            for i in range(x.shape[1] // MXU_DIM)], axis=1)


def _gated_deltanet(qkv, get_z, beta_logit, decay_logit, convw_ref, alog, dtb, dnorm, ones_bd,
                    conv_scr, state_scr, fill):
    t = qkv.shape[0]
    c = DN_CHUNK
    n_chunk = t // c
    pad = SUBLANES
    heads = range(DN_HEADS)
    hs = [slice(h * DN_HD, (h + 1) * DN_HD) for h in heads]

    xa = jnp.concatenate([conv_scr[0:pad, :], qkv], axis=0)
    y = qkv * convw_ref[DN_CONV - 1:DN_CONV, :]
    for i in range(DN_CONV - 1):
        y = y + pltpu.roll(xa, DN_CONV - 1 - i, 0)[pad:, :] * convw_ref[i:i + 1, :]
    conv_scr[0:pad, :] = qkv[t - pad:, :]
    y = _silu(y)

    q = y[:, :MIX]
    k = y[:, MIX:2 * MIX]
    v = y[:, 2 * MIX:]
    fill("prologue")
    q = q * lax.rsqrt(_head_sums(q, ones_bd) + NORM_EPS) * (DN_HD ** -0.5)
    k = k * lax.rsqrt(_head_sums(k, ones_bd) + NORM_EPS)

    beta = jax.nn.sigmoid(beta_logit)
    g = -jnp.exp(alog) * _softplus(decay_logit + dtb)
    gc = _chunk_cumsum(g)
    egc = jnp.exp(gc)
    g_last = jnp.concatenate(
        [jnp.broadcast_to(gc[(ci + 1) * c - 1:(ci + 1) * c, :], (c, MIX)) for ci in range(n_chunk)],
        axis=0)
    kb = k * beta
    vb = v * beta
    kbe = kb * egc
    qd = q * egc
    kd = k * jnp.exp(g_last - gc)
    c_dec = [jnp.exp(g_last[ci * c:ci * c + 1, :]) for ci in range(n_chunk)]

    lane = _iota((1, LANES), 1)
    g4 = gc[:, hs[DN_HEADS - 1]]
    for h in reversed(range(DN_HEADS - 1)):
        g4 = jnp.where(lane == h, gc[:, hs[h]], g4)
    g4_t = g4.T
    kd_t = [kd[:, hs[h]].T.astype(BF16) for h in heads]
    fill("norms")

    ri = _iota((t, t), 0)
    ci_ = _iota((t, t), 1)
    same = (ri // c) == (ci_ // c)
    incl = jnp.logical_and(same, ri >= ci_)
    strict = jnp.logical_and(same, ri > ci_)
    neg_inf = jnp.float32(-jnp.inf)

    aq = [_dot_nt(jnp.concatenate([kb[:, hs[h]], q[:, hs[h]]], axis=0), k[:, hs[h]])
          for h in heads]
    lower, attn, sol = [], [], []
    for h in heads:
        g_col = gc[:, hs[h]]
        diff = jnp.concatenate([g_col] * (t // DN_HD), axis=1) - g4_t[h:h + 1, :]
        decay = jnp.exp(jnp.where(incl, diff, neg_inf))
        lower.append(jnp.where(strict, aq[h][:t] * decay, 0.0).astype(BF16))
        attn.append((aq[h][t:] * decay).astype(BF16))
        sol.append(jnp.concatenate([vb[:, hs[h]], kbe[:, hs[h]]], axis=1))
    fill("decay")

    sol = [sol[h] - _dot(lower[h], sol[h]) for h in heads]
    power = [_dot(lower[h], lower[h]).astype(BF16) for h in heads]
    n_sq = 2
    while True:
        sol = [sol[h] + _dot(power[h], sol[h]) for h in heads]
        if 2 * n_sq >= c:
            break
        power = [_dot(power[h], power[h]).astype(BF16) for h in heads]
        n_sq *= 2

    row_chunk = _iota((2 * c, 1), 0) // c
    gn = []
    for h in heads:
        per_chunk = []
        for ci in range(n_chunk):
            pair = slice((ci // 2) * 2 * c, (ci // 2 + 1) * 2 * c)
            rhs = jnp.where(row_chunk == (ci % 2), sol[h][pair], 0.0)
            per_chunk.append(_dot(kd_t[h][:, pair], rhs))
        gn.append(per_chunk)

    def scan_and_outputs():
        state = [state_scr[h] for h in heads]
        starts = [[] for _ in heads]
        for ci in range(n_chunk):
            for h in heads:
                starts[h].append(state[h])
            fill(("scan", ci))
            upd = [_dot(gn[h][ci][:, DN_HD:], state[h]) for h in heads]
            state = [state[h] * c_dec[ci][:, hs[h]]
                     + (gn[h][ci][:, :DN_HD] - upd[h]) for h in heads]

        fill("outputs")
        xs = [[_dot(jnp.concatenate([qd[ci * c:(ci + 1) * c, hs[h]],
                                     sol[h][ci * c:(ci + 1) * c, DN_HD:]], axis=0), starts[h][ci])
               for ci in range(n_chunk)] for h in heads]
        outs = []
        for h in heads:
            xq = jnp.concatenate([xs[h][ci][:c] for ci in range(n_chunk)], axis=0)
            xw = jnp.concatenate([xs[h][ci][c:] for ci in range(n_chunk)], axis=0)
            outs.append(xq + _dot(attn[h], sol[h][:, :DN_HD] - xw))
        o = jnp.concatenate(outs, axis=1)
        o = (o * lax.rsqrt(_head_sums(o, ones_bd) * (1.0 / DN_HD) + NORM_EPS) * dnorm
             * _silu(get_z()))
        for h in heads:
            state_scr[h] = state[h]
        return o

    return scan_and_outputs


def _mixer_kernel(x_ref, xn_ref, cos_ref, sin_ref, ones_ref, anorm_ref, wfront_ref, wrep_ref,
                  wtail_ref, lng_ref, lnb_ref, sguw_ref, sgub_ref, sinks_ref, convw_ref, alog_ref,
                  dtb_ref, dnorm_ref, wbr_ref, wout_ref, o_ref, kv_scr, conv_scr, state_scr,
                  merged_scr, gate_scr, wgate_scr, u_scr, vn_scr, probs_scr, h_scr):
    seq_start = pl.program_id(1) == 0

    @pl.when(jnp.logical_and(pl.program_id(0) == 0, seq_start))
    def _():
        for n in range(3):
            window = wtail_ref[:, n * D_MODEL:(n + 1) * D_MODEL + GATE_OFF].astype(F32)
            wgate_scr[:, n * D_MODEL:(n + 1) * D_MODEL] = window[:, GATE_OFF:].astype(BF16)
        h_scr[...] = _rmsnorm(x_ref[...], anorm_ref[...]).astype(BF16)

    @pl.when(seq_start)
    def _():
        kv_scr[...] = jnp.zeros_like(kv_scr)
        conv_scr[0:SUBLANES, :] = jnp.zeros((SUBLANES, 3 * MIX), F32)
        state_scr[...] = jnp.zeros_like(state_scr)

    x = x_ref[...]
    h = h_scr[...]

    def proj(w_ref, lo, width):
        return jnp.dot(h, w_ref[:, lo:lo + width], preferred_element_type=F32)

    def gate(n, lo=0, width=D_MODEL):
        return jax.nn.sigmoid(proj(wgate_scr, n * D_MODEL + lo, width))

    park = {}

    def fill(stage):
        if stage == "prologue":
            qkv_b = proj(wfront_ref, C_QKVB, MIX + 2 * SWA_KV * SWA_HD)
            park["swa"] = _sliding_window_attention(
                qkv_b[:, :MIX], qkv_b[:, MIX:MIX + LANES], qkv_b[:, MIX + LANES:],
                cos_ref[...], sin_ref[...], sinks_ref, ones_ref[0:LANES, 0:LANES], kv_scr,
                probs_scr, seq_start)
            park["uv"] = proj(wfront_ref, C_UV, 2 * MIX)
            gate_scr[0] = gate(0)
            next(park["swa"])
        elif stage == "norms":
            gate_scr[1] = gate(1)
        elif stage == "decay":
            park["z"] = proj(wfront_ref, C_Z, MIX)
            next(park["swa"])
            uv = park.pop("uv")
            _spatial_gating_prep(uv[:, :MIX], uv[:, MIX:], lng_ref[...], lnb_ref[...],
                                 u_scr, vn_scr)
        elif stage[0] == "scan":
            lo = stage[1] * 2 * LANES
            gate_scr[2, :, lo:lo + 2 * LANES] = gate(2, lo, 2 * LANES)

    dn_tail = _gated_deltanet(proj(wfront_ref, C_QKVC, 3 * MIX), lambda: park.pop("z"),
                              proj(wrep_ref, 0, MIX), proj(wrep_ref, MIX, MIX), convw_ref,
                              alog_ref[...], dtb_ref[...], dnorm_ref[...], ones_ref[...],
                              conv_scr, state_scr, fill)

    out_c = dn_tail()
    out_a = _spatial_gating_mix(u_scr, vn_scr, sguw_ref, sgub_ref)
    out_b = next(park.pop("swa"))
    merged = (gate_scr[0] * _dot(out_a, wbr_ref[0]) + gate_scr[1] * _dot(out_b, wbr_ref[1])
              + gate_scr[2] * _dot(out_c, wbr_ref[2]))
    o_ref[...] = x + _dot(merged, wout_ref[...])
    h_scr[...] = _rmsnorm(xn_ref[...], anorm_ref[...]).astype(BF16)


def _ffn_kernel(x_ref, fnorm_ref, wgu_ref, wd_ref, final_ref, o_ref, *, apply_final_norm):
    x = x_ref[...]
    h = _rmsnorm(x, fnorm_ref[...]).astype(BF16)
    gu = jnp.dot(h, wgu_ref[...], preferred_element_type=F32)
    act = _silu(gu[:, :D_FF]) * gu[:, D_FF:]
    y = x + _dot(act, wd_ref[...])
    if apply_final_norm:
        y = _rmsnorm(y, final_ref[...])
    o_ref[...] = y


def _resident(shape_tail, layer):
    n = len(shape_tail)
    return pl.BlockSpec((None,) + tuple(shape_tail), lambda *_: (layer,) + (0,) * n,
                        pipeline_mode=pl.Buffered(1))


def _mixer_call(x, rope, params, layer):
    batch, seq, _ = x.shape
    t = MIXER_TILE
    n_t = seq // t
    tok = lambda w: pl.BlockSpec((None, t, w), lambda b, i: (b, i, 0))

    def next_tile(b, i):
        return (jnp.minimum(b + (i + 1) // n_t, batch - 1), (i + 1) % n_t, 0)

    in_specs = [
        tok(D_MODEL), pl.BlockSpec((None, t, D_MODEL), next_tile), tok(LANES), tok(LANES),
        pl.BlockSpec((MXU_DIM, MXU_DIM), lambda b, i: (0, 0), pipeline_mode=pl.Buffered(1)),
        _resident((1, D_MODEL), layer),
        _resident((D_MODEL, C_FRONT_END), layer),
        _resident((D_MODEL, 2 * MIX), layer),
        _resident((D_MODEL, GATE_OFF + 3 * D_MODEL), layer),
        _resident((1, MIX), layer), _resident((1, MIX), layer),
        _resident((SGU_GROUPS, BLOCK, BLOCK), layer), _resident((BLOCK, SGU_GROUPS), layer),
        _resident((SWA_HEADS, LANES), layer),
        _resident((DN_CONV, 3 * MIX), layer),
        _resident((1, MIX), layer), _resident((1, MIX), layer), _resident((1, MIX), layer),
        _resident((3, MIX, D_MODEL), layer), _resident((D_MODEL, D_MODEL), layer),
    ]
    return pl.pallas_call(
        _mixer_kernel,
        out_shape=jax.ShapeDtypeStruct(x.shape, F32),
        grid=(batch, seq // t),
        in_specs=in_specs,
        out_specs=tok(D_MODEL),
        scratch_shapes=[
            pltpu.VMEM((2, BLOCK, LANES), F32),
            pltpu.VMEM((SUBLANES, 3 * MIX), F32),
            pltpu.VMEM((DN_HEADS, DN_HD, DN_HD), F32),
            pltpu.VMEM((t, D_MODEL), F32),
            pltpu.VMEM((3, t, D_MODEL), F32),
            pltpu.VMEM((D_MODEL, 3 * D_MODEL), BF16),
            pltpu.VMEM((t, MIX), F32),
            pltpu.VMEM((t, MIX), BF16),
            pltpu.VMEM((t // BLOCK * SWA_KV, SWA_HEADS // SWA_KV * BLOCK, 2 * BLOCK), BF16),
            pltpu.VMEM((t, D_MODEL), BF16),
        ],
        compiler_params=pltpu.CompilerParams(
            dimension_semantics=("arbitrary", "arbitrary"),
            vmem_limit_bytes=VMEM_LIMIT_BYTES),
        name=f"mixer_l{layer}",
    )(x, x, *rope, params["head_ones"], params["attn_norm"], params["w_in"], params["w_rep"],
      params["w_tail"], params["sgu_ln_g"], params["sgu_ln_b"],
      params["sgu_w"], params["sgu_b"], params["attn_sinks"], params["dn_conv_w"],
      params["dn_a_log"], params["dn_dt_bias"], params["dn_norm"], params["w_branch"],
      params["w_out"])


def _ffn_call(x, params, layer, apply_final_norm):
    batch, seq, _ = x.shape
    t = FFN_TILE
    tok = pl.BlockSpec((None, t, D_MODEL), lambda b, i: (b, i, 0))
    return pl.pallas_call(
        functools.partial(_ffn_kernel, apply_final_norm=apply_final_norm),
        out_shape=jax.ShapeDtypeStruct(x.shape, F32),
        grid=(batch, seq // t),
        in_specs=[
            tok,
            _resident((1, D_MODEL), layer),
            _resident((D_MODEL, 2 * D_FF), layer),
            _resident((D_FF, D_MODEL), layer),
            pl.BlockSpec((1, D_MODEL), lambda b, i: (0, 0)),
        ],
        out_specs=tok,
        compiler_params=pltpu.CompilerParams(
            dimension_semantics=("arbitrary", "arbitrary"),
            vmem_limit_bytes=VMEM_LIMIT_BYTES),
        name=f"ffn_l{layer}",
    )(x, params["ffn_norm"], params["w_gate_up"], params["w_down"], params["final_norm"])


def _rope_tables(positions):
    half = ROPE_DIM // 2
    inv_freq = ROPE_THETA ** (-jnp.arange(0, ROPE_DIM, 2, dtype=F32) / ROPE_DIM)
    ang = positions.astype(F32)[..., None] * inv_freq
    cos, sin = lax.optimization_barrier((jnp.cos(ang), jnp.sin(ang)))
    rotary = (jnp.arange(LANES) % SWA_HD) < ROPE_DIM
    reps = (1,) * (ang.ndim - 1) + (LANES // half,)
    return jnp.where(rotary, jnp.tile(cos, reps), 1.0), jnp.where(rotary, jnp.tile(sin, reps), 0.0)


def _pack_params(attn_norm, w_in, sgu_ln_g, sgu_ln_b, sgu_w, sgu_b, attn_sinks, dn_conv_w,
                 dn_a_log, dn_dt_bias, dn_norm, w_branch, w_out, ffn_norm, w_gate_up, w_down,
                 final_norm):
    depth = w_in.shape[0]
    c_gate = C_FRONT_END + GATE_OFF
    per_head = lambda a: jnp.repeat(a, DN_HD, axis=-1)
    lane_head = jnp.arange(MXU_DIM) // DN_HD
    w_in16 = w_in.astype(BF16)
    return {
        "head_ones": (lane_head[:, None] == lane_head[None, :]).astype(BF16),
        "attn_norm": attn_norm[:, None, :],
        "w_in": w_in16,
        "w_rep": per_head(w_in16[:, :, C_FRONT_END:c_gate]),
        "w_tail": w_in16[:, :, C_FRONT_END:],
        "sgu_ln_g": sgu_ln_g[:, None, :],
        "sgu_ln_b": sgu_ln_b[:, None, :],
        "sgu_w": sgu_w,
        "sgu_b": jnp.swapaxes(sgu_b, 1, 2),
        "attn_sinks": jnp.broadcast_to(attn_sinks[:, :, None], (depth, SWA_HEADS, LANES)),
        "dn_conv_w": dn_conv_w,
        "dn_a_log": per_head(dn_a_log)[:, None, :],
        "dn_dt_bias": per_head(dn_dt_bias)[:, None, :],
        "dn_norm": jnp.tile(dn_norm, (1, DN_HEADS))[:, None, :],
        "w_branch": w_branch.astype(BF16),
        "w_out": w_out.astype(BF16),
        "ffn_norm": ffn_norm[:, None, :],
        "w_gate_up": w_gate_up.astype(BF16),
        "w_down": w_down.astype(BF16),
        "final_norm": final_norm[None, :],
    }


def kernel(x, positions, attn_norm, w_in, sgu_ln_g, sgu_ln_b, sgu_w, sgu_b, attn_sinks,
           dn_conv_w, dn_a_log, dn_dt_bias, dn_norm, w_branch, w_out, ffn_norm, w_gate_up,
           w_down, final_norm):
    assert MIXER_TILE == MXU_DIM == 4 * DN_CHUNK
    assert x.shape[1] % MIXER_TILE == 0 and x.shape[1] % FFN_TILE == 0
    assert x.shape[2] == D_MODEL
    assert w_in.shape[2] == C_FRONT_END + GATE_OFF + 3 * D_MODEL
    depth = w_in.shape[0]
    params = _pack_params(attn_norm, w_in, sgu_ln_g, sgu_ln_b, sgu_w, sgu_b, attn_sinks,
                          dn_conv_w, dn_a_log, dn_dt_bias, dn_norm, w_branch, w_out, ffn_norm,
                          w_gate_up, w_down, final_norm)
    rope = _rope_tables(positions)
    for layer in range(depth):
        x = _mixer_call(x, rope, params, layer)
        x = _ffn_call(x, params, layer, apply_final_norm=(layer == depth - 1))
    return x
```

```python
import functools

import jax
import jax.numpy as jnp
from jax import lax
from jax.experimental import pallas as pl
from jax.experimental.pallas import tpu as pltpu

D_MODEL = 1024
MIX = 512
NORM_EPS = 1e-6

SGU_GROUPS = 4
BLOCK = 128

SWA_HEADS = 8
SWA_KV = 2
SWA_HD = 64
ROPE_THETA = 500000.0
ROPE_DIM = 16

DN_HEADS = 4
DN_HD = 128
DN_CONV = 4
DN_CHUNK = 64

D_FF = 2816

LANES = 128
SUBLANES = 8
MXU_DIM = 256

C_UV = 0
C_QKVB = C_UV + 2 * MIX
C_QKVC = C_QKVB + MIX + 2 * SWA_KV * SWA_HD
C_Z = C_QKVC + 3 * MIX
C_FRONT_END = C_Z + MIX
GATE_OFF = 2 * DN_HEADS

MIXER_TILE = 256
FFN_TILE = 512
W_STAGE_ROWS = 64
VMEM_LIMIT_BYTES = 56 * 1024 * 1024

BF16 = jnp.bfloat16
F32 = jnp.float32
_GELU_C = 0.7978845608028654


def _dot(a, b):
    return jnp.dot(a.astype(BF16), b.astype(BF16), preferred_element_type=F32)


def _dot_nt(a, b):
    return lax.dot_general(a.astype(BF16), b.astype(BF16), (((1,), (1,)), ((), ())),
                           preferred_element_type=F32)


def _rmsnorm(x, g):
    return x * lax.rsqrt(jnp.mean(x * x, axis=-1, keepdims=True) + NORM_EPS) * g


def _gelu_tanh(x):
    inner = x * (_GELU_C + (_GELU_C * 0.044715) * (x * x))
    return (0.5 * x) * (1.0 + jnp.tanh(inner))


def _silu(x):
    return x * jax.nn.sigmoid(x)


def _softplus(x):
    return jnp.maximum(x, 0.0) + jnp.log(1.0 + jnp.exp(-jnp.abs(x)))


def _iota(shape, dim):
    return lax.broadcasted_iota(jnp.int32, shape, dim)


def _spatial_gating_prep(u_pre, v_pre, ln_g, ln_b, u_scr, vn_scr):
    u_scr[...] = _gelu_tanh(u_pre)
    v = _gelu_tanh(v_pre)
    vc = v - jnp.mean(v, axis=-1, keepdims=True)
    vn_scr[...] = (vc * lax.rsqrt(jnp.mean(vc * vc, axis=-1, keepdims=True) + NORM_EPS) * ln_g
                   + ln_b).astype(BF16)


def _spatial_gating_mix(u_scr, vn_scr, sguw_ref, sgub_ref):
    u = u_scr[...]
    vn = vn_scr[...]
    t = u.shape[0]
    causal = _iota((BLOCK, BLOCK), 0) >= _iota((BLOCK, BLOCK), 1)
    groups = []
    for g in range(SGU_GROUPS):
        w = jnp.where(causal, sguw_ref[g], 0.0).astype(BF16)
        bias = sgub_ref[:, g:g + 1]
        cols = slice(g * LANES, (g + 1) * LANES)
        mixed = [_dot(w, vn[blk * BLOCK:(blk + 1) * BLOCK, cols]) + bias
                 for blk in range(t // BLOCK)]
        groups.append(u[:, cols] * jnp.concatenate(mixed, axis=0))
    return jnp.concatenate(groups, axis=1)


def _rope(x, cos, sin):
    half = ROPE_DIM // 2
    first_half = (_iota((1, LANES), 1) % SWA_HD) < half
    sin_a = jnp.where(first_half, -sin, 0.0)
    sin_b = jnp.where(first_half, 0.0, sin)
    n = x.shape[1] // LANES
    if n > 1:
        cos = jnp.concatenate([cos] * n, axis=1)
        sin_a = jnp.concatenate([sin_a] * n, axis=1)
        sin_b = jnp.concatenate([sin_b] * n, axis=1)
    width = x.shape[1]
    return (x * cos + pltpu.roll(x, width - half, 1) * sin_a + pltpu.roll(x, half, 1) * sin_b)


def _sliding_window_attention(q, k, v, cos, sin, sinks_ref, ones, kv_scr, probs_scr, seq_start):
    t = q.shape[0]
    n_blk = t // BLOCK
    group = SWA_HEADS // SWA_KV
    q = _rope(q, cos, sin) * (SWA_HD ** -0.5)
    k = _rope(k, cos, sin)

    kcat = jnp.concatenate([kv_scr[0], k], axis=0)
    vcat = jnp.concatenate([kv_scr[1], v], axis=0)

    lane = _iota((1, LANES), 1)
    lo = lane < SWA_HD
    krot = pltpu.roll(kcat, SWA_HD, 1)
    vrot = pltpu.roll(vcat, SWA_HD, 1)
    kdup = [jnp.where(lo, kcat, krot).astype(BF16), jnp.where(lo, krot, kcat).astype(BF16)]
    vlo = [jnp.where(lo, vcat, 0.0).astype(BF16), jnp.where(lo, vrot, 0.0).astype(BF16)]
    vhi = [jnp.where(lo, 0.0, vrot).astype(BF16), jnp.where(lo, 0.0, vcat).astype(BF16)]

    rows4 = group * BLOCK
    cur_mask = (_iota((rows4, BLOCK), 0) % BLOCK) >= _iota((rows4, BLOCK), 1)
    neg_inf = jnp.float32(-jnp.inf)
    units = [(blk, kh) for blk in range(n_blk) for kh in range(SWA_KV)]

    lhs = []
    for blk, kh in units:
        rows = slice(blk * BLOCK, (blk + 1) * BLOCK)
        s0 = q[rows, (2 * kh) * LANES:(2 * kh + 1) * LANES]
        s1 = q[rows, (2 * kh + 1) * LANES:(2 * kh + 2) * LANES]
        lhs.append(jnp.concatenate([jnp.where(lo, s0, 0.0), jnp.where(lo, s1, 0.0),
                                    jnp.where(lo, 0.0, s0), jnp.where(lo, 0.0, s1)],
                                   axis=0).astype(BF16))
    logits = [_dot_nt(lhs[i], kdup[kh][blk * BLOCK:(blk + 2) * BLOCK])
              for i, (blk, kh) in enumerate(units)]
    yield
    n_unit = len(units)
    sinks = [jnp.concatenate(
        [jnp.broadcast_to(sinks_ref[group * kh + 2 * j + half:group * kh + 2 * j + half + 1, :],
                          (BLOCK, LANES)) for half in range(2) for j in range(2)], axis=0)
        for kh in range(SWA_KV)]
    act = []
    for i, (blk, kh) in enumerate(units):
        l_prev = logits[i][:, :BLOCK]
        if blk == 0:
            l_prev = jnp.where(seq_start, neg_inf, l_prev)
        act.append(jnp.where(cur_mask, logits[i][:, BLOCK:], l_prev))
    m = [jnp.maximum(jnp.broadcast_to(jnp.max(act[i], axis=1, keepdims=True), act[i].shape),
                     sinks[units[i][1]]) for i in range(n_unit)]
    p = [jnp.exp(act[i] - m[i]) for i in range(n_unit)]
    den = [_dot(p[i], ones) + jnp.exp(sinks[units[i][1]] - m[i]) for i in range(n_unit)]
    for i in range(n_unit):
        pn = p[i] * (1.0 / den[i])
        probs_scr[i] = jnp.concatenate([jnp.where(cur_mask, 0.0, pn), jnp.where(cur_mask, pn, 0.0)],
                                       axis=1).astype(BF16)
    yield
    probs = [probs_scr[i] for i in range(len(units))]
    out_rows = []
    for blk in range(n_blk):
        slabs = []
        for kh in range(SWA_KV):
            i = blk * SWA_KV + kh
            keys = slice(blk * BLOCK, (blk + 2) * BLOCK)
            res = (_dot(probs[i][:2 * BLOCK], vlo[kh][keys])
                   + _dot(probs[i][2 * BLOCK:], vhi[kh][keys]))
            slabs += [res[:BLOCK], res[BLOCK:]]
        out_rows.append(jnp.concatenate(slabs, axis=1))
    kv_scr[0] = k[t - BLOCK:, :]
    kv_scr[1] = v[t - BLOCK:, :]
    yield jnp.concatenate(out_rows, axis=0)


def _chunk_cumsum(x):
    row = _iota(x.shape, 0) % DN_CHUNK
    shift = 1
    while shift < DN_CHUNK:
        x = x + jnp.where(row >= shift, pltpu.roll(x, shift, 0), 0.0)
        shift *= 2
    return x


def _head_sums(x, ones_bd):
    xx = x * x
    return jnp.concatenate([_dot(xx[:, i * MXU_DIM:(i + 1) * MXU_DIM], ones_bd)
                            for i in range(x.shape[1] // MXU_DIM)], axis=1)


def _gated_deltanet(qkv, get_z, beta_logit, decay_logit, convw_ref, alog, dtb, dnorm, ones_bd,
                    conv_scr, state_scr, fill):
    t = qkv.shape[0]
    c = DN_CHUNK
    n_chunk = t // c
    pad = SUBLANES
    heads = range(DN_HEADS)
    hs = [slice(h * DN_HD, (h + 1) * DN_HD) for h in heads]

    xa = jnp.concatenate([conv_scr[0:pad, :], qkv], axis=0)
    y = qkv * convw_ref[DN_CONV - 1:DN_CONV, :]
    for i in range(DN_CONV - 1):
        y = y + pltpu.roll(xa, DN_CONV - 1 - i, 0)[pad:, :] * convw_ref[i:i + 1, :]
    conv_scr[0:pad, :] = qkv[t - pad:, :]
    y = _silu(y)

    q = y[:, :MIX]
    k = y[:, MIX:2 * MIX]
    v = y[:, 2 * MIX:]
    fill("prologue")
    q = q * lax.rsqrt(_head_sums(q, ones_bd) + NORM_EPS) * (DN_HD ** -0.5)
    k = k * lax.rsqrt(_head_sums(k, ones_bd) + NORM_EPS)

    beta = jax.nn.sigmoid(beta_logit)
    g = -jnp.exp(alog) * _softplus(decay_logit + dtb)
    gc = _chunk_cumsum(g)
    egc = jnp.exp(gc)
    g_last = jnp.concatenate(
        [jnp.broadcast_to(gc[(ci + 1) * c - 1:(ci + 1) * c, :], (c, MIX)) for ci in range(n_chunk)],
        axis=0)
    kb = k * beta
    vb = v * beta
    kbe = kb * egc
    qd = q * egc
    kd = k * jnp.exp(g_last - gc)
    c_dec = [jnp.exp(g_last[ci * c:ci * c + 1, :]) for ci in range(n_chunk)]

    lane = _iota((1, LANES), 1)
    g4 = gc[:, hs[DN_HEADS - 1]]
    for h in reversed(range(DN_HEADS - 1)):
        g4 = jnp.where(lane == h, gc[:, hs[h]], g4)
    g4_t = g4.T
    kd_t = [kd[:, hs[h]].T.astype(BF16) for h in heads]
    fill("norms")

    ri = _iota((t, t), 0)
    ci_ = _iota((t, t), 1)
    same = (ri // c) == (ci_ // c)
    incl = jnp.logical_and(same, ri >= ci_)
    strict = jnp.logical_and(same, ri > ci_)
    neg_inf = jnp.float32(-jnp.inf)

    aq = [_dot_nt(jnp.concatenate([kb[:, hs[h]], q[:, hs[h]]], axis=0), k[:, hs[h]])
          for h in heads]
    lower, attn, sol = [], [], []
    for h in heads:
        g_col = gc[:, hs[h]]
        diff = jnp.concatenate([g_col] * (t // DN_HD), axis=1) - g4_t[h:h + 1, :]
        decay = jnp.exp(jnp.where(incl, diff, neg_inf))
        lower.append(jnp.where(strict, aq[h][:t] * decay, 0.0).astype(BF16))
        attn.append((aq[h][t:] * decay).astype(BF16))
        sol.append(jnp.concatenate([vb[:, hs[h]], kbe[:, hs[h]]], axis=1))
    fill("decay")

    sol = [sol[h] - _dot(lower[h], sol[h]) for h in heads]
    power = [_dot(lower[h], lower[h]).astype(BF16) for h in heads]
    n_sq = 2
    while True:
        sol = [sol[h] + _dot(power[h], sol[h]) for h in heads]
        if 2 * n_sq >= c:
            break
        power = [_dot(power[h], power[h]).astype(BF16) for h in heads]
        n_sq *= 2

    row_chunk = _iota((2 * c, 1), 0) // c
    gn = []
    for h in heads:
        per_chunk = []
        for ci in range(n_chunk):
            pair = slice((ci // 2) * 2 * c, (ci // 2 + 1) * 2 * c)
            rhs = jnp.where(row_chunk == (ci % 2), sol[h][pair], 0.0)
            per_chunk.append(_dot(kd_t[h][:, pair], rhs))
        gn.append(per_chunk)

    def scan_and_outputs():
        state = [state_scr[h] for h in heads]
        starts = [[] for _ in heads]
        for ci in range(n_chunk):
            for h in heads:
                starts[h].append(state[h])
            fill(("scan", ci))
            upd = [_dot(gn[h][ci][:, DN_HD:], state[h]) for h in heads]
            state = [state[h] * c_dec[ci][:, hs[h]]
                     + (gn[h][ci][:, :DN_HD] - upd[h]) for h in heads]

        fill("outputs")
        xs = [[_dot(jnp.concatenate([qd[ci * c:(ci + 1) * c, hs[h]],
                                     sol[h][ci * c:(ci + 1) * c, DN_HD:]], axis=0), starts[h][ci])
               for ci in range(n_chunk)] for h in heads]
        outs = []
        for h in heads:
            xq = jnp.concatenate([xs[h][ci][:c] for ci in range(n_chunk)], axis=0)
            xw = jnp.concatenate([xs[h][ci][c:] for ci in range(n_chunk)], axis=0)
            outs.append(xq + _dot(attn[h], sol[h][:, :DN_HD] - xw))
        o = jnp.concatenate(outs, axis=1)
        o = (o * lax.rsqrt(_head_sums(o, ones_bd) * (1.0 / DN_HD) + NORM_EPS) * dnorm
             * _silu(get_z()))
        for h in heads:
            state_scr[h] = state[h]
        return o

    return scan_and_outputs


def _load_w_in(win_hbm, layer, stage, sem, wfront_scr, wrep_scr, wgate_scr):
    rows = W_STAGE_ROWS
    n_chunks = D_MODEL // rows

    def chunk_copy(c, slot):
        return pltpu.make_async_copy(win_hbm.at[layer, pl.ds(c * rows, rows), :],
                                     stage.at[slot], sem.at[slot])

    chunk_copy(0, 0).start()

    def body(c, carry):
        slot = c % 2

        @pl.when(c + 1 < n_chunks)
        def _():
            chunk_copy(c + 1, 1 - slot).start()

        chunk_copy(c, slot).wait()
        buf = stage.at[slot]
        dst = pl.ds(pl.multiple_of(c * rows, rows), rows)
        wfront_scr[dst, :] = buf[:, 0:C_FRONT_END].astype(BF16)
        logits = buf[:, C_FRONT_END:C_FRONT_END + LANES]
        wrep_scr[dst, :] = jnp.concatenate(
            [jnp.broadcast_to(logits[:, j:j + 1], (rows, DN_HD)) for j in range(GATE_OFF)],
            axis=1).astype(BF16)
        for n in range(3):
            lo = C_FRONT_END + n * D_MODEL
            window = buf[:, lo:lo + D_MODEL + GATE_OFF]
            wgate_scr[dst, n * D_MODEL:(n + 1) * D_MODEL] = window[:, GATE_OFF:].astype(BF16)
        return carry

    lax.fori_loop(0, n_chunks, body, 0)


def _mixer_kernel(x_ref, cos_ref, sin_ref, ones_ref, anorm_ref, win_hbm, lng_ref, lnb_ref,
                  sguw_ref, sgub_ref, sinks_ref, convw_ref, alog_ref, dtb_ref, dnorm_ref, wbr_ref,
                  wout_ref, o_ref, kv_scr, conv_scr, state_scr, merged_scr, gate_scr, wgate_scr,
                  u_scr, vn_scr, probs_scr, wfront_ref, wrep_ref, stage_scr, stage_sem, *, layer):
    seq_start = pl.program_id(1) == 0

    @pl.when(jnp.logical_and(pl.program_id(0) == 0, seq_start))
    def _():
        _load_w_in(win_hbm, layer, stage_scr, stage_sem, wfront_ref, wrep_ref, wgate_scr)

    @pl.when(seq_start)
    def _():
        kv_scr[...] = jnp.zeros_like(kv_scr)
        conv_scr[0:SUBLANES, :] = jnp.zeros((SUBLANES, 3 * MIX), F32)
        state_scr[...] = jnp.zeros_like(state_scr)

    x = x_ref[...]
    h = _rmsnorm(x, anorm_ref[...]).astype(BF16)

    def proj(w_ref, lo, width):
        return jnp.dot(h, w_ref[:, lo:lo + width], preferred_element_type=F32)

    def gate(n, lo=0, width=D_MODEL):
        return jax.nn.sigmoid(proj(wgate_scr, n * D_MODEL + lo, width))

    park = {}

    def fill(stage):
        if stage == "prologue":
            qkv_b = proj(wfront_ref, C_QKVB, MIX + 2 * SWA_KV * SWA_HD)
            park["swa"] = _sliding_window_attention(
                qkv_b[:, :MIX], qkv_b[:, MIX:MIX + LANES], qkv_b[:, MIX + LANES:],
                cos_ref[...], sin_ref[...], sinks_ref, ones_ref[0:LANES, 0:LANES], kv_scr,
                probs_scr, seq_start)
            park["uv"] = proj(wfront_ref, C_UV, 2 * MIX)
            gate_scr[0] = gate(0)
            next(park["swa"])
        elif stage == "norms":
            gate_scr[1] = gate(1)
        elif stage == "decay":
            park["z"] = proj(wfront_ref, C_Z, MIX)
            next(park["swa"])
            uv = park.pop("uv")
            _spatial_gating_prep(uv[:, :MIX], uv[:, MIX:], lng_ref[...], lnb_ref[...],
                                 u_scr, vn_scr)
        elif stage[0] == "scan":
            lo = stage[1] * 2 * LANES
            gate_scr[2, :, lo:lo + 2 * LANES] = gate(2, lo, 2 * LANES)

    dn_tail = _gated_deltanet(proj(wfront_ref, C_QKVC, 3 * MIX), lambda: park.pop("z"),
                              proj(wrep_ref, 0, MIX), proj(wrep_ref, MIX, MIX), convw_ref,
                              alog_ref[...], dtb_ref[...], dnorm_ref[...], ones_ref[...],
                              conv_scr, state_scr, fill)

    out_c = dn_tail()
    out_a = _spatial_gating_mix(u_scr, vn_scr, sguw_ref, sgub_ref)
    out_b = next(park.pop("swa"))
    merged = (gate_scr[0] * _dot(out_a, wbr_ref[0]) + gate_scr[1] * _dot(out_b, wbr_ref[1])
              + gate_scr[2] * _dot(out_c, wbr_ref[2]))
    o_ref[...] = x + _dot(merged, wout_ref[...])


def _ffn_kernel(x_ref, fnorm_ref, wgu_ref, wd_ref, final_ref, o_ref, *, apply_final_norm):
    x = x_ref[...]
    h = _rmsnorm(x, fnorm_ref[...]).astype(BF16)
    gu = jnp.dot(h, wgu_ref[...], preferred_element_type=F32)
    act = _silu(gu[:, :D_FF]) * gu[:, D_FF:]
    y = x + _dot(act, wd_ref[...])
    if apply_final_norm:
        y = _rmsnorm(y, final_ref[...])
    o_ref[...] = y


def _resident(shape_tail, layer):
    n = len(shape_tail)
    return pl.BlockSpec((None,) + tuple(shape_tail), lambda *_: (layer,) + (0,) * n,
                        pipeline_mode=pl.Buffered(1))


def _mixer_call(x, rope, params, layer):
    batch, seq, _ = x.shape
    t = MIXER_TILE
    tok = lambda w: pl.BlockSpec((None, t, w), lambda b, i: (b, i, 0))
    in_specs = [
        tok(D_MODEL), tok(LANES), tok(LANES),
        pl.BlockSpec((MXU_DIM, MXU_DIM), lambda b, i: (0, 0), pipeline_mode=pl.Buffered(1)),
        _resident((1, D_MODEL), layer),
        pl.BlockSpec(memory_space=pl.ANY),
        _resident((1, MIX), layer), _resident((1, MIX), layer),
        _resident((SGU_GROUPS, BLOCK, BLOCK), layer), _resident((BLOCK, SGU_GROUPS), layer),
        _resident((SWA_HEADS, LANES), layer),
        _resident((DN_CONV, 3 * MIX), layer),
        _resident((1, MIX), layer), _resident((1, MIX), layer), _resident((1, MIX), layer),
        _resident((3, MIX, D_MODEL), layer), _resident((D_MODEL, D_MODEL), layer),
    ]
    return pl.pallas_call(
        functools.partial(_mixer_kernel, layer=layer),
        out_shape=jax.ShapeDtypeStruct(x.shape, F32),
        grid=(batch, seq // t),
        in_specs=in_specs,
        out_specs=tok(D_MODEL),
        scratch_shapes=[
            pltpu.VMEM((2, BLOCK, LANES), F32),
            pltpu.VMEM((SUBLANES, 3 * MIX), F32),
            pltpu.VMEM((DN_HEADS, DN_HD, DN_HD), F32),
            pltpu.VMEM((t, D_MODEL), F32),
            pltpu.VMEM((3, t, D_MODEL), F32),
            pltpu.VMEM((D_MODEL, 3 * D_MODEL), BF16),
            pltpu.VMEM((t, MIX), F32),
            pltpu.VMEM((t, MIX), BF16),
            pltpu.VMEM((t // BLOCK * SWA_KV, SWA_HEADS // SWA_KV * BLOCK, 2 * BLOCK), BF16),
            pltpu.VMEM((D_MODEL, C_FRONT_END), BF16),
            pltpu.VMEM((D_MODEL, 2 * MIX), BF16),
            pltpu.VMEM((2, W_STAGE_ROWS, C_FRONT_END + GATE_OFF + 3 * D_MODEL), F32),
            pltpu.SemaphoreType.DMA((2,)),
        ],
        compiler_params=pltpu.CompilerParams(
            dimension_semantics=("arbitrary", "arbitrary"),
            vmem_limit_bytes=VMEM_LIMIT_BYTES),
        name=f"mixer_l{layer}",
    )(x, *rope, params["head_ones"], params["attn_norm"], params["w_in"],
      params["sgu_ln_g"], params["sgu_ln_b"],
      params["sgu_w"], params["sgu_b"], params["attn_sinks"], params["dn_conv_w"],
      params["dn_a_log"], params["dn_dt_bias"], params["dn_norm"], params["w_branch"],
      params["w_out"])


def _ffn_call(x, params, layer, apply_final_norm):
    batch, seq, _ = x.shape
    t = FFN_TILE
    tok = pl.BlockSpec((None, t, D_MODEL), lambda b, i: (b, i, 0))
    return pl.pallas_call(
        functools.partial(_ffn_kernel, apply_final_norm=apply_final_norm),
        out_shape=jax.ShapeDtypeStruct(x.shape, F32),
        grid=(batch, seq // t),
        in_specs=[
            tok,
            _resident((1, D_MODEL), layer),
            _resident((D_MODEL, 2 * D_FF), layer),
            _resident((D_FF, D_MODEL), layer),
            pl.BlockSpec((1, D_MODEL), lambda b, i: (0, 0)),
        ],
        out_specs=tok,
        compiler_params=pltpu.CompilerParams(
            dimension_semantics=("arbitrary", "arbitrary"),
            vmem_limit_bytes=VMEM_LIMIT_BYTES),
        name=f"ffn_l{layer}",
    )(x, params["ffn_norm"], params["w_gate_up"], params["w_down"], params["final_norm"])


def _rope_tables(positions):
    half = ROPE_DIM // 2
    inv_freq = ROPE_THETA ** (-jnp.arange(0, ROPE_DIM, 2, dtype=F32) / ROPE_DIM)
    ang = positions.astype(F32)[..., None] * inv_freq
    cos, sin = lax.optimization_barrier((jnp.cos(ang), jnp.sin(ang)))
    rotary = (jnp.arange(LANES) % SWA_HD) < ROPE_DIM
    reps = (1,) * (ang.ndim - 1) + (LANES // half,)
    return jnp.where(rotary, jnp.tile(cos, reps), 1.0), jnp.where(rotary, jnp.tile(sin, reps), 0.0)


def _pack_params(attn_norm, w_in, sgu_ln_g, sgu_ln_b, sgu_w, sgu_b, attn_sinks, dn_conv_w,
                 dn_a_log, dn_dt_bias, dn_norm, w_branch, w_out, ffn_norm, w_gate_up, w_down,
                 final_norm):
    depth = w_in.shape[0]
    per_head = lambda a: jnp.repeat(a, DN_HD, axis=-1)
    lane_head = jnp.arange(MXU_DIM) // DN_HD
    return {
        "head_ones": (lane_head[:, None] == lane_head[None, :]).astype(BF16),
        "attn_norm": attn_norm[:, None, :],
        "w_in": w_in,
        "sgu_ln_g": sgu_ln_g[:, None, :],
        "sgu_ln_b": sgu_ln_b[:, None, :],
        "sgu_w": sgu_w,
        "sgu_b": jnp.swapaxes(sgu_b, 1, 2),
        "attn_sinks": jnp.broadcast_to(attn_sinks[:, :, None], (depth, SWA_HEADS, LANES)),
        "dn_conv_w": dn_conv_w,
        "dn_a_log": per_head(dn_a_log)[:, None, :],
        "dn_dt_bias": per_head(dn_dt_bias)[:, None, :],
        "dn_norm": jnp.tile(dn_norm, (1, DN_HEADS))[:, None, :],
        "w_branch": w_branch.astype(BF16),
        "w_out": w_out.astype(BF16),
        "ffn_norm": ffn_norm[:, None, :],
        "w_gate_up": w_gate_up.astype(BF16),
        "w_down": w_down.astype(BF16),
        "final_norm": final_norm[None, :],
    }


def kernel(x, positions, attn_norm, w_in, sgu_ln_g, sgu_ln_b, sgu_w, sgu_b, attn_sinks,
           dn_conv_w, dn_a_log, dn_dt_bias, dn_norm, w_branch, w_out, ffn_norm, w_gate_up,
           w_down, final_norm):
    assert MIXER_TILE == MXU_DIM == 4 * DN_CHUNK
    assert x.shape[1] % MIXER_TILE == 0 and x.shape[1] % FFN_TILE == 0
    assert x.shape[2] == D_MODEL
    assert w_in.shape[2] == C_FRONT_END + GATE_OFF + 3 * D_MODEL
    depth = w_in.shape[0]
    params = _pack_params(attn_norm, w_in, sgu_ln_g, sgu_ln_b, sgu_w, sgu_b, attn_sinks,
                          dn_conv_w, dn_a_log, dn_dt_bias, dn_norm, w_branch, w_out, ffn_norm,
                          w_gate_up, w_down, final_norm)
    rope = _rope_tables(positions)
    for layer in range(depth):
        x = _mixer_call(x, rope, params, layer)
        x = _ffn_call(x, params, layer, apply_final_norm=(layer == depth - 1))
    return x
```

```python
import functools

import jax
import jax.numpy as jnp
from jax import lax
from jax.experimental import pallas as pl
from jax.experimental.pallas import tpu as pltpu

D_MODEL = 1024
MIX = 512
NORM_EPS = 1e-6

SGU_GROUPS = 4
BLOCK = 128

SWA_HEADS = 8
SWA_KV = 2
SWA_HD = 64
ROPE_THETA = 500000.0
ROPE_DIM = 16

DN_HEADS = 4
DN_HD = 128
DN_CONV = 4
DN_CHUNK = 64

D_FF = 2816

LANES = 128
SUBLANES = 8
MXU_DIM = 256

C_UV = 0
C_QKVB = C_UV + 2 * MIX
C_QKVC = C_QKVB + MIX + 2 * SWA_KV * SWA_HD
C_Z = C_QKVC + 3 * MIX
C_FRONT_END = C_Z + MIX
GATE_OFF = 2 * DN_HEADS

MIXER_TILE = 256
FFN_TILE = 512
VMEM_LIMIT_BYTES = 56 * 1024 * 1024

BF16 = jnp.bfloat16
F32 = jnp.float32
_GELU_C = 0.7978845608028654


def _dot(a, b):
    return jnp.dot(a.astype(BF16), b.astype(BF16), preferred_element_type=F32)


def _dot_nt(a, b):
    return lax.dot_general(a.astype(BF16), b.astype(BF16), (((1,), (1,)), ((), ())),
                           preferred_element_type=F32)


def _rmsnorm(x, g):
    return x * lax.rsqrt(jnp.mean(x * x, axis=-1, keepdims=True) + NORM_EPS) * g


def _gelu_tanh(x):
    inner = x * (_GELU_C + (_GELU_C * 0.044715) * (x * x))
    return (0.5 * x) * (1.0 + jnp.tanh(inner))


def _silu(x):
    return x * jax.nn.sigmoid(x)


def _softplus(x):
    return jnp.maximum(x, 0.0) + jnp.log(1.0 + jnp.exp(-jnp.abs(x)))


def _iota(shape, dim):
    return lax.broadcasted_iota(jnp.int32, shape, dim)


def _spatial_gating_prep(u_pre, v_pre, ln_g, ln_b, u_scr, vn_scr):
    u_scr[...] = _gelu_tanh(u_pre)
    v = _gelu_tanh(v_pre)
    vc = v - jnp.mean(v, axis=-1, keepdims=True)
    vn_scr[...] = (vc * lax.rsqrt(jnp.mean(vc * vc, axis=-1, keepdims=True) + NORM_EPS) * ln_g
                   + ln_b).astype(BF16)


def _spatial_gating_mix(u_scr, vn_scr, sguw_ref, sgub_ref):
    u = u_scr[...]
    vn = vn_scr[...]
    t = u.shape[0]
    causal = _iota((BLOCK, BLOCK), 0) >= _iota((BLOCK, BLOCK), 1)
    groups = []
    for g in range(SGU_GROUPS):
        w = jnp.where(causal, sguw_ref[g], 0.0).astype(BF16)
        bias = sgub_ref[:, g:g + 1]
        cols = slice(g * LANES, (g + 1) * LANES)
        mixed = [_dot(w, vn[blk * BLOCK:(blk + 1) * BLOCK, cols]) + bias
                 for blk in range(t // BLOCK)]
        groups.append(u[:, cols] * jnp.concatenate(mixed, axis=0))
    return jnp.concatenate(groups, axis=1)


def _rope(x, cos, sin):
    half = ROPE_DIM // 2
    first_half = (_iota((1, LANES), 1) % SWA_HD) < half
    sin_a = jnp.where(first_half, -sin, 0.0)
    sin_b = jnp.where(first_half, 0.0, sin)
    n = x.shape[1] // LANES
    if n > 1:
        cos = jnp.concatenate([cos] * n, axis=1)
        sin_a = jnp.concatenate([sin_a] * n, axis=1)
        sin_b = jnp.concatenate([sin_b] * n, axis=1)
    width = x.shape[1]
    return (x * cos + pltpu.roll(x, width - half, 1) * sin_a + pltpu.roll(x, half, 1) * sin_b)


def _sliding_window_attention(q, k, v, cos, sin, sinks_ref, ones, kv_scr, probs_scr, seq_start):
    t = q.shape[0]
    n_blk = t // BLOCK
    group = SWA_HEADS // SWA_KV
    q = _rope(q, cos, sin) * (SWA_HD ** -0.5)
    k = _rope(k, cos, sin)

    kcat = jnp.concatenate([kv_scr[0], k], axis=0)
    vcat = jnp.concatenate([kv_scr[1], v], axis=0)

    lane = _iota((1, LANES), 1)
    lo = lane < SWA_HD
    krot = pltpu.roll(kcat, SWA_HD, 1)
    vrot = pltpu.roll(vcat, SWA_HD, 1)
    kdup = [jnp.where(lo, kcat, krot).astype(BF16), jnp.where(lo, krot, kcat).astype(BF16)]
    vlo = [jnp.where(lo, vcat, 0.0).astype(BF16), jnp.where(lo, vrot, 0.0).astype(BF16)]
    vhi = [jnp.where(lo, 0.0, vrot).astype(BF16), jnp.where(lo, 0.0, vcat).astype(BF16)]

    rows4 = group * BLOCK
    cur_mask = (_iota((rows4, BLOCK), 0) % BLOCK) >= _iota((rows4, BLOCK), 1)
    neg_inf = jnp.float32(-jnp.inf)
    units = [(blk, kh) for blk in range(n_blk) for kh in range(SWA_KV)]

    lhs = []
    for blk, kh in units:
        rows = slice(blk * BLOCK, (blk + 1) * BLOCK)
        s0 = q[rows, (2 * kh) * LANES:(2 * kh + 1) * LANES]
        s1 = q[rows, (2 * kh + 1) * LANES:(2 * kh + 2) * LANES]
        lhs.append(jnp.concatenate([jnp.where(lo, s0, 0.0), jnp.where(lo, s1, 0.0),
                                    jnp.where(lo, 0.0, s0), jnp.where(lo, 0.0, s1)],
                                   axis=0).astype(BF16))
    logits = [_dot_nt(lhs[i], kdup[kh][blk * BLOCK:(blk + 2) * BLOCK])
              for i, (blk, kh) in enumerate(units)]
    yield
    n_unit = len(units)
    sinks = [jnp.concatenate(
        [jnp.broadcast_to(sinks_ref[group * kh + 2 * j + half:group * kh + 2 * j + half + 1, :],
                          (BLOCK, LANES)) for half in range(2) for j in range(2)], axis=0)
        for kh in range(SWA_KV)]
    act = []
    for i, (blk, kh) in enumerate(units):
        l_prev = logits[i][:, :BLOCK]
        if blk == 0:
            l_prev = jnp.where(seq_start, neg_inf, l_prev)
        act.append(jnp.where(cur_mask, logits[i][:, BLOCK:], l_prev))
    m = [jnp.maximum(jnp.broadcast_to(jnp.max(act[i], axis=1, keepdims=True), act[i].shape),
                     sinks[units[i][1]]) for i in range(n_unit)]
    p = [jnp.exp(act[i] - m[i]) for i in range(n_unit)]
    den = [_dot(p[i], ones) + jnp.exp(sinks[units[i][1]] - m[i]) for i in range(n_unit)]
    for i in range(n_unit):
        pn = p[i] * (1.0 / den[i])
        probs_scr[i] = jnp.concatenate([jnp.where(cur_mask, 0.0, pn), jnp.where(cur_mask, pn, 0.0)],
                                       axis=1).astype(BF16)
    yield
    probs = [probs_scr[i] for i in range(len(units))]
    out_rows = []
    for blk in range(n_blk):
        slabs = []
        for kh in range(SWA_KV):
            i = blk * SWA_KV + kh
            keys = slice(blk * BLOCK, (blk + 2) * BLOCK)
            res = (_dot(probs[i][:2 * BLOCK], vlo[kh][keys])
                   + _dot(probs[i][2 * BLOCK:], vhi[kh][keys]))
            slabs += [res[:BLOCK], res[BLOCK:]]
        out_rows.append(jnp.concatenate(slabs, axis=1))
    kv_scr[0] = k[t - BLOCK:, :]
    kv_scr[1] = v[t - BLOCK:, :]
    yield jnp.concatenate(out_rows, axis=0)


def _chunk_cumsum(x):
    row = _iota(x.shape, 0) % DN_CHUNK
    shift = 1
    while shift < DN_CHUNK:
        x = x + jnp.where(row >= shift, pltpu.roll(x, shift, 0), 0.0)
        shift *= 2
    return x


def _head_sums(x, ones_bd):
    xx = x * x
    return jnp.concatenate([_dot(xx[:, i * MXU_DIM:(i + 1) * MXU_DIM], ones_bd)
                            for i in range(x.shape[1] // MXU_DIM)], axis=1)


def _gated_deltanet(qkv, get_z, beta_logit, decay_logit, convw_ref, alog, dtb, dnorm, ones_bd,
                    conv_scr, state_scr, fill):
    t = qkv.shape[0]
    c = DN_CHUNK
    n_chunk = t // c
    pad = SUBLANES
    heads = range(DN_HEADS)
    hs = [slice(h * DN_HD, (h + 1) * DN_HD) for h in heads]

    xa = jnp.concatenate([conv_scr[0:pad, :], qkv], axis=0)
    y = qkv * convw_ref[DN_CONV - 1:DN_CONV, :]
    for i in range(DN_CONV - 1):
        y = y + pltpu.roll(xa, DN_CONV - 1 - i, 0)[pad:, :] * convw_ref[i:i + 1, :]
    conv_scr[0:pad, :] = qkv[t - pad:, :]
    y = _silu(y)

    q = y[:, :MIX]
    k = y[:, MIX:2 * MIX]
    v = y[:, 2 * MIX:]
    fill("prologue")
    q = q * lax.rsqrt(_head_sums(q, ones_bd) + NORM_EPS) * (DN_HD ** -0.5)
    k = k * lax.rsqrt(_head_sums(k, ones_bd) + NORM_EPS)

    beta = jax.nn.sigmoid(beta_logit)
    g = -jnp.exp(alog) * _softplus(decay_logit + dtb)
    gc = _chunk_cumsum(g)
    egc = jnp.exp(gc)
    g_last = jnp.concatenate(
        [jnp.broadcast_to(gc[(ci + 1) * c - 1:(ci + 1) * c, :], (c, MIX)) for ci in range(n_chunk)],
        axis=0)
    kb = k * beta
    vb = v * beta
    kbe = kb * egc
    qd = q * egc
    kd = k * jnp.exp(g_last - gc)
    c_dec = [jnp.exp(g_last[ci * c:ci * c + 1, :]) for ci in range(n_chunk)]

    lane = _iota((1, LANES), 1)
    g4 = gc[:, hs[DN_HEADS - 1]]
    for h in reversed(range(DN_HEADS - 1)):
        g4 = jnp.where(lane == h, gc[:, hs[h]], g4)
    g4_t = g4.T
    kd_t = [kd[:, hs[h]].T.astype(BF16) for h in heads]
    fill("norms")

    ri = _iota((t, t), 0)
    ci_ = _iota((t, t), 1)
    same = (ri // c) == (ci_ // c)
    incl = jnp.logical_and(same, ri >= ci_)
    strict = jnp.logical_and(same, ri > ci_)
    neg_inf = jnp.float32(-jnp.inf)

    aq = [_dot_nt(jnp.concatenate([kb[:, hs[h]], q[:, hs[h]]], axis=0), k[:, hs[h]])
          for h in heads]
    lower, attn, sol = [], [], []
    for h in heads:
        g_col = gc[:, hs[h]]
        diff = jnp.concatenate([g_col] * (t // DN_HD), axis=1) - g4_t[h:h + 1, :]
        decay = jnp.exp(jnp.where(incl, diff, neg_inf))
        lower.append(jnp.where(strict, aq[h][:t] * decay, 0.0).astype(BF16))
        attn.append((aq[h][t:] * decay).astype(BF16))
        sol.append(jnp.concatenate([vb[:, hs[h]], kbe[:, hs[h]]], axis=1))
    fill("decay")

    sol = [sol[h] - _dot(lower[h], sol[h]) for h in heads]
    power = [_dot(lower[h], lower[h]).astype(BF16) for h in heads]
    n_sq = 2
    while True:
        sol = [sol[h] + _dot(power[h], sol[h]) for h in heads]
        if 2 * n_sq >= c:
            break
        power = [_dot(power[h], power[h]).astype(BF16) for h in heads]
        n_sq *= 2

    row_chunk = _iota((2 * c, 1), 0) // c
    gn = []
    for h in heads:
        per_chunk = []
        for ci in range(n_chunk):
            pair = slice((ci // 2) * 2 * c, (ci // 2 + 1) * 2 * c)
            rhs = jnp.where(row_chunk == (ci % 2), sol[h][pair], 0.0)
            per_chunk.append(_dot(kd_t[h][:, pair], rhs))
        gn.append(per_chunk)

    def scan_and_outputs():
        state = [state_scr[h] for h in heads]
        starts = [[] for _ in heads]
        for ci in range(n_chunk):
            for h in heads:
                starts[h].append(state[h])
            fill(("scan", ci))
            upd = [_dot(gn[h][ci][:, DN_HD:], state[h]) for h in heads]
            state = [state[h] * c_dec[ci][:, hs[h]]
                     + (gn[h][ci][:, :DN_HD] - upd[h]) for h in heads]

        fill("outputs")
        xs = [[_dot(jnp.concatenate([qd[ci * c:(ci + 1) * c, hs[h]],
                                     sol[h][ci * c:(ci + 1) * c, DN_HD:]], axis=0), starts[h][ci])
               for ci in range(n_chunk)] for h in heads]
        outs = []
        for h in heads:
            xq = jnp.concatenate([xs[h][ci][:c] for ci in range(n_chunk)], axis=0)
            xw = jnp.concatenate([xs[h][ci][c:] for ci in range(n_chunk)], axis=0)
            outs.append(xq + _dot(attn[h], sol[h][:, :DN_HD] - xw))
        o = jnp.concatenate(outs, axis=1)
        o = (o * lax.rsqrt(_head_sums(o, ones_bd) * (1.0 / DN_HD) + NORM_EPS) * dnorm
             * _silu(get_z()))
        for h in heads:
            state_scr[h] = state[h]
        return o

    return scan_and_outputs


def _mixer_kernel(x_ref, cos_ref, sin_ref, ones_ref, anorm_ref, wfront_ref, wtail_ref,
                  lng_ref, lnb_ref, sguw_ref, sgub_ref, sinks_ref, convw_ref, alog_ref, dtb_ref,
                  dnorm_ref, wbr_ref, wout_ref, o_ref, kv_scr, conv_scr, state_scr, merged_scr,
                  gate_scr, wgate_scr, u_scr, vn_scr, probs_scr):
    seq_start = pl.program_id(1) == 0

    @pl.when(jnp.logical_and(pl.program_id(0) == 0, seq_start))
    def _():
        for n in range(3):
            window = wtail_ref[:, n * D_MODEL:(n + 1) * D_MODEL + GATE_OFF].astype(F32)
            wgate_scr[:, n * D_MODEL:(n + 1) * D_MODEL] = window[:, GATE_OFF:].astype(BF16)

    @pl.when(seq_start)
    def _():
        kv_scr[...] = jnp.zeros_like(kv_scr)
        conv_scr[0:SUBLANES, :] = jnp.zeros((SUBLANES, 3 * MIX), F32)
        state_scr[...] = jnp.zeros_like(state_scr)

    x = x_ref[...]
    h = _rmsnorm(x, anorm_ref[...]).astype(BF16)

    def proj(w_ref, lo, width):
        return jnp.dot(h, w_ref[:, lo:lo + width], preferred_element_type=F32)

    def gate(n, lo=0, width=D_MODEL):
        return jax.nn.sigmoid(proj(wgate_scr, n * D_MODEL + lo, width))

    park = {}

    def fill(stage):
        if stage == "prologue":
            qkv_b = proj(wfront_ref, C_QKVB, MIX + 2 * SWA_KV * SWA_HD)
            park["swa"] = _sliding_window_attention(
                qkv_b[:, :MIX], qkv_b[:, MIX:MIX + LANES], qkv_b[:, MIX + LANES:],
                cos_ref[...], sin_ref[...], sinks_ref, ones_ref[0:LANES, 0:LANES], kv_scr,
                probs_scr, seq_start)
            park["uv"] = proj(wfront_ref, C_UV, 2 * MIX)
            gate_scr[0] = gate(0)
            next(park["swa"])
        elif stage == "norms":
            gate_scr[1] = gate(1)
        elif stage == "decay":
            park["z"] = proj(wfront_ref, C_Z, MIX)
            next(park["swa"])
            uv = park.pop("uv")
            _spatial_gating_prep(uv[:, :MIX], uv[:, MIX:], lng_ref[...], lnb_ref[...],
                                 u_scr, vn_scr)
        elif stage[0] == "scan":
            lo = stage[1] * 2 * LANES
            gate_scr[2, :, lo:lo + 2 * LANES] = gate(2, lo, 2 * LANES)

    logits = proj(wtail_ref, 0, LANES)
    per_head = lambda lo: jnp.concatenate(
        [jnp.broadcast_to(logits[:, lo + j:lo + j + 1], (logits.shape[0], DN_HD))
         for j in range(DN_HEADS)], axis=1)
    dn_tail = _gated_deltanet(proj(wfront_ref, C_QKVC, 3 * MIX), lambda: park.pop("z"),
                              per_head(0), per_head(DN_HEADS), convw_ref,
                              alog_ref[...], dtb_ref[...], dnorm_ref[...], ones_ref[...],
                              conv_scr, state_scr, fill)

    out_c = dn_tail()
    out_a = _spatial_gating_mix(u_scr, vn_scr, sguw_ref, sgub_ref)
    out_b = next(park.pop("swa"))
    merged = (gate_scr[0] * _dot(out_a, wbr_ref[0]) + gate_scr[1] * _dot(out_b, wbr_ref[1])
              + gate_scr[2] * _dot(out_c, wbr_ref[2]))
    o_ref[...] = x + _dot(merged, wout_ref[...])


def _ffn_kernel(x_ref, fnorm_ref, wgu_ref, wd_ref, final_ref, o_ref, *, apply_final_norm):
    x = x_ref[...]
    h = _rmsnorm(x, fnorm_ref[...]).astype(BF16)
    gu = jnp.dot(h, wgu_ref[...], preferred_element_type=F32)
    act = _silu(gu[:, :D_FF]) * gu[:, D_FF:]
    y = x + _dot(act, wd_ref[...])
    if apply_final_norm:
        y = _rmsnorm(y, final_ref[...])
    o_ref[...] = y


def _resident(shape_tail, layer):
    n = len(shape_tail)
    return pl.BlockSpec((None,) + tuple(shape_tail), lambda *_: (layer,) + (0,) * n,
                        pipeline_mode=pl.Buffered(1))


def _mixer_call(x, rope, params, layer):
    batch, seq, _ = x.shape
    t = MIXER_TILE
    tok = lambda w: pl.BlockSpec((None, t, w), lambda b, i: (b, i, 0))
    in_specs = [
        tok(D_MODEL), tok(LANES), tok(LANES),
        pl.BlockSpec((MXU_DIM, MXU_DIM), lambda b, i: (0, 0), pipeline_mode=pl.Buffered(1)),
        _resident((1, D_MODEL), layer),
        _resident((D_MODEL, C_FRONT_END), layer),
        _resident((D_MODEL, GATE_OFF + 3 * D_MODEL), layer),
        _resident((1, MIX), layer), _resident((1, MIX), layer),
        _resident((SGU_GROUPS, BLOCK, BLOCK), layer), _resident((BLOCK, SGU_GROUPS), layer),
        _resident((SWA_HEADS, LANES), layer),
        _resident((DN_CONV, 3 * MIX), layer),
        _resident((1, MIX), layer), _resident((1, MIX), layer), _resident((1, MIX), layer),
        _resident((3, MIX, D_MODEL), layer), _resident((D_MODEL, D_MODEL), layer),
    ]
    return pl.pallas_call(
        _mixer_kernel,
        out_shape=jax.ShapeDtypeStruct(x.shape, F32),
        grid=(batch, seq // t),
        in_specs=in_specs,
        out_specs=tok(D_MODEL),
        scratch_shapes=[
            pltpu.VMEM((2, BLOCK, LANES), F32),
            pltpu.VMEM((SUBLANES, 3 * MIX), F32),
            pltpu.VMEM((DN_HEADS, DN_HD, DN_HD), F32),
            pltpu.VMEM((t, D_MODEL), F32),
            pltpu.VMEM((3, t, D_MODEL), F32),
            pltpu.VMEM((D_MODEL, 3 * D_MODEL), BF16),
            pltpu.VMEM((t, MIX), F32),
            pltpu.VMEM((t, MIX), BF16),
            pltpu.VMEM((t // BLOCK * SWA_KV, SWA_HEADS // SWA_KV * BLOCK, 2 * BLOCK), BF16),
        ],
        compiler_params=pltpu.CompilerParams(
            dimension_semantics=("arbitrary", "arbitrary"),
            vmem_limit_bytes=VMEM_LIMIT_BYTES),
        name=f"mixer_l{layer}",
    )(x, *rope, params["head_ones"], params["attn_norm"], params["w_in"],
      params["w_tail"], params["sgu_ln_g"], params["sgu_ln_b"],
      params["sgu_w"], params["sgu_b"], params["attn_sinks"], params["dn_conv_w"],
      params["dn_a_log"], params["dn_dt_bias"], params["dn_norm"], params["w_branch"],
      params["w_out"])


def _ffn_call(x, params, layer, apply_final_norm):
    batch, seq, _ = x.shape
    t = FFN_TILE
    tok = pl.BlockSpec((None, t, D_MODEL), lambda b, i: (b, i, 0))
    return pl.pallas_call(
        functools.partial(_ffn_kernel, apply_final_norm=apply_final_norm),
        out_shape=jax.ShapeDtypeStruct(x.shape, F32),
        grid=(batch, seq // t),
        in_specs=[
            tok,
            _resident((1, D_MODEL), layer),
            _resident((D_MODEL, 2 * D_FF), layer),
            _resident((D_FF, D_MODEL), layer),
            pl.BlockSpec((1, D_MODEL), lambda b, i: (0, 0)),
        ],
        out_specs=tok,
        compiler_params=pltpu.CompilerParams(
            dimension_semantics=("arbitrary", "arbitrary"),
            vmem_limit_bytes=VMEM_LIMIT_BYTES),
        name=f"ffn_l{layer}",
    )(x, params["ffn_norm"], params["w_gate_up"], params["w_down"], params["final_norm"])


def _rope_tables(positions):
    half = ROPE_DIM // 2
    inv_freq = ROPE_THETA ** (-jnp.arange(0, ROPE_DIM, 2, dtype=F32) / ROPE_DIM)
    ang = positions.astype(F32)[..., None] * inv_freq
    cos, sin = lax.optimization_barrier((jnp.cos(ang), jnp.sin(ang)))
    rotary = (jnp.arange(LANES) % SWA_HD) < ROPE_DIM
    reps = (1,) * (ang.ndim - 1) + (LANES // half,)
    return jnp.where(rotary, jnp.tile(cos, reps), 1.0), jnp.where(rotary, jnp.tile(sin, reps), 0.0)


def _pack_params(attn_norm, w_in, sgu_ln_g, sgu_ln_b, sgu_w, sgu_b, attn_sinks, dn_conv_w,
                 dn_a_log, dn_dt_bias, dn_norm, w_branch, w_out, ffn_norm, w_gate_up, w_down,
                 final_norm):
    depth = w_in.shape[0]
    per_head = lambda a: jnp.repeat(a, DN_HD, axis=-1)
    lane_head = jnp.arange(MXU_DIM) // DN_HD
    w_in16 = w_in.astype(BF16)
    return {
        "head_ones": (lane_head[:, None] == lane_head[None, :]).astype(BF16),
        "attn_norm": attn_norm[:, None, :],
        "w_in": w_in16,
        "w_tail": w_in16[:, :, C_FRONT_END:],
        "sgu_ln_g": sgu_ln_g[:, None, :],
        "sgu_ln_b": sgu_ln_b[:, None, :],
        "sgu_w": sgu_w,
        "sgu_b": jnp.swapaxes(sgu_b, 1, 2),
        "attn_sinks": jnp.broadcast_to(attn_sinks[:, :, None], (depth, SWA_HEADS, LANES)),
        "dn_conv_w": dn_conv_w,
        "dn_a_log": per_head(dn_a_log)[:, None, :],
        "dn_dt_bias": per_head(dn_dt_bias)[:, None, :],
        "dn_norm": jnp.tile(dn_norm, (1, DN_HEADS))[:, None, :],
        "w_branch": w_branch.astype(BF16),
        "w_out": w_out.astype(BF16),
        "ffn_norm": ffn_norm[:, None, :],
        "w_gate_up": w_gate_up.astype(BF16),
        "w_down": w_down.astype(BF16),
        "final_norm": final_norm[None, :],
    }


def kernel(x, positions, attn_norm, w_in, sgu_ln_g, sgu_ln_b, sgu_w, sgu_b, attn_sinks,
           dn_conv_w, dn_a_log, dn_dt_bias, dn_norm, w_branch, w_out, ffn_norm, w_gate_up,
           w_down, final_norm):
    assert MIXER_TILE == MXU_DIM == 4 * DN_CHUNK
    assert x.shape[1] % MIXER_TILE == 0 and x.shape[1] % FFN_TILE == 0
    assert x.shape[2] == D_MODEL
    assert w_in.shape[2] == C_FRONT_END + GATE_OFF + 3 * D_MODEL
    depth = w_in.shape[0]
    params = _pack_params(attn_norm, w_in, sgu_ln_g, sgu_ln_b, sgu_w, sgu_b, attn_sinks,
                          dn_conv_w, dn_a_log, dn_dt_bias, dn_norm, w_branch, w_out, ffn_norm,
                          w_gate_up, w_down, final_norm)
    rope = _rope_tables(positions)
    for layer in range(depth):
        x = _mixer_call(x, rope, params, layer)
        x = _ffn_call(x, params, layer, apply_final_norm=(layer == depth - 1))
    return x
```

```python
import functools

import jax
import jax.numpy as jnp
from jax import lax
from jax.experimental import pallas as pl
from jax.experimental.pallas import tpu as pltpu

D_MODEL = 1024
MIX = 512
NORM_EPS = 1e-6

SGU_GROUPS = 4
BLOCK = 128

SWA_HEADS = 8
SWA_KV = 2
SWA_HD = 64
ROPE_THETA = 500000.0
ROPE_DIM = 16

DN_HEADS = 4
DN_HD = 128
DN_CONV = 4
DN_CHUNK = 64

D_FF = 2816

LANES = 128
SUBLANES = 8
MXU_DIM = 256

C_UV = 0
C_QKVB = C_UV + 2 * MIX
C_QKVC = C_QKVB + MIX + 2 * SWA_KV * SWA_HD
C_Z = C_QKVC + 3 * MIX
C_FRONT_END = C_Z + MIX
GATE_OFF = 2 * DN_HEADS

MIXER_TILE = 256
FFN_TILE = 512
VMEM_LIMIT_BYTES = 56 * 1024 * 1024

BF16 = jnp.bfloat16
F32 = jnp.float32
_GELU_C = 0.7978845608028654


def _dot(a, b):
    return jnp.dot(a.astype(BF16), b.astype(BF16), preferred_element_type=F32)


def _dot_nt(a, b):
    return lax.dot_general(a.astype(BF16), b.astype(BF16), (((1,), (1,)), ((), ())),
                           preferred_element_type=F32)


def _rmsnorm(x, g):
    return x * lax.rsqrt(jnp.mean(x * x, axis=-1, keepdims=True) + NORM_EPS) * g


def _gelu_tanh(x):
    inner = x * (_GELU_C + (_GELU_C * 0.044715) * (x * x))
    return (0.5 * x) * (1.0 + jnp.tanh(inner))


def _silu(x):
    return x * jax.nn.sigmoid(x)


def _softplus(x):
    return jnp.maximum(x, 0.0) + jnp.log(1.0 + jnp.exp(-jnp.abs(x)))


def _iota(shape, dim):
    return lax.broadcasted_iota(jnp.int32, shape, dim)


def _spatial_gating_prep(u_pre, v_pre, ln_g, ln_b, u_scr, vn_scr):
    u_scr[...] = _gelu_tanh(u_pre)
    v = _gelu_tanh(v_pre)
    vc = v - jnp.mean(v, axis=-1, keepdims=True)
    vn_scr[...] = (vc * lax.rsqrt(jnp.mean(vc * vc, axis=-1, keepdims=True) + NORM_EPS) * ln_g
                   + ln_b).astype(BF16)


def _spatial_gating_mix(u_scr, vn_scr, sguw_ref, sgub_ref):
    u = u_scr[...]
    vn = vn_scr[...]
    t = u.shape[0]
    causal = _iota((BLOCK, BLOCK), 0) >= _iota((BLOCK, BLOCK), 1)
    groups = []
    for g in range(SGU_GROUPS):
        w = jnp.where(causal, sguw_ref[g], 0.0).astype(BF16)
        bias = sgub_ref[:, g:g + 1]
        cols = slice(g * LANES, (g + 1) * LANES)
        mixed = [_dot(w, vn[blk * BLOCK:(blk + 1) * BLOCK, cols]) + bias
                 for blk in range(t // BLOCK)]
        groups.append(u[:, cols] * jnp.concatenate(mixed, axis=0))
    return jnp.concatenate(groups, axis=1)


def _rope(x, cos, sin):
    half = ROPE_DIM // 2
    first_half = (_iota((1, LANES), 1) % SWA_HD) < half
    sin_a = jnp.where(first_half, -sin, 0.0)
    sin_b = jnp.where(first_half, 0.0, sin)
    n = x.shape[1] // LANES
    if n > 1:
        cos = jnp.concatenate([cos] * n, axis=1)
        sin_a = jnp.concatenate([sin_a] * n, axis=1)
        sin_b = jnp.concatenate([sin_b] * n, axis=1)
    width = x.shape[1]
    return (x * cos + pltpu.roll(x, width - half, 1) * sin_a + pltpu.roll(x, half, 1) * sin_b)


def _sliding_window_attention(q, k, v, cos, sin, sinks_ref, ones, kv_scr, probs_scr, seq_start):
    t = q.shape[0]
    n_blk = t // BLOCK
    group = SWA_HEADS // SWA_KV
    q = _rope(q, cos, sin) * (SWA_HD ** -0.5)
    k = _rope(k, cos, sin)

    kcat = jnp.concatenate([kv_scr[0], k], axis=0)
    vcat = jnp.concatenate([kv_scr[1], v], axis=0)

    lane = _iota((1, LANES), 1)
    lo = lane < SWA_HD
    krot = pltpu.roll(kcat, SWA_HD, 1)
    vrot = pltpu.roll(vcat, SWA_HD, 1)
    kdup = [jnp.where(lo, kcat, krot).astype(BF16), jnp.where(lo, krot, kcat).astype(BF16)]
    vlo = [jnp.where(lo, vcat, 0.0).astype(BF16), jnp.where(lo, vrot, 0.0).astype(BF16)]
    vhi = [jnp.where(lo, 0.0, vrot).astype(BF16), jnp.where(lo, 0.0, vcat).astype(BF16)]

    rows4 = group * BLOCK
    cur_mask = (_iota((rows4, BLOCK), 0) % BLOCK) >= _iota((rows4, BLOCK), 1)
    neg_inf = jnp.float32(-jnp.inf)
    units = [(blk, kh) for blk in range(n_blk) for kh in range(SWA_KV)]

    lhs = []
    for blk, kh in units:
        rows = slice(blk * BLOCK, (blk + 1) * BLOCK)
        s0 = q[rows, (2 * kh) * LANES:(2 * kh + 1) * LANES]
        s1 = q[rows, (2 * kh + 1) * LANES:(2 * kh + 2) * LANES]
        lhs.append(jnp.concatenate([jnp.where(lo, s0, 0.0), jnp.where(lo, s1, 0.0),
                                    jnp.where(lo, 0.0, s0), jnp.where(lo, 0.0, s1)],
                                   axis=0).astype(BF16))
    logits = [_dot_nt(lhs[i], kdup[kh][blk * BLOCK:(blk + 2) * BLOCK])
              for i, (blk, kh) in enumerate(units)]
    yield
    n_unit = len(units)
    sinks = [jnp.concatenate(
        [jnp.broadcast_to(sinks_ref[group * kh + 2 * j + half:group * kh + 2 * j + half + 1, :],
                          (BLOCK, LANES)) for half in range(2) for j in range(2)], axis=0)
        for kh in range(SWA_KV)]
    act = []
    for i, (blk, kh) in enumerate(units):
        l_prev = logits[i][:, :BLOCK]
        if blk == 0:
            l_prev = jnp.where(seq_start, neg_inf, l_prev)
        act.append(jnp.where(cur_mask, logits[i][:, BLOCK:], l_prev))
    m = [jnp.maximum(jnp.broadcast_to(jnp.max(act[i], axis=1, keepdims=True), act[i].shape),
                     sinks[units[i][1]]) for i in range(n_unit)]
    p = [jnp.exp(act[i] - m[i]) for i in range(n_unit)]
    den = [_dot(p[i], ones) + jnp.exp(sinks[units[i][1]] - m[i]) for i in range(n_unit)]
    for i in range(n_unit):
        pn = p[i] * (1.0 / den[i])
        probs_scr[i] = jnp.concatenate([jnp.where(cur_mask, 0.0, pn), jnp.where(cur_mask, pn, 0.0)],
                                       axis=1).astype(BF16)
    yield
    probs = [probs_scr[i] for i in range(len(units))]
    out_rows = []
    for blk in range(n_blk):
        slabs = []
        for kh in range(SWA_KV):
            i = blk * SWA_KV + kh
            keys = slice(blk * BLOCK, (blk + 2) * BLOCK)
            res = (_dot(probs[i][:2 * BLOCK], vlo[kh][keys])
                   + _dot(probs[i][2 * BLOCK:], vhi[kh][keys]))
            slabs += [res[:BLOCK], res[BLOCK:]]
        out_rows.append(jnp.concatenate(slabs, axis=1))
    kv_scr[0] = k[t - BLOCK:, :]
    kv_scr[1] = v[t - BLOCK:, :]
    yield jnp.concatenate(out_rows, axis=0)


def _chunk_cumsum(x):
    row = _iota(x.shape, 0) % DN_CHUNK
    shift = 1
    while shift < DN_CHUNK:
        x = x + jnp.where(row >= shift, pltpu.roll(x, shift, 0), 0.0)
        shift *= 2
    return x


def _head_sums(x, ones_bd):
    xx = x * x
    return jnp.concatenate([_dot(xx[:, i * MXU_DIM:(i + 1) * MXU_DIM], ones_bd)
                            for i in range(x.shape[1] // MXU_DIM)], axis=1)


def _gated_deltanet(qkv, get_z, logits, convw_ref, alog, dtb, dnorm, ones_bd,
                    conv_scr, state_scr, fill):
    t = qkv.shape[0]
    c = DN_CHUNK
    n_chunk = t // c
    pad = SUBLANES
    heads = range(DN_HEADS)
    hs = [slice(h * DN_HD, (h + 1) * DN_HD) for h in heads]

    xa = jnp.concatenate([conv_scr[0:pad, :], qkv], axis=0)
    y = qkv * convw_ref[DN_CONV - 1:DN_CONV, :]
    for i in range(DN_CONV - 1):
        y = y + pltpu.roll(xa, DN_CONV - 1 - i, 0)[pad:, :] * convw_ref[i:i + 1, :]
    conv_scr[0:pad, :] = qkv[t - pad:, :]
    y = _silu(y)

    q = y[:, :MIX]
    k = y[:, MIX:2 * MIX]
    v = y[:, 2 * MIX:]
    fill("prologue")
    q = q * lax.rsqrt(_head_sums(q, ones_bd) + NORM_EPS) * (DN_HD ** -0.5)
    k = k * lax.rsqrt(_head_sums(k, ones_bd) + NORM_EPS)

    def spread(slab, lo):
        return jnp.concatenate([jnp.broadcast_to(slab[:, lo + j:lo + j + 1], (t, DN_HD))
                                for j in heads], axis=1)

    dl = DN_HEADS
    gc_c = _chunk_cumsum(-jnp.exp(alog) * _softplus(logits + dtb))
    last = [gc_c[(ci + 1) * c - 1:(ci + 1) * c, :] for ci in range(n_chunk)]
    g_last_c = jnp.concatenate([jnp.broadcast_to(r, (c, LANES)) for r in last], axis=0)
    beta = spread(jax.nn.sigmoid(logits), 0)
    gc = spread(gc_c, dl)
    egc = spread(jnp.exp(gc_c), dl)
    kb = k * beta
    vb = v * beta
    kbe = kb * egc
    qd = q * egc
    kd = k * spread(jnp.exp(g_last_c - gc_c), dl)
    c_dec = [jnp.exp(r) for r in last]

    g4_t = gc_c.T
    kd_t = [kd[:, hs[h]].T.astype(BF16) for h in heads]
    fill("norms")

    ri = _iota((t, t), 0)
    ci_ = _iota((t, t), 1)
    same = (ri // c) == (ci_ // c)
    incl = jnp.logical_and(same, ri >= ci_)
    strict = jnp.logical_and(same, ri > ci_)
    neg_inf = jnp.float32(-jnp.inf)

    aq = [_dot_nt(jnp.concatenate([kb[:, hs[h]], q[:, hs[h]]], axis=0), k[:, hs[h]])
          for h in heads]
    lower, attn, sol = [], [], []
    for h in heads:
        g_col = gc[:, hs[h]]
        diff = jnp.concatenate([g_col] * (t // DN_HD), axis=1) - g4_t[dl + h:dl + h + 1, :]
        decay = jnp.exp(jnp.where(incl, diff, neg_inf))
        lower.append(jnp.where(strict, aq[h][:t] * decay, 0.0).astype(BF16))
        attn.append((aq[h][t:] * decay).astype(BF16))
        sol.append(jnp.concatenate([vb[:, hs[h]], kbe[:, hs[h]]], axis=1))
    fill("decay")

    sol = [sol[h] - _dot(lower[h], sol[h]) for h in heads]
    power = [_dot(lower[h], lower[h]).astype(BF16) for h in heads]
    n_sq = 2
    while True:
        sol = [sol[h] + _dot(power[h], sol[h]) for h in heads]
        if 2 * n_sq >= c:
            break
        power = [_dot(power[h], power[h]).astype(BF16) for h in heads]
        n_sq *= 2

    row_chunk = _iota((2 * c, 1), 0) // c
    gn = []
    for h in heads:
        per_chunk = []
        for ci in range(n_chunk):
            pair = slice((ci // 2) * 2 * c, (ci // 2 + 1) * 2 * c)
            rhs = jnp.where(row_chunk == (ci % 2), sol[h][pair], 0.0)
            per_chunk.append(_dot(kd_t[h][:, pair], rhs))
        gn.append(per_chunk)

    def scan_and_outputs():
        state = [state_scr[h] for h in heads]
        starts = [[] for _ in heads]
        for ci in range(n_chunk):
            for h in heads:
                starts[h].append(state[h])
            fill(("scan", ci))
            upd = [_dot(gn[h][ci][:, DN_HD:], state[h]) for h in heads]
            state = [state[h] * c_dec[ci][:, dl + h:dl + h + 1]
                     + (gn[h][ci][:, :DN_HD] - upd[h]) for h in heads]

        fill("outputs")
        xs = [[_dot(jnp.concatenate([qd[ci * c:(ci + 1) * c, hs[h]],
                                     sol[h][ci * c:(ci + 1) * c, DN_HD:]], axis=0), starts[h][ci])
               for ci in range(n_chunk)] for h in heads]
        outs = []
        for h in heads:
            xq = jnp.concatenate([xs[h][ci][:c] for ci in range(n_chunk)], axis=0)
            xw = jnp.concatenate([xs[h][ci][c:] for ci in range(n_chunk)], axis=0)
            outs.append(xq + _dot(attn[h], sol[h][:, :DN_HD] - xw))
        o = jnp.concatenate(outs, axis=1)
        o = (o * lax.rsqrt(_head_sums(o, ones_bd) * (1.0 / DN_HD) + NORM_EPS) * dnorm
             * _silu(get_z()))
        for h in heads:
            state_scr[h] = state[h]
        return o

    return scan_and_outputs


def _mixer_kernel(x_ref, cos_ref, sin_ref, ones_ref, anorm_ref, wfront_ref, wtail_ref,
                  lng_ref, lnb_ref, sguw_ref, sgub_ref, sinks_ref, convw_ref, alog_ref, dtb_ref,
                  dnorm_ref, wbr_ref, wout_ref, o_ref, kv_scr, conv_scr, state_scr, merged_scr,
                  gate_scr, wgate_scr, u_scr, vn_scr, probs_scr):
    seq_start = pl.program_id(1) == 0

    @pl.when(jnp.logical_and(pl.program_id(0) == 0, seq_start))
    def _():
        for n in range(3):
            window = wtail_ref[:, n * D_MODEL:(n + 1) * D_MODEL + GATE_OFF].astype(F32)
            wgate_scr[:, n * D_MODEL:(n + 1) * D_MODEL] = window[:, GATE_OFF:].astype(BF16)

    @pl.when(seq_start)
    def _():
        kv_scr[...] = jnp.zeros_like(kv_scr)
        conv_scr[0:SUBLANES, :] = jnp.zeros((SUBLANES, 3 * MIX), F32)
        state_scr[...] = jnp.zeros_like(state_scr)

    x = x_ref[...]
    h = _rmsnorm(x, anorm_ref[...]).astype(BF16)

    def proj(w_ref, lo, width):
        return jnp.dot(h, w_ref[:, lo:lo + width], preferred_element_type=F32)

    def gate(n, lo=0, width=D_MODEL):
        return jax.nn.sigmoid(proj(wgate_scr, n * D_MODEL + lo, width))

    park = {}

    def fill(stage):
        if stage == "prologue":
            qkv_b = proj(wfront_ref, C_QKVB, MIX + 2 * SWA_KV * SWA_HD)
            park["swa"] = _sliding_window_attention(
                qkv_b[:, :MIX], qkv_b[:, MIX:MIX + LANES], qkv_b[:, MIX + LANES:],
                cos_ref[...], sin_ref[...], sinks_ref, ones_ref[0:LANES, 0:LANES], kv_scr,
                probs_scr, seq_start)
            park["uv"] = proj(wfront_ref, C_UV, 2 * MIX)
            gate_scr[0] = gate(0)
            next(park["swa"])
        elif stage == "norms":
            gate_scr[1] = gate(1)
        elif stage == "decay":
            park["z"] = proj(wfront_ref, C_Z, MIX)
            next(park["swa"])
            uv = park.pop("uv")
            _spatial_gating_prep(uv[:, :MIX], uv[:, MIX:], lng_ref[...], lnb_ref[...],
                                 u_scr, vn_scr)
        elif stage[0] == "scan":
            lo = stage[1] * 2 * LANES
            gate_scr[2, :, lo:lo + 2 * LANES] = gate(2, lo, 2 * LANES)

    dn_tail = _gated_deltanet(proj(wfront_ref, C_QKVC, 3 * MIX), lambda: park.pop("z"),
                              proj(wtail_ref, 0, LANES), convw_ref,
                              alog_ref[...], dtb_ref[...], dnorm_ref[...], ones_ref[...],
                              conv_scr, state_scr, fill)

    out_c = dn_tail()
    out_a = _spatial_gating_mix(u_scr, vn_scr, sguw_ref, sgub_ref)
    out_b = next(park.pop("swa"))
    merged = (gate_scr[0] * _dot(out_a, wbr_ref[0]) + gate_scr[1] * _dot(out_b, wbr_ref[1])
              + gate_scr[2] * _dot(out_c, wbr_ref[2]))
    o_ref[...] = x + _dot(merged, wout_ref[...])


def _ffn_kernel(x_ref, fnorm_ref, wgu_ref, wd_ref, final_ref, o_ref, *, apply_final_norm):
    x = x_ref[...]
    h = _rmsnorm(x, fnorm_ref[...]).astype(BF16)
    gu = jnp.dot(h, wgu_ref[...], preferred_element_type=F32)
    act = _silu(gu[:, :D_FF]) * gu[:, D_FF:]
    y = x + _dot(act, wd_ref[...])
    if apply_final_norm:
        y = _rmsnorm(y, final_ref[...])
    o_ref[...] = y


def _resident(shape_tail, layer):
    n = len(shape_tail)
    return pl.BlockSpec((None,) + tuple(shape_tail), lambda *_: (layer,) + (0,) * n,
                        pipeline_mode=pl.Buffered(1))


def _mixer_call(x, rope, params, layer):
    batch, seq, _ = x.shape
    t = MIXER_TILE
    tok = lambda w: pl.BlockSpec((None, t, w), lambda b, i: (b, i, 0))
    in_specs = [
        tok(D_MODEL), tok(LANES), tok(LANES),
        pl.BlockSpec((MXU_DIM, MXU_DIM), lambda b, i: (0, 0), pipeline_mode=pl.Buffered(1)),
        _resident((1, D_MODEL), layer),
        _resident((D_MODEL, C_FRONT_END), layer),
        _resident((D_MODEL, GATE_OFF + 3 * D_MODEL), layer),
        _resident((1, MIX), layer), _resident((1, MIX), layer),
        _resident((SGU_GROUPS, BLOCK, BLOCK), layer), _resident((BLOCK, SGU_GROUPS), layer),
        _resident((SWA_HEADS, LANES), layer),
        _resident((DN_CONV, 3 * MIX), layer),
        _resident((1, LANES), layer), _resident((1, LANES), layer), _resident((1, MIX), layer),
        _resident((3, MIX, D_MODEL), layer), _resident((D_MODEL, D_MODEL), layer),
    ]
    return pl.pallas_call(
        _mixer_kernel,
        out_shape=jax.ShapeDtypeStruct(x.shape, F32),
        grid=(batch, seq // t),
        in_specs=in_specs,
        out_specs=tok(D_MODEL),
        scratch_shapes=[
            pltpu.VMEM((2, BLOCK, LANES), F32),
            pltpu.VMEM((SUBLANES, 3 * MIX), F32),
            pltpu.VMEM((DN_HEADS, DN_HD, DN_HD), F32),
            pltpu.VMEM((t, D_MODEL), F32),
            pltpu.VMEM((3, t, D_MODEL), F32),
            pltpu.VMEM((D_MODEL, 3 * D_MODEL), BF16),
            pltpu.VMEM((t, MIX), F32),
            pltpu.VMEM((t, MIX), BF16),
            pltpu.VMEM((t // BLOCK * SWA_KV, SWA_HEADS // SWA_KV * BLOCK, 2 * BLOCK), BF16),
        ],
        compiler_params=pltpu.CompilerParams(
            dimension_semantics=("arbitrary", "arbitrary"),
            vmem_limit_bytes=VMEM_LIMIT_BYTES),
        name=f"mixer_l{layer}",
    )(x, *rope, params["head_ones"], params["attn_norm"], params["w_in"],
      params["w_tail"], params["sgu_ln_g"], params["sgu_ln_b"],
      params["sgu_w"], params["sgu_b"], params["attn_sinks"], params["dn_conv_w"],
      params["dn_a_log"], params["dn_dt_bias"], params["dn_norm"], params["w_branch"],
      params["w_out"])


def _ffn_call(x, params, layer, apply_final_norm):
    batch, seq, _ = x.shape
    t = FFN_TILE
    tok = pl.BlockSpec((None, t, D_MODEL), lambda b, i: (b, i, 0))
    return pl.pallas_call(
        functools.partial(_ffn_kernel, apply_final_norm=apply_final_norm),
        out_shape=jax.ShapeDtypeStruct(x.shape, F32),
        grid=(batch, seq // t),
        in_specs=[
            tok,
            _resident((1, D_MODEL), layer),
            _resident((D_MODEL, 2 * D_FF), layer),
            _resident((D_FF, D_MODEL), layer),
            pl.BlockSpec((1, D_MODEL), lambda b, i: (0, 0)),
        ],
        out_specs=tok,
        compiler_params=pltpu.CompilerParams(
            dimension_semantics=("arbitrary", "arbitrary"),
            vmem_limit_bytes=VMEM_LIMIT_BYTES),
        name=f"ffn_l{layer}",
    )(x, params["ffn_norm"], params["w_gate_up"], params["w_down"], params["final_norm"])


def _rope_tables(positions):
    half = ROPE_DIM // 2
    inv_freq = ROPE_THETA ** (-jnp.arange(0, ROPE_DIM, 2, dtype=F32) / ROPE_DIM)
    ang = positions.astype(F32)[..., None] * inv_freq
    cos, sin = lax.optimization_barrier((jnp.cos(ang), jnp.sin(ang)))
    rotary = (jnp.arange(LANES) % SWA_HD) < ROPE_DIM
    reps = (1,) * (ang.ndim - 1) + (LANES // half,)
    return jnp.where(rotary, jnp.tile(cos, reps), 1.0), jnp.where(rotary, jnp.tile(sin, reps), 0.0)


def _pack_params(attn_norm, w_in, sgu_ln_g, sgu_ln_b, sgu_w, sgu_b, attn_sinks, dn_conv_w,
                 dn_a_log, dn_dt_bias, dn_norm, w_branch, w_out, ffn_norm, w_gate_up, w_down,
                 final_norm):
    depth = w_in.shape[0]
    decay_lanes = lambda a: jnp.pad(a, ((0, 0), (DN_HEADS, LANES - 2 * DN_HEADS)))[:, None, :]
    lane_head = jnp.arange(MXU_DIM) // DN_HD
    w_in16 = w_in.astype(BF16)
    return {
        "head_ones": (lane_head[:, None] == lane_head[None, :]).astype(BF16),
        "attn_norm": attn_norm[:, None, :],
        "w_in": w_in16,
        "w_tail": w_in16[:, :, C_FRONT_END:],
        "sgu_ln_g": sgu_ln_g[:, None, :],
        "sgu_ln_b": sgu_ln_b[:, None, :],
        "sgu_w": sgu_w,
        "sgu_b": jnp.swapaxes(sgu_b, 1, 2),
        "attn_sinks": jnp.broadcast_to(attn_sinks[:, :, None], (depth, SWA_HEADS, LANES)),
        "dn_conv_w": dn_conv_w,
        "dn_a_log": decay_lanes(dn_a_log),
        "dn_dt_bias": decay_lanes(dn_dt_bias),
        "dn_norm": jnp.tile(dn_norm, (1, DN_HEADS))[:, None, :],
        "w_branch": w_branch.astype(BF16),
        "w_out": w_out.astype(BF16),
        "ffn_norm": ffn_norm[:, None, :],
        "w_gate_up": w_gate_up.astype(BF16),
        "w_down": w_down.astype(BF16),
        "final_norm": final_norm[None, :],
    }


def kernel(x, positions, attn_norm, w_in, sgu_ln_g, sgu_ln_b, sgu_w, sgu_b, attn_sinks,
           dn_conv_w, dn_a_log, dn_dt_bias, dn_norm, w_branch, w_out, ffn_norm, w_gate_up,
           w_down, final_norm):
    assert MIXER_TILE == MXU_DIM == 4 * DN_CHUNK
    assert x.shape[1] % MIXER_TILE == 0 and x.shape[1] % FFN_TILE == 0
    assert x.shape[2] == D_MODEL
    assert w_in.shape[2] == C_FRONT_END + GATE_OFF + 3 * D_MODEL
    depth = w_in.shape[0]
    params = _pack_params(attn_norm, w_in, sgu_ln_g, sgu_ln_b, sgu_w, sgu_b, attn_sinks,
                          dn_conv_w, dn_a_log, dn_dt_bias, dn_norm, w_branch, w_out, ffn_norm,
                          w_gate_up, w_down, final_norm)
    rope = _rope_tables(positions)
    for layer in range(depth):
        x = _mixer_call(x, rope, params, layer)
        x = _ffn_call(x, params, layer, apply_final_norm=(layer == depth - 1))
    return x
```

```python
import functools

import jax
import jax.numpy as jnp
from jax import lax
from jax.experimental import pallas as pl
from jax.experimental.pallas import tpu as pltpu

D_MODEL = 1024
MIX = 512
NORM_EPS = 1e-6

SGU_GROUPS = 4
BLOCK = 128

SWA_HEADS = 8
SWA_KV = 2
SWA_HD = 64
ROPE_THETA = 500000.0
ROPE_DIM = 16

DN_HEADS = 4
DN_HD = 128
DN_CONV = 4
DN_CHUNK = 64

D_FF = 2816

LANES = 128
SUBLANES = 8
MXU_DIM = 256

C_UV = 0
C_QKVB = C_UV + 2 * MIX
C_QKVC = C_QKVB + MIX + 2 * SWA_KV * SWA_HD
C_Z = C_QKVC + 3 * MIX
C_FRONT_END = C_Z + MIX
GATE_OFF = 2 * DN_HEADS

MIXER_TILE = 256
FFN_TILE = 512
VMEM_CAPACITY_BYTES = 64 * 1024 * 1024
VMEM_LIMIT_BYTES = VMEM_CAPACITY_BYTES // 8 * 7

BF16 = jnp.bfloat16
F32 = jnp.float32
_GELU_C = 0.7978845608028654


def _dot(a, b):
    return jnp.dot(a.astype(BF16), b.astype(BF16), preferred_element_type=F32)


def _dot_nt(a, b):
    return lax.dot_general(a.astype(BF16), b.astype(BF16), (((1,), (1,)), ((), ())),
                           preferred_element_type=F32)


def _rmsnorm(x, g):
    return x * lax.rsqrt(jnp.mean(x * x, axis=-1, keepdims=True) + NORM_EPS) * g


def _gelu_tanh(x):
    inner = x * (_GELU_C + (_GELU_C * 0.044715) * (x * x))
    return (0.5 * x) * (1.0 + jnp.tanh(inner))


def _silu(x):
    return x * jax.nn.sigmoid(x)


def _softplus(x):
    return jnp.maximum(x, 0.0) + jnp.log(1.0 + jnp.exp(-jnp.abs(x)))


def _iota(shape, dim):
    return lax.broadcasted_iota(jnp.int32, shape, dim)


def _spatial_gating_prep(u_pre, v_pre, ln_g, ln_b, u_scr, vn_scr):
    u_scr[...] = _gelu_tanh(u_pre)
    v = _gelu_tanh(v_pre)
    vc = v - jnp.mean(v, axis=-1, keepdims=True)
    vn_scr[...] = (vc * lax.rsqrt(jnp.mean(vc * vc, axis=-1, keepdims=True) + NORM_EPS) * ln_g
                   + ln_b).astype(BF16)


def _spatial_gating_mix(u_scr, vn_scr, sguw_ref, sgub_ref):
    u = u_scr[...]
    vn = vn_scr[...]
    t = u.shape[0]
    causal = _iota((BLOCK, BLOCK), 0) >= _iota((BLOCK, BLOCK), 1)
    groups = []
    for g in range(SGU_GROUPS):
        w = jnp.where(causal, sguw_ref[g], 0.0).astype(BF16)
        bias = sgub_ref[:, g:g + 1]
        cols = slice(g * LANES, (g + 1) * LANES)
        mixed = [_dot(w, vn[blk * BLOCK:(blk + 1) * BLOCK, cols]) + bias
                 for blk in range(t // BLOCK)]
        groups.append(u[:, cols] * jnp.concatenate(mixed, axis=0))
    return jnp.concatenate(groups, axis=1)


def _rope(x, cos, sin):
    half = ROPE_DIM // 2
    first_half = (_iota((1, LANES), 1) % SWA_HD) < half
    sin_a = jnp.where(first_half, -sin, 0.0)
    sin_b = jnp.where(first_half, 0.0, sin)
    n = x.shape[1] // LANES
    if n > 1:
        cos = jnp.concatenate([cos] * n, axis=1)
        sin_a = jnp.concatenate([sin_a] * n, axis=1)
        sin_b = jnp.concatenate([sin_b] * n, axis=1)
    width = x.shape[1]
    return (x * cos + pltpu.roll(x, width - half, 1) * sin_a + pltpu.roll(x, half, 1) * sin_b)


def _sliding_window_attention(q, k, v, cos, sin, sinks_ref, ones, kv_scr, probs_scr, seq_start):
    t = q.shape[0]
    n_blk = t // BLOCK
    group = SWA_HEADS // SWA_KV
    q = _rope(q, cos, sin) * (SWA_HD ** -0.5)
    k = _rope(k, cos, sin)

    kcat = jnp.concatenate([kv_scr[0], k], axis=0)
    vcat = jnp.concatenate([kv_scr[1], v], axis=0)

    lane = _iota((1, LANES), 1)
    lo = lane < SWA_HD
    krot = pltpu.roll(kcat, SWA_HD, 1)
    vrot = pltpu.roll(vcat, SWA_HD, 1)
    kdup = [jnp.where(lo, kcat, krot).astype(BF16), jnp.where(lo, krot, kcat).astype(BF16)]
    vlo = [jnp.where(lo, vcat, 0.0).astype(BF16), jnp.where(lo, vrot, 0.0).astype(BF16)]
    vhi = [jnp.where(lo, 0.0, vrot).astype(BF16), jnp.where(lo, 0.0, vcat).astype(BF16)]

    rows4 = group * BLOCK
    cur_mask = (_iota((rows4, BLOCK), 0) % BLOCK) >= _iota((rows4, BLOCK), 1)
    neg_inf = jnp.float32(-jnp.inf)
    units = [(blk, kh) for blk in range(n_blk) for kh in range(SWA_KV)]

    lhs = []
    for blk, kh in units:
        rows = slice(blk * BLOCK, (blk + 1) * BLOCK)
        s0 = q[rows, (2 * kh) * LANES:(2 * kh + 1) * LANES]
        s1 = q[rows, (2 * kh + 1) * LANES:(2 * kh + 2) * LANES]
        lhs.append(jnp.concatenate([jnp.where(lo, s0, 0.0), jnp.where(lo, s1, 0.0),
                                    jnp.where(lo, 0.0, s0), jnp.where(lo, 0.0, s1)],
                                   axis=0).astype(BF16))
    logits = [_dot_nt(lhs[i], kdup[kh][blk * BLOCK:(blk + 2) * BLOCK])
              for i, (blk, kh) in enumerate(units)]
    yield
    n_unit = len(units)
    sinks = [jnp.concatenate(
        [jnp.broadcast_to(sinks_ref[group * kh + 2 * j + half:group * kh + 2 * j + half + 1, :],
                          (BLOCK, LANES)) for half in range(2) for j in range(2)], axis=0)
        for kh in range(SWA_KV)]
    act = []
    for i, (blk, kh) in enumerate(units):
        l_prev = logits[i][:, :BLOCK]
        if blk == 0:
            l_prev = jnp.where(seq_start, neg_inf, l_prev)
        act.append(jnp.where(cur_mask, logits[i][:, BLOCK:], l_prev))
    m = [jnp.maximum(jnp.broadcast_to(jnp.max(act[i], axis=1, keepdims=True), act[i].shape),
                     sinks[units[i][1]]) for i in range(n_unit)]
    p = [jnp.exp(act[i] - m[i]) for i in range(n_unit)]
    den = [_dot(p[i], ones) + jnp.exp(sinks[units[i][1]] - m[i]) for i in range(n_unit)]
    for i in range(n_unit):
        pn = p[i] * (1.0 / den[i])
        probs_scr[i] = jnp.concatenate([jnp.where(cur_mask, 0.0, pn), jnp.where(cur_mask, pn, 0.0)],
                                       axis=1).astype(BF16)
    yield
    probs = [probs_scr[i] for i in range(len(units))]
    out_rows = []
    for blk in range(n_blk):
        slabs = []
        for kh in range(SWA_KV):
            i = blk * SWA_KV + kh
            keys = slice(blk * BLOCK, (blk + 2) * BLOCK)
            res = (_dot(probs[i][:2 * BLOCK], vlo[kh][keys])
                   + _dot(probs[i][2 * BLOCK:], vhi[kh][keys]))
            slabs += [res[:BLOCK], res[BLOCK:]]
        out_rows.append(jnp.concatenate(slabs, axis=1))
    kv_scr[0] = k[t - BLOCK:, :]
    kv_scr[1] = v[t - BLOCK:, :]
    yield jnp.concatenate(out_rows, axis=0)


def _chunk_cumsum(x):
    row = _iota(x.shape, 0) % DN_CHUNK
    shift = 1
    while shift < DN_CHUNK:
        x = x + jnp.where(row >= shift, pltpu.roll(x, shift, 0), 0.0)
        shift *= 2
    return x


def _head_sums(x, ones_bd):
    xx = x * x
    return jnp.concatenate([_dot(xx[:, i * MXU_DIM:(i + 1) * MXU_DIM], ones_bd)
                            for i in range(x.shape[1] // MXU_DIM)], axis=1)


def _gated_deltanet(qkv, get_z, logits, convw_ref, alog, dtb, dnorm, ones_bd,
                    conv_scr, state_scr, fill):
    t = qkv.shape[0]
    c = DN_CHUNK
    n_chunk = t // c
    pad = SUBLANES
    heads = range(DN_HEADS)
    hs = [slice(h * DN_HD, (h + 1) * DN_HD) for h in heads]

    xa = jnp.concatenate([conv_scr[0:pad, :], qkv], axis=0)
    y = qkv * convw_ref[DN_CONV - 1:DN_CONV, :]
    for i in range(DN_CONV - 1):
        y = y + pltpu.roll(xa, DN_CONV - 1 - i, 0)[pad:, :] * convw_ref[i:i + 1, :]
    conv_scr[0:pad, :] = qkv[t - pad:, :]
    y = _silu(y)

    q = y[:, :MIX]
    k = y[:, MIX:2 * MIX]
    v = y[:, 2 * MIX:]
    fill("prologue")
    q = q * lax.rsqrt(_head_sums(q, ones_bd) + NORM_EPS) * (DN_HD ** -0.5)
    k = k * lax.rsqrt(_head_sums(k, ones_bd) + NORM_EPS)

    def spread(slab, lo):
        return jnp.concatenate([jnp.broadcast_to(slab[:, lo + j:lo + j + 1], (t, DN_HD))
                                for j in heads], axis=1)

    dl = DN_HEADS
    gc_c = _chunk_cumsum(-jnp.exp(alog) * _softplus(logits + dtb))
    last = [gc_c[(ci + 1) * c - 1:(ci + 1) * c, :] for ci in range(n_chunk)]
    g_last_c = jnp.concatenate([jnp.broadcast_to(r, (c, LANES)) for r in last], axis=0)
    beta = spread(jax.nn.sigmoid(logits), 0)
    gc = spread(gc_c, dl)
    egc = spread(jnp.exp(gc_c), dl)
    kb = k * beta
    vb = v * beta
    kbe = kb * egc
    qd = q * egc
    kd = k * spread(jnp.exp(g_last_c - gc_c), dl)
    c_dec = [jnp.exp(r) for r in last]

    g4_t = gc_c.T
    kd_t = [kd[:, hs[h]].T.astype(BF16) for h in heads]
    fill("norms")

    ri = _iota((t, t), 0)
    ci_ = _iota((t, t), 1)
    same = (ri // c) == (ci_ // c)
    incl = jnp.logical_and(same, ri >= ci_)
    strict = jnp.logical_and(same, ri > ci_)
    neg_inf = jnp.float32(-jnp.inf)

    aq = [_dot_nt(jnp.concatenate([kb[:, hs[h]], q[:, hs[h]]], axis=0), k[:, hs[h]])
          for h in heads]
    lower, attn, sol = [], [], []
    for h in heads:
        g_col = gc[:, hs[h]]
        diff = jnp.concatenate([g_col] * (t // DN_HD), axis=1) - g4_t[dl + h:dl + h + 1, :]
        decay = jnp.exp(jnp.where(incl, diff, neg_inf))
        lower.append(jnp.where(strict, aq[h][:t] * decay, 0.0).astype(BF16))
        attn.append((aq[h][t:] * decay).astype(BF16))
        sol.append(jnp.concatenate([vb[:, hs[h]], kbe[:, hs[h]]], axis=1))
    fill("decay")

    sol = [sol[h] - _dot(lower[h], sol[h]) for h in heads]
    power = [_dot(lower[h], lower[h]).astype(BF16) for h in heads]
    n_sq = 2
    while True:
        sol = [sol[h] + _dot(power[h], sol[h]) for h in heads]
        if 2 * n_sq >= c:
            break
        power = [_dot(power[h], power[h]).astype(BF16) for h in heads]
        n_sq *= 2

    row_chunk = _iota((2 * c, 1), 0) // c
    gn = []
    for h in heads:
        per_chunk = []
        for ci in range(n_chunk):
            pair = slice((ci // 2) * 2 * c, (ci // 2 + 1) * 2 * c)
            rhs = jnp.where(row_chunk == (ci % 2), sol[h][pair], 0.0)
            per_chunk.append(_dot(kd_t[h][:, pair], rhs))
        gn.append(per_chunk)

    def scan_and_outputs():
        state = [state_scr[h] for h in heads]
        starts = [[] for _ in heads]
        for ci in range(n_chunk):
            for h in heads:
                starts[h].append(state[h])
            fill(("scan", ci))
            upd = [_dot(gn[h][ci][:, DN_HD:], state[h]) for h in heads]
            state = [state[h] * c_dec[ci][:, dl + h:dl + h + 1]
                     + (gn[h][ci][:, :DN_HD] - upd[h]) for h in heads]

        fill("outputs")
        xs = [[_dot(jnp.concatenate([qd[ci * c:(ci + 1) * c, hs[h]],
                                     sol[h][ci * c:(ci + 1) * c, DN_HD:]], axis=0), starts[h][ci])
               for ci in range(n_chunk)] for h in heads]
        outs = []
        for h in heads:
            xq = jnp.concatenate([xs[h][ci][:c] for ci in range(n_chunk)], axis=0)
            xw = jnp.concatenate([xs[h][ci][c:] for ci in range(n_chunk)], axis=0)
            outs.append(xq + _dot(attn[h], sol[h][:, :DN_HD] - xw))
        o = jnp.concatenate(outs, axis=1)
        o = (o * lax.rsqrt(_head_sums(o, ones_bd) * (1.0 / DN_HD) + NORM_EPS) * dnorm
             * _silu(get_z()))
        for h in heads:
            state_scr[h] = state[h]
        return o

    return scan_and_outputs


def _mixer_kernel(x_ref, cos_ref, sin_ref, ones_ref, anorm_ref, wfront_ref, wtail_ref,
                  lng_ref, lnb_ref, sguw_ref, sgub_ref, sinks_ref, convw_ref, alog_ref, dtb_ref,
                  dnorm_ref, wbr_ref, wout_ref, o_ref, kv_scr, conv_scr, state_scr, gate_scr,
                  wgate_scr, u_scr, vn_scr, probs_scr):
    seq_start = pl.program_id(1) == 0

    @pl.when(jnp.logical_and(pl.program_id(0) == 0, seq_start))
    def _():
        for n in range(3):
            window = wtail_ref[:, n * D_MODEL:(n + 1) * D_MODEL + GATE_OFF].astype(F32)
            wgate_scr[:, n * D_MODEL:(n + 1) * D_MODEL] = window[:, GATE_OFF:].astype(BF16)

    @pl.when(seq_start)
    def _():
        kv_scr[...] = jnp.zeros_like(kv_scr)
        conv_scr[0:SUBLANES, :] = jnp.zeros((SUBLANES, 3 * MIX), F32)
        state_scr[...] = jnp.zeros_like(state_scr)

    x = x_ref[...]
    h = _rmsnorm(x, anorm_ref[...]).astype(BF16)

    def proj(w_ref, lo, width):
        return jnp.dot(h, w_ref[:, lo:lo + width], preferred_element_type=F32)

    def gate(n, lo=0, width=D_MODEL):
        return jax.nn.sigmoid(proj(wgate_scr, n * D_MODEL + lo, width))

    park = {}

    def fill(stage):
        if stage == "prologue":
            qkv_b = proj(wfront_ref, C_QKVB, MIX + 2 * SWA_KV * SWA_HD)
            park["swa"] = _sliding_window_attention(
                qkv_b[:, :MIX], qkv_b[:, MIX:MIX + LANES], qkv_b[:, MIX + LANES:],
                cos_ref[...], sin_ref[...], sinks_ref, ones_ref[0:LANES, 0:LANES], kv_scr,
                probs_scr, seq_start)
            park["uv"] = proj(wfront_ref, C_UV, 2 * MIX)
            gate_scr[0] = gate(0)
            next(park["swa"])
        elif stage == "norms":
            gate_scr[1] = gate(1)
        elif stage == "decay":
            park["z"] = proj(wfront_ref, C_Z, MIX)
            next(park["swa"])
            uv = park.pop("uv")
            _spatial_gating_prep(uv[:, :MIX], uv[:, MIX:], lng_ref[...], lnb_ref[...],
                                 u_scr, vn_scr)
        elif stage[0] == "scan":
            lo = stage[1] * 2 * LANES
            gate_scr[2, :, lo:lo + 2 * LANES] = gate(2, lo, 2 * LANES)

    dn_tail = _gated_deltanet(proj(wfront_ref, C_QKVC, 3 * MIX), lambda: park.pop("z"),
                              proj(wtail_ref, 0, LANES), convw_ref,
                              alog_ref[...], dtb_ref[...], dnorm_ref[...], ones_ref[...],
                              conv_scr, state_scr, fill)

    out_c = dn_tail()
    out_a = _spatial_gating_mix(u_scr, vn_scr, sguw_ref, sgub_ref)
    out_b = next(park.pop("swa"))
    merged = (gate_scr[0] * _dot(out_a, wbr_ref[0]) + gate_scr[1] * _dot(out_b, wbr_ref[1])
              + gate_scr[2] * _dot(out_c, wbr_ref[2]))
    o_ref[...] = x + _dot(merged, wout_ref[...])


def _ffn_kernel(x_ref, fnorm_ref, wgu_ref, wd_ref, final_ref, o_ref, *, apply_final_norm):
    x = x_ref[...]
    h = _rmsnorm(x, fnorm_ref[...]).astype(BF16)
    gu = jnp.dot(h, wgu_ref[...], preferred_element_type=F32)
    act = _silu(gu[:, :D_FF]) * gu[:, D_FF:]
    y = x + _dot(act, wd_ref[...])
    if apply_final_norm:
        y = _rmsnorm(y, final_ref[...])
    o_ref[...] = y


def _resident(shape_tail, layer):
    n = len(shape_tail)
    return pl.BlockSpec((None,) + tuple(shape_tail), lambda *_: (layer,) + (0,) * n,
                        pipeline_mode=pl.Buffered(1))


def _mixer_call(x, rope, params, layer):
    batch, seq, _ = x.shape
    t = MIXER_TILE
    tok = lambda w: pl.BlockSpec((None, t, w), lambda b, i: (b, i, 0))
    in_specs = [
        tok(D_MODEL), tok(LANES), tok(LANES),
        pl.BlockSpec((MXU_DIM, MXU_DIM), lambda b, i: (0, 0), pipeline_mode=pl.Buffered(1)),
        _resident((1, D_MODEL), layer),
        _resident((D_MODEL, C_FRONT_END), layer),
        _resident((D_MODEL, GATE_OFF + 3 * D_MODEL), layer),
        _resident((1, MIX), layer), _resident((1, MIX), layer),
        _resident((SGU_GROUPS, BLOCK, BLOCK), layer), _resident((BLOCK, SGU_GROUPS), layer),
        _resident((SWA_HEADS, LANES), layer),
        _resident((DN_CONV, 3 * MIX), layer),
        _resident((1, LANES), layer), _resident((1, LANES), layer), _resident((1, MIX), layer),
        _resident((3, MIX, D_MODEL), layer), _resident((D_MODEL, D_MODEL), layer),
    ]
    return pl.pallas_call(
        _mixer_kernel,
        out_shape=jax.ShapeDtypeStruct(x.shape, F32),
        grid=(batch, seq // t),
        in_specs=in_specs,
        out_specs=tok(D_MODEL),
        scratch_shapes=[
            pltpu.VMEM((2, BLOCK, LANES), F32),
            pltpu.VMEM((SUBLANES, 3 * MIX), F32),
            pltpu.VMEM((DN_HEADS, DN_HD, DN_HD), F32),
            pltpu.VMEM((3, t, D_MODEL), F32),
            pltpu.VMEM((D_MODEL, 3 * D_MODEL), BF16),
            pltpu.VMEM((t, MIX), F32),
            pltpu.VMEM((t, MIX), BF16),
            pltpu.VMEM((t // BLOCK * SWA_KV, SWA_HEADS // SWA_KV * BLOCK, 2 * BLOCK), BF16),
        ],
        compiler_params=pltpu.CompilerParams(
            dimension_semantics=("arbitrary", "arbitrary"),
            vmem_limit_bytes=VMEM_LIMIT_BYTES),
        name=f"mixer_l{layer}",
    )(x, *rope, params["head_ones"], params["attn_norm"], params["w_in"],
      params["w_tail"], params["sgu_ln_g"], params["sgu_ln_b"],
      params["sgu_w"], params["sgu_b"], params["attn_sinks"], params["dn_conv_w"],
      params["dn_a_log"], params["dn_dt_bias"], params["dn_norm"], params["w_branch"],
      params["w_out"])


def _ffn_call(x, params, layer, apply_final_norm):
    batch, seq, _ = x.shape
    t = FFN_TILE
    tok = pl.BlockSpec((None, t, D_MODEL), lambda b, i: (b, i, 0))
    return pl.pallas_call(
        functools.partial(_ffn_kernel, apply_final_norm=apply_final_norm),
        out_shape=jax.ShapeDtypeStruct(x.shape, F32),
        grid=(batch, seq // t),
        in_specs=[
            tok,
            _resident((1, D_MODEL), layer),
            _resident((D_MODEL, 2 * D_FF), layer),
            _resident((D_FF, D_MODEL), layer),
            pl.BlockSpec((1, D_MODEL), lambda b, i: (0, 0)),
        ],
        out_specs=tok,
        compiler_params=pltpu.CompilerParams(
            dimension_semantics=("arbitrary", "arbitrary"),
            vmem_limit_bytes=VMEM_LIMIT_BYTES),
        name=f"ffn_l{layer}",
    )(x, params["ffn_norm"], params["w_gate_up"], params["w_down"], params["final_norm"])


def _rope_tables(positions):
    half = ROPE_DIM // 2
    inv_freq = ROPE_THETA ** (-jnp.arange(0, ROPE_DIM, 2, dtype=F32) / ROPE_DIM)
    ang = positions.astype(F32)[..., None] * inv_freq
    cos, sin = lax.optimization_barrier((jnp.cos(ang), jnp.sin(ang)))
    rotary = (jnp.arange(LANES) % SWA_HD) < ROPE_DIM
    reps = (1,) * (ang.ndim - 1) + (LANES // half,)
    return jnp.where(rotary, jnp.tile(cos, reps), 1.0), jnp.where(rotary, jnp.tile(sin, reps), 0.0)


def _pack_params(attn_norm, w_in, sgu_ln_g, sgu_ln_b, sgu_w, sgu_b, attn_sinks, dn_conv_w,
                 dn_a_log, dn_dt_bias, dn_norm, w_branch, w_out, ffn_norm, w_gate_up, w_down,
                 final_norm):
    depth = w_in.shape[0]
    decay_lanes = lambda a: jnp.pad(a, ((0, 0), (DN_HEADS, LANES - 2 * DN_HEADS)))[:, None, :]
    lane_head = jnp.arange(MXU_DIM) // DN_HD
    w_in16 = w_in.astype(BF16)
    return {
        "head_ones": (lane_head[:, None] == lane_head[None, :]).astype(BF16),
        "attn_norm": attn_norm[:, None, :],
        "w_in": w_in16,
        "w_tail": w_in16[:, :, C_FRONT_END:],
        "sgu_ln_g": sgu_ln_g[:, None, :],
        "sgu_ln_b": sgu_ln_b[:, None, :],
        "sgu_w": sgu_w,
        "sgu_b": jnp.swapaxes(sgu_b, 1, 2),
        "attn_sinks": jnp.broadcast_to(attn_sinks[:, :, None], (depth, SWA_HEADS, LANES)),
        "dn_conv_w": dn_conv_w,
        "dn_a_log": decay_lanes(dn_a_log),
        "dn_dt_bias": decay_lanes(dn_dt_bias),
        "dn_norm": jnp.tile(dn_norm, (1, DN_HEADS))[:, None, :],
        "w_branch": w_branch.astype(BF16),
        "w_out": w_out.astype(BF16),
        "ffn_norm": ffn_norm[:, None, :],
        "w_gate_up": w_gate_up.astype(BF16),
        "w_down": w_down.astype(BF16),
        "final_norm": final_norm[None, :],
    }


def kernel(x, positions, attn_norm, w_in, sgu_ln_g, sgu_ln_b, sgu_w, sgu_b, attn_sinks,
           dn_conv_w, dn_a_log, dn_dt_bias, dn_norm, w_branch, w_out, ffn_norm, w_gate_up,
           w_down, final_norm):
    assert MIXER_TILE == MXU_DIM == 4 * DN_CHUNK
    assert x.shape[1] % MIXER_TILE == 0 and x.shape[1] % FFN_TILE == 0
    assert x.shape[2] == D_MODEL
    assert w_in.shape[2] == C_FRONT_END + GATE_OFF + 3 * D_MODEL
    depth = w_in.shape[0]
    params = _pack_params(attn_norm, w_in, sgu_ln_g, sgu_ln_b, sgu_w, sgu_b, attn_sinks,
                          dn_conv_w, dn_a_log, dn_dt_bias, dn_norm, w_branch, w_out, ffn_norm,
                          w_gate_up, w_down, final_norm)
    rope = _rope_tables(positions)
    for layer in range(depth):
        x = _mixer_call(x, rope, params, layer)
        x = _ffn_call(x, params, layer, apply_final_norm=(layer == depth - 1))
    return x
```

```python
import functools

import jax
import jax.numpy as jnp
from jax import lax
from jax.experimental import pallas as pl
from jax.experimental.pallas import tpu as pltpu

D_MODEL = 1024
MIX = 512
NORM_EPS = 1e-6

SGU_GROUPS = 4
BLOCK = 128

SWA_HEADS = 8
SWA_KV = 2
SWA_HD = 64
ROPE_THETA = 500000.0
ROPE_DIM = 16

DN_HEADS = 4
DN_HD = 128
DN_CONV = 4
DN_CHUNK = 64

D_FF = 2816

LANES = 128
SUBLANES = 8
MXU_DIM = 256

C_UV = 0
C_QKVB = C_UV + 2 * MIX
C_QKVC = C_QKVB + MIX + 2 * SWA_KV * SWA_HD
C_Z = C_QKVC + 3 * MIX
C_FRONT_END = C_Z + MIX
GATE_OFF = 2 * DN_HEADS

MIXER_TILE = 256
FFN_TILE = 512
VMEM_CAPACITY_BYTES = 64 * 1024 * 1024
VMEM_LIMIT_BYTES = VMEM_CAPACITY_BYTES // 8 * 7

BF16 = jnp.bfloat16
F32 = jnp.float32
_GELU_C = 0.7978845608028654


def _dot(a, b):
    return jnp.dot(a.astype(BF16), b.astype(BF16), preferred_element_type=F32)


def _dot_nt(a, b):
    return lax.dot_general(a.astype(BF16), b.astype(BF16), (((1,), (1,)), ((), ())),
                           preferred_element_type=F32)


def _rmsnorm(x, g):
    return x * lax.rsqrt(jnp.mean(x * x, axis=-1, keepdims=True) + NORM_EPS) * g


def _gelu_tanh(x):
    inner = x * (_GELU_C + (_GELU_C * 0.044715) * (x * x))
    return (0.5 * x) * (1.0 + jnp.tanh(inner))


def _silu(x):
    return x * jax.nn.sigmoid(x)


def _softplus(x):
    return jnp.maximum(x, 0.0) + jnp.log(1.0 + jnp.exp(-jnp.abs(x)))


def _iota(shape, dim):
    return lax.broadcasted_iota(jnp.int32, shape, dim)


def _spatial_gating_prep(u_pre, v_pre, ln_g, ln_b, u_scr, vn_scr):
    u_scr[...] = _gelu_tanh(u_pre)
    v = _gelu_tanh(v_pre)
    vc = v - jnp.mean(v, axis=-1, keepdims=True)
    vn_scr[...] = (vc * lax.rsqrt(jnp.mean(vc * vc, axis=-1, keepdims=True) + NORM_EPS) * ln_g
                   + ln_b).astype(BF16)


def _spatial_gating_mix(u_scr, vn_scr, sguw_ref, sgub_ref):
    u = u_scr[...]
    vn = vn_scr[...]
    t = u.shape[0]
    causal = _iota((BLOCK, BLOCK), 0) >= _iota((BLOCK, BLOCK), 1)
    groups = []
    for g in range(SGU_GROUPS):
        w = jnp.where(causal, sguw_ref[g], 0.0).astype(BF16)
        bias = sgub_ref[:, g:g + 1]
        cols = slice(g * LANES, (g + 1) * LANES)
        mixed = [_dot(w, vn[blk * BLOCK:(blk + 1) * BLOCK, cols]) + bias
                 for blk in range(t // BLOCK)]
        groups.append(u[:, cols] * jnp.concatenate(mixed, axis=0))
    return jnp.concatenate(groups, axis=1)


def _rope(x, cos, sin):
    half = ROPE_DIM // 2
    first_half = (_iota((1, LANES), 1) % SWA_HD) < half
    sin_a = jnp.where(first_half, -sin, 0.0)
    sin_b = jnp.where(first_half, 0.0, sin)
    n = x.shape[1] // LANES
    if n > 1:
        cos = jnp.concatenate([cos] * n, axis=1)
        sin_a = jnp.concatenate([sin_a] * n, axis=1)
        sin_b = jnp.concatenate([sin_b] * n, axis=1)
    width = x.shape[1]
    return (x * cos + pltpu.roll(x, width - half, 1) * sin_a + pltpu.roll(x, half, 1) * sin_b)


def _sliding_window_attention(q, k, v, cos, sin, sinks_ref, ones, kv_scr, probs_scr, seq_start):
    t = q.shape[0]
    n_blk = t // BLOCK
    group = SWA_HEADS // SWA_KV
    q = _rope(q, cos, sin) * (SWA_HD ** -0.5)
    k = _rope(k, cos, sin)

    kcat = jnp.concatenate([kv_scr[0], k], axis=0)
    vcat = jnp.concatenate([kv_scr[1], v], axis=0)

    lane = _iota((1, LANES), 1)
    lo = lane < SWA_HD
    krot = pltpu.roll(kcat, SWA_HD, 1)
    vrot = pltpu.roll(vcat, SWA_HD, 1)
    kdup = [jnp.where(lo, kcat, krot).astype(BF16), jnp.where(lo, krot, kcat).astype(BF16)]
    vlo = [jnp.where(lo, vcat, 0.0).astype(BF16), jnp.where(lo, vrot, 0.0).astype(BF16)]
    vhi = [jnp.where(lo, 0.0, vrot).astype(BF16), jnp.where(lo, 0.0, vcat).astype(BF16)]

    rows4 = group * BLOCK
    cur_mask = (_iota((rows4, BLOCK), 0) % BLOCK) >= _iota((rows4, BLOCK), 1)
    neg_inf = jnp.float32(-jnp.inf)
    units = [(blk, kh) for blk in range(n_blk) for kh in range(SWA_KV)]

    lhs = []
    for blk, kh in units:
        rows = slice(blk * BLOCK, (blk + 1) * BLOCK)
        s0 = q[rows, (2 * kh) * LANES:(2 * kh + 1) * LANES]
        s1 = q[rows, (2 * kh + 1) * LANES:(2 * kh + 2) * LANES]
        lhs.append(jnp.concatenate([jnp.where(lo, s0, 0.0), jnp.where(lo, s1, 0.0),
                                    jnp.where(lo, 0.0, s0), jnp.where(lo, 0.0, s1)],
                                   axis=0).astype(BF16))
    logits = [_dot_nt(lhs[i], kdup[kh][blk * BLOCK:(blk + 2) * BLOCK])
              for i, (blk, kh) in enumerate(units)]
    yield
    n_unit = len(units)
    sinks = [jnp.concatenate(
        [jnp.broadcast_to(sinks_ref[group * kh + 2 * j + half:group * kh + 2 * j + half + 1, :],
                          (BLOCK, LANES)) for half in range(2) for j in range(2)], axis=0)
        for kh in range(SWA_KV)]
    act = []
    for i, (blk, kh) in enumerate(units):
        l_prev = logits[i][:, :BLOCK]
        if blk == 0:
            l_prev = jnp.where(seq_start, neg_inf, l_prev)
        act.append(jnp.where(cur_mask, logits[i][:, BLOCK:], l_prev))
    m = [jnp.maximum(jnp.broadcast_to(jnp.max(act[i], axis=1, keepdims=True), act[i].shape),
                     sinks[units[i][1]]) for i in range(n_unit)]
    p = [jnp.exp(act[i] - m[i]) for i in range(n_unit)]
    rows_u = p[0].shape[0]
    sums = _dot(jnp.concatenate(p, axis=0), ones)
    den = [sums[i * rows_u:(i + 1) * rows_u] + jnp.exp(sinks[units[i][1]] - m[i])
           for i in range(n_unit)]
    for i in range(n_unit):
        pn = p[i] * (1.0 / den[i])
        probs_scr[i] = jnp.concatenate([jnp.where(cur_mask, 0.0, pn), jnp.where(cur_mask, pn, 0.0)],
                                       axis=1).astype(BF16)
    yield
    probs = [probs_scr[i] for i in range(len(units))]
    out_rows = []
    for blk in range(n_blk):
        slabs = []
        for kh in range(SWA_KV):
            i = blk * SWA_KV + kh
            keys = slice(blk * BLOCK, (blk + 2) * BLOCK)
            res = (_dot(probs[i][:2 * BLOCK], vlo[kh][keys])
                   + _dot(probs[i][2 * BLOCK:], vhi[kh][keys]))
            slabs += [res[:BLOCK], res[BLOCK:]]
        out_rows.append(jnp.concatenate(slabs, axis=1))
    kv_scr[0] = k[t - BLOCK:, :]
    kv_scr[1] = v[t - BLOCK:, :]
    yield jnp.concatenate(out_rows, axis=0)


def _chunk_cumsum(x):
    row = _iota(x.shape, 0) % DN_CHUNK
    shift = 1
    while shift < DN_CHUNK:
        x = x + jnp.where(row >= shift, pltpu.roll(x, shift, 0), 0.0)
        shift *= 2
    return x


def _head_sums(xs, ones_bd):
    t = xs[0].shape[0]
    n_half = xs[0].shape[1] // MXU_DIM
    stacked = jnp.concatenate([(x * x)[:, i * MXU_DIM:(i + 1) * MXU_DIM]
                               for x in xs for i in range(n_half)], axis=0)
    sums = _dot(stacked, ones_bd)
    return [jnp.concatenate([sums[(j * n_half + i) * t:(j * n_half + i + 1) * t]
                             for i in range(n_half)], axis=1) for j in range(len(xs))]


def _gated_deltanet(qkv, get_z, logits, convw_ref, alog, dtb, dnorm, ones_bd,
                    conv_scr, state_scr, fill):
    t = qkv.shape[0]
    c = DN_CHUNK
    n_chunk = t // c
    pad = SUBLANES
    heads = range(DN_HEADS)
    hs = [slice(h * DN_HD, (h + 1) * DN_HD) for h in heads]

    xa = jnp.concatenate([conv_scr[0:pad, :], qkv], axis=0)
    y = qkv * convw_ref[DN_CONV - 1:DN_CONV, :]
    for i in range(DN_CONV - 1):
        y = y + pltpu.roll(xa, DN_CONV - 1 - i, 0)[pad:, :] * convw_ref[i:i + 1, :]
    conv_scr[0:pad, :] = qkv[t - pad:, :]
    y = _silu(y)

    q = y[:, :MIX]
    k = y[:, MIX:2 * MIX]
    v = y[:, 2 * MIX:]
    fill("prologue")
    q_ss, k_ss = _head_sums([q, k], ones_bd)
    q = q * lax.rsqrt(q_ss + NORM_EPS) * (DN_HD ** -0.5)
    k = k * lax.rsqrt(k_ss + NORM_EPS)

    def spread(slab, lo):
        return jnp.concatenate([jnp.broadcast_to(slab[:, lo + j:lo + j + 1], (t, DN_HD))
                                for j in heads], axis=1)

    dl = DN_HEADS
    gc_c = _chunk_cumsum(-jnp.exp(alog) * _softplus(logits + dtb))
    last = [gc_c[(ci + 1) * c - 1:(ci + 1) * c, :] for ci in range(n_chunk)]
    g_last_c = jnp.concatenate([jnp.broadcast_to(r, (c, LANES)) for r in last], axis=0)
    beta = spread(jax.nn.sigmoid(logits), 0)
    gc = spread(gc_c, dl)
    egc = spread(jnp.exp(gc_c), dl)
    kb = k * beta
    vb = v * beta
    kbe = kb * egc
    qd = q * egc
    kd = k * spread(jnp.exp(g_last_c - gc_c), dl)
    c_dec = [jnp.exp(r) for r in last]

    g4_t = gc_c.T
    kd_t = [kd[:, hs[h]].T.astype(BF16) for h in heads]
    fill("norms")

    ri = _iota((t, t), 0)
    ci_ = _iota((t, t), 1)
    same = (ri // c) == (ci_ // c)
    incl = jnp.logical_and(same, ri >= ci_)
    strict = jnp.logical_and(same, ri > ci_)
    neg_inf = jnp.float32(-jnp.inf)

    aq = [_dot_nt(jnp.concatenate([kb[:, hs[h]], q[:, hs[h]]], axis=0), k[:, hs[h]])
          for h in heads]
    lower, attn, sol = [], [], []
    for h in heads:
        g_col = gc[:, hs[h]]
        diff = jnp.concatenate([g_col] * (t // DN_HD), axis=1) - g4_t[dl + h:dl + h + 1, :]
        decay = jnp.exp(jnp.where(incl, diff, neg_inf))
        lower.append(jnp.where(strict, aq[h][:t] * decay, 0.0).astype(BF16))
        attn.append((aq[h][t:] * decay).astype(BF16))
        sol.append(jnp.concatenate([vb[:, hs[h]], kbe[:, hs[h]]], axis=1))
    fill("decay")

    sol = [sol[h] - _dot(lower[h], sol[h]) for h in heads]
    power = [_dot(lower[h], lower[h]).astype(BF16) for h in heads]
    n_sq = 2
    while True:
        sol = [sol[h] + _dot(power[h], sol[h]) for h in heads]
        if 2 * n_sq >= c:
            break
        power = [_dot(power[h], power[h]).astype(BF16) for h in heads]
        n_sq *= 2

    row_chunk = _iota((2 * c, 1), 0) // c
    gn = []
    for h in heads:
        per_chunk = []
        for ci in range(n_chunk):
            pair = slice((ci // 2) * 2 * c, (ci // 2 + 1) * 2 * c)
            rhs = jnp.where(row_chunk == (ci % 2), sol[h][pair], 0.0)
            per_chunk.append(_dot(kd_t[h][:, pair], rhs))
        gn.append(per_chunk)

    au = [_dot(attn[h], sol[h]) for h in heads]
    q_eff = [qd[:, hs[h]] - au[h][:, DN_HD:] for h in heads]

    def scan_and_outputs():
        state = [state_scr[h] for h in heads]
        from_state = [[] for _ in heads]
        for ci in range(n_chunk):
            rows = slice(ci * c, (ci + 1) * c)
            fill(("scan", ci))
            both = [_dot(jnp.concatenate([q_eff[h][rows], gn[h][ci][:, DN_HD:]], axis=0), state[h])
                    for h in heads]
            for h in heads:
                from_state[h].append(both[h][:c])
            state = [state[h] * c_dec[ci][:, dl + h:dl + h + 1]
                     + (gn[h][ci][:, :DN_HD] - both[h][c:]) for h in heads]

        outs = [jnp.concatenate(from_state[h], axis=0) + au[h][:, :DN_HD] for h in heads]
        o = jnp.concatenate(outs, axis=1)
        o = (o * lax.rsqrt(_head_sums([o], ones_bd)[0] * (1.0 / DN_HD) + NORM_EPS) * dnorm
             * _silu(get_z()))
        for h in heads:
            state_scr[h] = state[h]
        return o

    return scan_and_outputs


def _mixer_kernel(x_ref, cos_ref, sin_ref, ones_ref, anorm_ref, wfront_ref, wtail_ref,
                  lng_ref, lnb_ref, sguw_ref, sgub_ref, sinks_ref, convw_ref, alog_ref, dtb_ref,
                  dnorm_ref, wbr_ref, wout_ref, o_ref, kv_scr, conv_scr, state_scr, gate_scr,
                  wgate_scr, u_scr, vn_scr, probs_scr):
    seq_start = pl.program_id(1) == 0

    @pl.when(jnp.logical_and(pl.program_id(0) == 0, seq_start))
    def _():
        for n in range(3):
            window = wtail_ref[:, n * D_MODEL:(n + 1) * D_MODEL + GATE_OFF].astype(F32)
            wgate_scr[:, n * D_MODEL:(n + 1) * D_MODEL] = window[:, GATE_OFF:].astype(BF16)

    @pl.when(seq_start)
    def _():
        kv_scr[...] = jnp.zeros_like(kv_scr)
        conv_scr[0:SUBLANES, :] = jnp.zeros((SUBLANES, 3 * MIX), F32)
        state_scr[...] = jnp.zeros_like(state_scr)

    x = x_ref[...]
    h = _rmsnorm(x, anorm_ref[...]).astype(BF16)

    def proj(w_ref, lo, width):
        return jnp.dot(h, w_ref[:, lo:lo + width], preferred_element_type=F32)

    def gate(n, lo=0, width=D_MODEL):
        return jax.nn.sigmoid(proj(wgate_scr, n * D_MODEL + lo, width))

    park = {}

    def fill(stage):
        if stage == "prologue":
            qkv_b = proj(wfront_ref, C_QKVB, MIX + 2 * SWA_KV * SWA_HD)
            park["swa"] = _sliding_window_attention(
                qkv_b[:, :MIX], qkv_b[:, MIX:MIX + LANES], qkv_b[:, MIX + LANES:],
                cos_ref[...], sin_ref[...], sinks_ref, ones_ref[0:LANES, 0:LANES], kv_scr,
                probs_scr, seq_start)
            park["uv"] = proj(wfront_ref, C_UV, 2 * MIX)
            gate_scr[0] = gate(0)
            next(park["swa"])
        elif stage == "norms":
            gate_scr[1] = gate(1)
        elif stage == "decay":
            park["z"] = proj(wfront_ref, C_Z, MIX)
            next(park["swa"])
            uv = park.pop("uv")
            _spatial_gating_prep(uv[:, :MIX], uv[:, MIX:], lng_ref[...], lnb_ref[...],
                                 u_scr, vn_scr)
        elif stage[0] == "scan":
            lo = stage[1] * 2 * LANES
            gate_scr[2, :, lo:lo + 2 * LANES] = gate(2, lo, 2 * LANES)

    dn_tail = _gated_deltanet(proj(wfront_ref, C_QKVC, 3 * MIX), lambda: park.pop("z"),
                              proj(wtail_ref, 0, LANES), convw_ref,
                              alog_ref[...], dtb_ref[...], dnorm_ref[...], ones_ref[...],
                              conv_scr, state_scr, fill)

    out_c = dn_tail()
    out_a = _spatial_gating_mix(u_scr, vn_scr, sguw_ref, sgub_ref)
    out_b = next(park.pop("swa"))
    merged = (gate_scr[0] * _dot(out_a, wbr_ref[0]) + gate_scr[1] * _dot(out_b, wbr_ref[1])
              + gate_scr[2] * _dot(out_c, wbr_ref[2]))
    o_ref[...] = x + _dot(merged, wout_ref[...])


def _ffn_kernel(x_ref, fnorm_ref, wgu_ref, wd_ref, final_ref, o_ref, *, apply_final_norm):
    x = x_ref[...]
    h = _rmsnorm(x, fnorm_ref[...]).astype(BF16)
    gu = jnp.dot(h, wgu_ref[...], preferred_element_type=F32)
    act = _silu(gu[:, :D_FF]) * gu[:, D_FF:]
    y = x + _dot(act, wd_ref[...])
    if apply_final_norm:
        y = _rmsnorm(y, final_ref[...])
    o_ref[...] = y


def _resident(shape_tail, layer):
    n = len(shape_tail)
    return pl.BlockSpec((None,) + tuple(shape_tail), lambda *_: (layer,) + (0,) * n,
                        pipeline_mode=pl.Buffered(1))


def _mixer_call(x, rope, params, layer):
    batch, seq, _ = x.shape
    t = MIXER_TILE
    tok = lambda w: pl.BlockSpec((None, t, w), lambda b, i: (b, i, 0))
    in_specs = [
        tok(D_MODEL), tok(LANES), tok(LANES),
        pl.BlockSpec((MXU_DIM, MXU_DIM), lambda b, i: (0, 0), pipeline_mode=pl.Buffered(1)),
        _resident((1, D_MODEL), layer),
        _resident((D_MODEL, C_FRONT_END), layer),
        _resident((D_MODEL, GATE_OFF + 3 * D_MODEL), layer),
        _resident((1, MIX), layer), _resident((1, MIX), layer),
        _resident((SGU_GROUPS, BLOCK, BLOCK), layer), _resident((BLOCK, SGU_GROUPS), layer),
        _resident((SWA_HEADS, LANES), layer),
        _resident((DN_CONV, 3 * MIX), layer),
        _resident((1, LANES), layer), _resident((1, LANES), layer), _resident((1, MIX), layer),
        _resident((3, MIX, D_MODEL), layer), _resident((D_MODEL, D_MODEL), layer),
    ]
    return pl.pallas_call(
        _mixer_kernel,
        out_shape=jax.ShapeDtypeStruct(x.shape, F32),
        grid=(batch, seq // t),
        in_specs=in_specs,
        out_specs=tok(D_MODEL),
        scratch_shapes=[
            pltpu.VMEM((2, BLOCK, LANES), F32),
            pltpu.VMEM((SUBLANES, 3 * MIX), F32),
            pltpu.VMEM((DN_HEADS, DN_HD, DN_HD), F32),
            pltpu.VMEM((3, t, D_MODEL), F32),
            pltpu.VMEM((D_MODEL, 3 * D_MODEL), BF16),
            pltpu.VMEM((t, MIX), F32),
            pltpu.VMEM((t, MIX), BF16),
            pltpu.VMEM((t // BLOCK * SWA_KV, SWA_HEADS // SWA_KV * BLOCK, 2 * BLOCK), BF16),
        ],
        compiler_params=pltpu.CompilerParams(
            dimension_semantics=("arbitrary", "arbitrary"),
            vmem_limit_bytes=VMEM_LIMIT_BYTES),
        name=f"mixer_l{layer}",
    )(x, *rope, params["head_ones"], params["attn_norm"], params["w_in"],
      params["w_tail"], params["sgu_ln_g"], params["sgu_ln_b"],
      params["sgu_w"], params["sgu_b"], params["attn_sinks"], params["dn_conv_w"],
      params["dn_a_log"], params["dn_dt_bias"], params["dn_norm"], params["w_branch"],
      params["w_out"])


def _ffn_call(x, params, layer, apply_final_norm):
    batch, seq, _ = x.shape
    t = FFN_TILE
    tok = pl.BlockSpec((None, t, D_MODEL), lambda b, i: (b, i, 0))
    return pl.pallas_call(
        functools.partial(_ffn_kernel, apply_final_norm=apply_final_norm),
        out_shape=jax.ShapeDtypeStruct(x.shape, F32),
        grid=(batch, seq // t),
        in_specs=[
            tok,
            _resident((1, D_MODEL), layer),
            _resident((D_MODEL, 2 * D_FF), layer),
            _resident((D_FF, D_MODEL), layer),
            pl.BlockSpec((1, D_MODEL), lambda b, i: (0, 0)),
        ],
        out_specs=tok,
        compiler_params=pltpu.CompilerParams(
            dimension_semantics=("arbitrary", "arbitrary"),
            vmem_limit_bytes=VMEM_LIMIT_BYTES),
        name=f"ffn_l{layer}",
    )(x, params["ffn_norm"], params["w_gate_up"], params["w_down"], params["final_norm"])


def _rope_tables(positions):
    half = ROPE_DIM // 2
    inv_freq = ROPE_THETA ** (-jnp.arange(0, ROPE_DIM, 2, dtype=F32) / ROPE_DIM)
    ang = positions.astype(F32)[..., None] * inv_freq
    cos, sin = lax.optimization_barrier((jnp.cos(ang), jnp.sin(ang)))
    rotary = (jnp.arange(LANES) % SWA_HD) < ROPE_DIM
    reps = (1,) * (ang.ndim - 1) + (LANES // half,)
    return jnp.where(rotary, jnp.tile(cos, reps), 1.0), jnp.where(rotary, jnp.tile(sin, reps), 0.0)


def _pack_params(attn_norm, w_in, sgu_ln_g, sgu_ln_b, sgu_w, sgu_b, attn_sinks, dn_conv_w,
                 dn_a_log, dn_dt_bias, dn_norm, w_branch, w_out, ffn_norm, w_gate_up, w_down,
                 final_norm):
    depth = w_in.shape[0]
    decay_lanes = lambda a: jnp.pad(a, ((0, 0), (DN_HEADS, LANES - 2 * DN_HEADS)))[:, None, :]
    lane_head = jnp.arange(MXU_DIM) // DN_HD
    w_in16 = w_in.astype(BF16)
    return {
        "head_ones": (lane_head[:, None] == lane_head[None, :]).astype(BF16),
        "attn_norm": attn_norm[:, None, :],
        "w_in": w_in16,
        "w_tail": w_in16[:, :, C_FRONT_END:],
        "sgu_ln_g": sgu_ln_g[:, None, :],
        "sgu_ln_b": sgu_ln_b[:, None, :],
        "sgu_w": sgu_w,
        "sgu_b": jnp.swapaxes(sgu_b, 1, 2),
        "attn_sinks": jnp.broadcast_to(attn_sinks[:, :, None], (depth, SWA_HEADS, LANES)),
        "dn_conv_w": dn_conv_w,
        "dn_a_log": decay_lanes(dn_a_log),
        "dn_dt_bias": decay_lanes(dn_dt_bias),
        "dn_norm": jnp.tile(dn_norm, (1, DN_HEADS))[:, None, :],
        "w_branch": w_branch.astype(BF16),
        "w_out": w_out.astype(BF16),
        "ffn_norm": ffn_norm[:, None, :],
        "w_gate_up": w_gate_up.astype(BF16),
        "w_down": w_down.astype(BF16),
        "final_norm": final_norm[None, :],
    }


def kernel(x, positions, attn_norm, w_in, sgu_ln_g, sgu_ln_b, sgu_w, sgu_b, attn_sinks,
           dn_conv_w, dn_a_log, dn_dt_bias, dn_norm, w_branch, w_out, ffn_norm, w_gate_up,
           w_down, final_norm):
    assert MIXER_TILE == MXU_DIM == 4 * DN_CHUNK
    assert x.shape[1] % MIXER_TILE == 0 and x.shape[1] % FFN_TILE == 0
    assert x.shape[2] == D_MODEL
    assert w_in.shape[2] == C_FRONT_END + GATE_OFF + 3 * D_MODEL
    depth = w_in.shape[0]
    params = _pack_params(attn_norm, w_in, sgu_ln_g, sgu_ln_b, sgu_w, sgu_b, attn_sinks,
                          dn_conv_w, dn_a_log, dn_dt_bias, dn_norm, w_branch, w_out, ffn_norm,
                          w_gate_up, w_down, final_norm)
    rope = _rope_tables(positions)
    for layer in range(depth):
        x = _mixer_call(x, rope, params, layer)
        x = _ffn_call(x, params, layer, apply_final_norm=(layer == depth - 1))
    return x
```

```python
import functools

import jax
import jax.numpy as jnp
from jax import lax
from jax.experimental import pallas as pl
from jax.experimental.pallas import tpu as pltpu

D_MODEL = 1024
MIX = 512
NORM_EPS = 1e-6

SGU_GROUPS = 4
BLOCK = 128

SWA_HEADS = 8
SWA_KV = 2
SWA_HD = 64
ROPE_THETA = 500000.0
ROPE_DIM = 16

DN_HEADS = 4
DN_HD = 128
DN_CONV = 4
DN_CHUNK = 64

D_FF = 2816

LANES = 128
SUBLANES = 8
MXU_DIM = 256

C_UV = 0
C_QKVB = C_UV + 2 * MIX
C_QKVC = C_QKVB + MIX + 2 * SWA_KV * SWA_HD
C_Z = C_QKVC + 3 * MIX
C_FRONT_END = C_Z + MIX
GATE_OFF = 2 * DN_HEADS

MIXER_TILE = 256
FFN_TILE = 512
WD_CAST_ROWS = 64
VMEM_CAPACITY_BYTES = 64 * 1024 * 1024
VMEM_LIMIT_BYTES = VMEM_CAPACITY_BYTES // 8 * 7

BF16 = jnp.bfloat16
F32 = jnp.float32
_GELU_C = 0.7978845608028654


def _dot(a, b):
    return jnp.dot(a.astype(BF16), b.astype(BF16), preferred_element_type=F32)


def _dot_nt(a, b):
    return lax.dot_general(a.astype(BF16), b.astype(BF16), (((1,), (1,)), ((), ())),
                           preferred_element_type=F32)


def _rmsnorm(x, g):
    return x * lax.rsqrt(jnp.mean(x * x, axis=-1, keepdims=True) + NORM_EPS) * g


def _gelu_tanh(x):
    inner = x * (_GELU_C + (_GELU_C * 0.044715) * (x * x))
    return (0.5 * x) * (1.0 + jnp.tanh(inner))


def _silu(x):
    return x * jax.nn.sigmoid(x)


def _softplus(x):
    return jnp.maximum(x, 0.0) + jnp.log(1.0 + jnp.exp(-jnp.abs(x)))


def _iota(shape, dim):
    return lax.broadcasted_iota(jnp.int32, shape, dim)


def _spatial_gating_prep(u_pre, v_pre, ln_g, ln_b, u_scr, vn_scr):
    u_scr[...] = _gelu_tanh(u_pre)
    v = _gelu_tanh(v_pre)
    vc = v - jnp.mean(v, axis=-1, keepdims=True)
    vn_scr[...] = (vc * lax.rsqrt(jnp.mean(vc * vc, axis=-1, keepdims=True) + NORM_EPS) * ln_g
                   + ln_b).astype(BF16)


def _spatial_gating_mix(u_scr, vn_scr, sguw_ref, sgub_ref):
    u = u_scr[...]
    vn = vn_scr[...]
    t = u.shape[0]
    causal = _iota((BLOCK, BLOCK), 0) >= _iota((BLOCK, BLOCK), 1)
    groups = []
    for g in range(SGU_GROUPS):
        w = jnp.where(causal, sguw_ref[g], 0.0).astype(BF16)
        bias = sgub_ref[:, g:g + 1]
        cols = slice(g * LANES, (g + 1) * LANES)
        mixed = [_dot(w, vn[blk * BLOCK:(blk + 1) * BLOCK, cols]) + bias
                 for blk in range(t // BLOCK)]
        groups.append(u[:, cols] * jnp.concatenate(mixed, axis=0))
    return jnp.concatenate(groups, axis=1)


def _rope(x, cos, sin):
    half = ROPE_DIM // 2
    first_half = (_iota((1, LANES), 1) % SWA_HD) < half
    sin_a = jnp.where(first_half, -sin, 0.0)
    sin_b = jnp.where(first_half, 0.0, sin)
    n = x.shape[1] // LANES
    if n > 1:
        cos = jnp.concatenate([cos] * n, axis=1)
        sin_a = jnp.concatenate([sin_a] * n, axis=1)
        sin_b = jnp.concatenate([sin_b] * n, axis=1)
    width = x.shape[1]
    return (x * cos + pltpu.roll(x, width - half, 1) * sin_a + pltpu.roll(x, half, 1) * sin_b)


def _sliding_window_attention(q, k, v, cos, sin, sinks_ref, ones, kv_scr, probs_scr, seq_start):
    t = q.shape[0]
    n_blk = t // BLOCK
    group = SWA_HEADS // SWA_KV
    q = _rope(q, cos, sin) * (SWA_HD ** -0.5)
    k = _rope(k, cos, sin)

    kcat = jnp.concatenate([kv_scr[0], k], axis=0)
    vcat = jnp.concatenate([kv_scr[1], v], axis=0)

    lane = _iota((1, LANES), 1)
    lo = lane < SWA_HD
    krot = pltpu.roll(kcat, SWA_HD, 1)
    vrot = pltpu.roll(vcat, SWA_HD, 1)
    kdup = [jnp.where(lo, kcat, krot).astype(BF16), jnp.where(lo, krot, kcat).astype(BF16)]
    vlo = [jnp.where(lo, vcat, 0.0).astype(BF16), jnp.where(lo, vrot, 0.0).astype(BF16)]
    vhi = [jnp.where(lo, 0.0, vrot).astype(BF16), jnp.where(lo, 0.0, vcat).astype(BF16)]

    rows4 = group * BLOCK
    cur_mask = (_iota((rows4, BLOCK), 0) % BLOCK) >= _iota((rows4, BLOCK), 1)
    neg_inf = jnp.float32(-jnp.inf)
    units = [(blk, kh) for blk in range(n_blk) for kh in range(SWA_KV)]

    lhs = []
    for blk, kh in units:
        rows = slice(blk * BLOCK, (blk + 1) * BLOCK)
        s0 = q[rows, (2 * kh) * LANES:(2 * kh + 1) * LANES]
        s1 = q[rows, (2 * kh + 1) * LANES:(2 * kh + 2) * LANES]
        lhs.append(jnp.concatenate([jnp.where(lo, s0, 0.0), jnp.where(lo, s1, 0.0),
                                    jnp.where(lo, 0.0, s0), jnp.where(lo, 0.0, s1)],
                                   axis=0).astype(BF16))
    logits = [_dot_nt(lhs[i], kdup[kh][blk * BLOCK:(blk + 2) * BLOCK])
              for i, (blk, kh) in enumerate(units)]
    yield
    n_unit = len(units)
    sinks = [jnp.concatenate(
        [jnp.broadcast_to(sinks_ref[group * kh + 2 * j + half:group * kh + 2 * j + half + 1, :],
                          (BLOCK, LANES)) for half in range(2) for j in range(2)], axis=0)
        for kh in range(SWA_KV)]
    act = []
    for i, (blk, kh) in enumerate(units):
        l_prev = logits[i][:, :BLOCK]
        if blk == 0:
            l_prev = jnp.where(seq_start, neg_inf, l_prev)
        act.append(jnp.where(cur_mask, logits[i][:, BLOCK:], l_prev))
    m = [jnp.maximum(jnp.broadcast_to(jnp.max(act[i], axis=1, keepdims=True), act[i].shape),
                     sinks[units[i][1]]) for i in range(n_unit)]
    p = [jnp.exp(act[i] - m[i]) for i in range(n_unit)]
    rows_u = p[0].shape[0]
    sums = _dot(jnp.concatenate(p, axis=0), ones)
    den = [sums[i * rows_u:(i + 1) * rows_u] + jnp.exp(sinks[units[i][1]] - m[i])
           for i in range(n_unit)]
    for i in range(n_unit):
        pn = p[i] * (1.0 / den[i])
        probs_scr[i] = jnp.concatenate([jnp.where(cur_mask, 0.0, pn), jnp.where(cur_mask, pn, 0.0)],
                                       axis=1).astype(BF16)
    yield
    probs = [probs_scr[i] for i in range(len(units))]
    out_rows = []
    for blk in range(n_blk):
        slabs = []
        for kh in range(SWA_KV):
            i = blk * SWA_KV + kh
            keys = slice(blk * BLOCK, (blk + 2) * BLOCK)
            res = (_dot(probs[i][:2 * BLOCK], vlo[kh][keys])
                   + _dot(probs[i][2 * BLOCK:], vhi[kh][keys]))
            slabs += [res[:BLOCK], res[BLOCK:]]
        out_rows.append(jnp.concatenate(slabs, axis=1))
    kv_scr[0] = k[t - BLOCK:, :]
    kv_scr[1] = v[t - BLOCK:, :]
    yield jnp.concatenate(out_rows, axis=0)


def _chunk_cumsum(x):
    row = _iota(x.shape, 0) % DN_CHUNK
    shift = 1
    while shift < DN_CHUNK:
        x = x + jnp.where(row >= shift, pltpu.roll(x, shift, 0), 0.0)
        shift *= 2
    return x


def _head_sums(xs, ones_bd):
    t = xs[0].shape[0]
    n_half = xs[0].shape[1] // MXU_DIM
    stacked = jnp.concatenate([(x * x)[:, i * MXU_DIM:(i + 1) * MXU_DIM]
                               for x in xs for i in range(n_half)], axis=0)
    sums = _dot(stacked, ones_bd)
    return [jnp.concatenate([sums[(j * n_half + i) * t:(j * n_half + i + 1) * t]
                             for i in range(n_half)], axis=1) for j in range(len(xs))]


def _gated_deltanet(qkv, get_z, logits, convw_ref, alog, dtb, dnorm, ones_bd,
                    conv_scr, state_scr, fill):
    t = qkv.shape[0]
    c = DN_CHUNK
    n_chunk = t // c
    pad = SUBLANES
    heads = range(DN_HEADS)
    hs = [slice(h * DN_HD, (h + 1) * DN_HD) for h in heads]

    xa = jnp.concatenate([conv_scr[0:pad, :], qkv], axis=0)
    y = qkv * convw_ref[DN_CONV - 1:DN_CONV, :]
    for i in range(DN_CONV - 1):
        y = y + pltpu.roll(xa, DN_CONV - 1 - i, 0)[pad:, :] * convw_ref[i:i + 1, :]
    conv_scr[0:pad, :] = qkv[t - pad:, :]
    y = _silu(y)

    q = y[:, :MIX]
    k = y[:, MIX:2 * MIX]
    v = y[:, 2 * MIX:]
    fill("prologue")
    q_ss, k_ss = _head_sums([q, k], ones_bd)
    q = q * lax.rsqrt(q_ss + NORM_EPS) * (DN_HD ** -0.5)
    k = k * lax.rsqrt(k_ss + NORM_EPS)

    def spread(slab, lo):
        return jnp.concatenate([jnp.broadcast_to(slab[:, lo + j:lo + j + 1], (t, DN_HD))
                                for j in heads], axis=1)

    dl = DN_HEADS
    gc_c = _chunk_cumsum(-jnp.exp(alog) * _softplus(logits + dtb))
    last = [gc_c[(ci + 1) * c - 1:(ci + 1) * c, :] for ci in range(n_chunk)]
    g_last_c = jnp.concatenate([jnp.broadcast_to(r, (c, LANES)) for r in last], axis=0)
    beta = spread(jax.nn.sigmoid(logits), 0)
    gc = spread(gc_c, dl)
    egc = spread(jnp.exp(gc_c), dl)
    kb = k * beta
    vb = v * beta
    kbe = kb * egc
    qd = q * egc
    kd = k * spread(jnp.exp(g_last_c - gc_c), dl)
    c_dec = [jnp.exp(r) for r in last]

    g4_t = gc_c.T
    kd_t = [kd[:, hs[h]].T.astype(BF16) for h in heads]
    fill("norms")

    ri = _iota((t, t), 0)
    ci_ = _iota((t, t), 1)
    same = (ri // c) == (ci_ // c)
    incl = jnp.logical_and(same, ri >= ci_)
    strict = jnp.logical_and(same, ri > ci_)
    neg_inf = jnp.float32(-jnp.inf)

    aq = [_dot_nt(jnp.concatenate([kb[:, hs[h]], q[:, hs[h]]], axis=0), k[:, hs[h]])
          for h in heads]
    lower, attn, sol = [], [], []
    for h in heads:
        g_col = gc[:, hs[h]]
        diff = jnp.concatenate([g_col] * (t // DN_HD), axis=1) - g4_t[dl + h:dl + h + 1, :]
        decay = jnp.exp(jnp.where(incl, diff, neg_inf))
        lower.append(jnp.where(strict, aq[h][:t] * decay, 0.0).astype(BF16))
        attn.append((aq[h][t:] * decay).astype(BF16))
        sol.append(jnp.concatenate([vb[:, hs[h]], kbe[:, hs[h]]], axis=1))
    fill("decay")

    sol = [sol[h] - _dot(lower[h], sol[h]) for h in heads]
    power = [_dot(lower[h], lower[h]).astype(BF16) for h in heads]
    n_sq = 2
    while True:
        sol = [sol[h] + _dot(power[h], sol[h]) for h in heads]
        if 2 * n_sq >= c:
            break
        power = [_dot(power[h], power[h]).astype(BF16) for h in heads]
        n_sq *= 2

    row_chunk = _iota((2 * c, 1), 0) // c
    gn = []
    for h in heads:
        per_chunk = []
        for ci in range(n_chunk):
            pair = slice((ci // 2) * 2 * c, (ci // 2 + 1) * 2 * c)
            rhs = jnp.where(row_chunk == (ci % 2), sol[h][pair], 0.0)
            per_chunk.append(_dot(kd_t[h][:, pair], rhs))
        gn.append(per_chunk)

    au = [_dot(attn[h], sol[h]) for h in heads]
    q_eff = [qd[:, hs[h]] - au[h][:, DN_HD:] for h in heads]

    def scan_and_outputs():
        state = [state_scr[h] for h in heads]
        from_state = [[] for _ in heads]
        for ci in range(n_chunk):
            rows = slice(ci * c, (ci + 1) * c)
            fill(("scan", ci))
            both = [_dot(jnp.concatenate([q_eff[h][rows], gn[h][ci][:, DN_HD:]], axis=0), state[h])
                    for h in heads]
            for h in heads:
                from_state[h].append(both[h][:c])
            state = [state[h] * c_dec[ci][:, dl + h:dl + h + 1]
                     + (gn[h][ci][:, :DN_HD] - both[h][c:]) for h in heads]

        outs = [jnp.concatenate(from_state[h], axis=0) + au[h][:, :DN_HD] for h in heads]
        o = jnp.concatenate(outs, axis=1)
        o = (o * lax.rsqrt(_head_sums([o], ones_bd)[0] * (1.0 / DN_HD) + NORM_EPS) * dnorm
             * _silu(get_z()))
        for h in heads:
            state_scr[h] = state[h]
        return o

    return scan_and_outputs


def _mixer_kernel(x_ref, cos_ref, sin_ref, ones_ref, anorm_ref, wfront_ref, wtail_ref,
                  lng_ref, lnb_ref, sguw_ref, sgub_ref, sinks_ref, convw_ref, alog_ref, dtb_ref,
                  dnorm_ref, wbr_ref, wout_ref, wgu32_ref, wd32_ref, o_ref, wgu16_ref, wd16_ref,
                  kv_scr, conv_scr, state_scr, gate_scr, wgate_scr, u_scr, vn_scr, probs_scr):
    seq_start = pl.program_id(1) == 0

    @pl.when(jnp.logical_and(pl.program_id(0) == 0, seq_start))
    def _():
        for n in range(3):
            window = wtail_ref[:, n * D_MODEL:(n + 1) * D_MODEL + GATE_OFF].astype(F32)
            wgate_scr[:, n * D_MODEL:(n + 1) * D_MODEL] = window[:, GATE_OFF:].astype(BF16)

    @pl.when(seq_start)
    def _():
        kv_scr[...] = jnp.zeros_like(kv_scr)
        conv_scr[0:SUBLANES, :] = jnp.zeros((SUBLANES, 3 * MIX), F32)
        state_scr[...] = jnp.zeros_like(state_scr)

    x = x_ref[...]
    h = _rmsnorm(x, anorm_ref[...]).astype(BF16)

    def proj(w_ref, lo, width):
        return jnp.dot(h, w_ref[:, lo:lo + width], preferred_element_type=F32)

    def gate(n, lo=0, width=D_MODEL):
        return jax.nn.sigmoid(proj(wgate_scr, n * D_MODEL + lo, width))

    park = {}

    def fill(stage):
        if stage == "prologue":
            qkv_b = proj(wfront_ref, C_QKVB, MIX + 2 * SWA_KV * SWA_HD)
            park["swa"] = _sliding_window_attention(
                qkv_b[:, :MIX], qkv_b[:, MIX:MIX + LANES], qkv_b[:, MIX + LANES:],
                cos_ref[...], sin_ref[...], sinks_ref, ones_ref[0:LANES, 0:LANES], kv_scr,
                probs_scr, seq_start)
            park["uv"] = proj(wfront_ref, C_UV, 2 * MIX)
            gate_scr[0] = gate(0)
            next(park["swa"])
        elif stage == "norms":
            gate_scr[1] = gate(1)
        elif stage == "decay":
            park["z"] = proj(wfront_ref, C_Z, MIX)
            next(park["swa"])
            uv = park.pop("uv")
            _spatial_gating_prep(uv[:, :MIX], uv[:, MIX:], lng_ref[...], lnb_ref[...],
                                 u_scr, vn_scr)
        elif stage[0] == "scan":
            lo = stage[1] * 2 * LANES
            gate_scr[2, :, lo:lo + 2 * LANES] = gate(2, lo, 2 * LANES)

    dn_tail = _gated_deltanet(proj(wfront_ref, C_QKVC, 3 * MIX), lambda: park.pop("z"),
                              proj(wtail_ref, 0, LANES), convw_ref,
                              alog_ref[...], dtb_ref[...], dnorm_ref[...], ones_ref[...],
                              conv_scr, state_scr, fill)

    out_c = dn_tail()
    out_a = _spatial_gating_mix(u_scr, vn_scr, sguw_ref, sgub_ref)
    out_b = next(park.pop("swa"))
    merged = (gate_scr[0] * _dot(out_a, wbr_ref[0]) + gate_scr[1] * _dot(out_b, wbr_ref[1])
              + gate_scr[2] * _dot(out_c, wbr_ref[2]))
    o_ref[...] = x + _dot(merged, wout_ref[...])
    wgu16_ref[...] = wgu32_ref[...].astype(BF16)
    wd16_ref[...] = wd32_ref[...].astype(BF16)


def _ffn_kernel(x_ref, fnorm_ref, wgu_ref, wd_ref, final_ref, o_ref, *, apply_final_norm):
    x = x_ref[...]
    h = _rmsnorm(x, fnorm_ref[...]).astype(BF16)
    gu = jnp.dot(h, wgu_ref[...], preferred_element_type=F32)
    act = _silu(gu[:, :D_FF]) * gu[:, D_FF:]
    y = x + _dot(act, wd_ref[...])
    if apply_final_norm:
        y = _rmsnorm(y, final_ref[...])
    o_ref[...] = y


def _resident(shape_tail, layer):
    n = len(shape_tail)
    return pl.BlockSpec((None,) + tuple(shape_tail), lambda *_: (layer,) + (0,) * n,
                        pipeline_mode=pl.Buffered(1))


def _mixer_call(x, rope, params, layer):
    batch, seq, _ = x.shape
    t = MIXER_TILE
    n_t = seq // t
    tok = lambda w: pl.BlockSpec((None, t, w), lambda b, i: (b, i, 0))
    gu_rows = D_MODEL // (batch * n_t)
    wd_steps = D_FF // WD_CAST_ROWS
    assert gu_rows % 16 == 0 and wd_steps <= batch * n_t
    gu_slice = lambda b, i: (b * n_t + i, 0)
    wd_slice = lambda b, i: (jnp.minimum(b * n_t + i, wd_steps - 1), 0)
    in_specs = [
        tok(D_MODEL), tok(LANES), tok(LANES),
        pl.BlockSpec((MXU_DIM, MXU_DIM), lambda b, i: (0, 0), pipeline_mode=pl.Buffered(1)),
        _resident((1, D_MODEL), layer),
        _resident((D_MODEL, C_FRONT_END), layer),
        _resident((D_MODEL, GATE_OFF + 3 * D_MODEL), layer),
        _resident((1, MIX), layer), _resident((1, MIX), layer),
        _resident((SGU_GROUPS, BLOCK, BLOCK), layer), _resident((BLOCK, SGU_GROUPS), layer),
        _resident((SWA_HEADS, LANES), layer),
        _resident((DN_CONV, 3 * MIX), layer),
        _resident((1, LANES), layer), _resident((1, LANES), layer), _resident((1, MIX), layer),
        _resident((3, MIX, D_MODEL), layer), _resident((D_MODEL, D_MODEL), layer),
        pl.BlockSpec((None, gu_rows, 2 * D_FF), lambda b, i: (layer,) + gu_slice(b, i)),
        pl.BlockSpec((None, WD_CAST_ROWS, D_MODEL), lambda b, i: (layer,) + wd_slice(b, i)),
    ]
    return pl.pallas_call(
        _mixer_kernel,
        out_shape=(jax.ShapeDtypeStruct(x.shape, F32),
                   jax.ShapeDtypeStruct((D_MODEL, 2 * D_FF), BF16),
                   jax.ShapeDtypeStruct((D_FF, D_MODEL), BF16)),
        grid=(batch, seq // t),
        in_specs=in_specs,
        out_specs=(tok(D_MODEL), pl.BlockSpec((gu_rows, 2 * D_FF), gu_slice),
                   pl.BlockSpec((WD_CAST_ROWS, D_MODEL), wd_slice)),
        scratch_shapes=[
            pltpu.VMEM((2, BLOCK, LANES), F32),
            pltpu.VMEM((SUBLANES, 3 * MIX), F32),
            pltpu.VMEM((DN_HEADS, DN_HD, DN_HD), F32),
            pltpu.VMEM((3, t, D_MODEL), F32),
            pltpu.VMEM((D_MODEL, 3 * D_MODEL), BF16),
            pltpu.VMEM((t, MIX), F32),
            pltpu.VMEM((t, MIX), BF16),
            pltpu.VMEM((t // BLOCK * SWA_KV, SWA_HEADS // SWA_KV * BLOCK, 2 * BLOCK), BF16),
        ],
        compiler_params=pltpu.CompilerParams(
            dimension_semantics=("arbitrary", "arbitrary"),
            vmem_limit_bytes=VMEM_LIMIT_BYTES),
        name=f"mixer_l{layer}",
    )(x, *rope, params["head_ones"], params["attn_norm"], params["w_in"],
      params["w_tail"], params["sgu_ln_g"], params["sgu_ln_b"],
      params["sgu_w"], params["sgu_b"], params["attn_sinks"], params["dn_conv_w"],
      params["dn_a_log"], params["dn_dt_bias"], params["dn_norm"], params["w_branch"],
      params["w_out"], params["w_gate_up"], params["w_down"])


def _ffn_call(x, w_gate_up16, w_down16, params, layer, apply_final_norm):
    batch, seq, _ = x.shape
    t = FFN_TILE
    tok = pl.BlockSpec((None, t, D_MODEL), lambda b, i: (b, i, 0))
    return pl.pallas_call(
        functools.partial(_ffn_kernel, apply_final_norm=apply_final_norm),
        out_shape=jax.ShapeDtypeStruct(x.shape, F32),
        grid=(batch, seq // t),
        in_specs=[
            tok,
            _resident((1, D_MODEL), layer),
            pl.BlockSpec((D_MODEL, 2 * D_FF), lambda b, i: (0, 0), pipeline_mode=pl.Buffered(1)),
            pl.BlockSpec((D_FF, D_MODEL), lambda b, i: (0, 0), pipeline_mode=pl.Buffered(1)),
            pl.BlockSpec((1, D_MODEL), lambda b, i: (0, 0)),
        ],
        out_specs=tok,
        compiler_params=pltpu.CompilerParams(
            dimension_semantics=("arbitrary", "arbitrary"),
            vmem_limit_bytes=VMEM_LIMIT_BYTES),
        name=f"ffn_l{layer}",
    )(x, params["ffn_norm"], w_gate_up16, w_down16, params["final_norm"])


def _rope_tables(positions):
    half = ROPE_DIM // 2
    inv_freq = ROPE_THETA ** (-jnp.arange(0, ROPE_DIM, 2, dtype=F32) / ROPE_DIM)
    ang = positions.astype(F32)[..., None] * inv_freq
    cos, sin = lax.optimization_barrier((jnp.cos(ang), jnp.sin(ang)))
    rotary = (jnp.arange(LANES) % SWA_HD) < ROPE_DIM
    reps = (1,) * (ang.ndim - 1) + (LANES // half,)
    return jnp.where(rotary, jnp.tile(cos, reps), 1.0), jnp.where(rotary, jnp.tile(sin, reps), 0.0)


def _pack_params(attn_norm, w_in, sgu_ln_g, sgu_ln_b, sgu_w, sgu_b, attn_sinks, dn_conv_w,
                 dn_a_log, dn_dt_bias, dn_norm, w_branch, w_out, ffn_norm, w_gate_up, w_down,
                 final_norm):
    depth = w_in.shape[0]
    decay_lanes = lambda a: jnp.pad(a, ((0, 0), (DN_HEADS, LANES - 2 * DN_HEADS)))[:, None, :]
    lane_head = jnp.arange(MXU_DIM) // DN_HD
    w_in16 = w_in.astype(BF16)
    return {
        "head_ones": (lane_head[:, None] == lane_head[None, :]).astype(BF16),
        "attn_norm": attn_norm[:, None, :],
        "w_in": w_in16,
        "w_tail": w_in16[:, :, C_FRONT_END:],
        "sgu_ln_g": sgu_ln_g[:, None, :],
        "sgu_ln_b": sgu_ln_b[:, None, :],
        "sgu_w": sgu_w,
        "sgu_b": jnp.swapaxes(sgu_b, 1, 2),
        "attn_sinks": jnp.broadcast_to(attn_sinks[:, :, None], (depth, SWA_HEADS, LANES)),
        "dn_conv_w": dn_conv_w,
        "dn_a_log": decay_lanes(dn_a_log),
        "dn_dt_bias": decay_lanes(dn_dt_bias),
        "dn_norm": jnp.tile(dn_norm, (1, DN_HEADS))[:, None, :],
        "w_branch": w_branch.astype(BF16),
        "w_out": w_out.astype(BF16),
        "ffn_norm": ffn_norm[:, None, :],
        "w_gate_up": w_gate_up,
        "w_down": w_down,
        "final_norm": final_norm[None, :],
    }


def kernel(x, positions, attn_norm, w_in, sgu_ln_g, sgu_ln_b, sgu_w, sgu_b, attn_sinks,
           dn_conv_w, dn_a_log, dn_dt_bias, dn_norm, w_branch, w_out, ffn_norm, w_gate_up,
           w_down, final_norm):
    assert MIXER_TILE == MXU_DIM == 4 * DN_CHUNK
    assert x.shape[1] % MIXER_TILE == 0 and x.shape[1] % FFN_TILE == 0
    assert x.shape[2] == D_MODEL
    assert w_in.shape[2] == C_FRONT_END + GATE_OFF + 3 * D_MODEL
    depth = w_in.shape[0]
    params = _pack_params(attn_norm, w_in, sgu_ln_g, sgu_ln_b, sgu_w, sgu_b, attn_sinks,
                          dn_conv_w, dn_a_log, dn_dt_bias, dn_norm, w_branch, w_out, ffn_norm,
                          w_gate_up, w_down, final_norm)
    rope = _rope_tables(positions)
    for layer in range(depth):
        x, w_gate_up16, w_down16 = _mixer_call(x, rope, params, layer)
        x = _ffn_call(x, w_gate_up16, w_down16, params, layer,
                      apply_final_norm=(layer == depth - 1))
    return x
```

```python
import functools

import jax
import jax.numpy as jnp
from jax import lax
from jax.experimental import pallas as pl
from jax.experimental.pallas import tpu as pltpu

D_MODEL = 1024
MIX = 512
NORM_EPS = 1e-6

SGU_GROUPS = 4
BLOCK = 128

SWA_HEADS = 8
SWA_KV = 2
SWA_HD = 64
ROPE_THETA = 500000.0
ROPE_DIM = 16

DN_HEADS = 4
DN_HD = 128
DN_CONV = 4
DN_CHUNK = 64

D_FF = 2816

LANES = 128
SUBLANES = 8
MXU_DIM = 256

C_UV = 0
C_QKVB = C_UV + 2 * MIX
C_QKVC = C_QKVB + MIX + 2 * SWA_KV * SWA_HD
C_Z = C_QKVC + 3 * MIX
C_FRONT_END = C_Z + MIX
GATE_OFF = 2 * DN_HEADS

MIXER_TILE = 256
FFN_TILE = 512
WD_CAST_ROWS = 64
VMEM_CAPACITY_BYTES = 64 * 1024 * 1024
VMEM_LIMIT_BYTES = VMEM_CAPACITY_BYTES // 8 * 7

BF16 = jnp.bfloat16
F32 = jnp.float32
_GELU_C = 0.7978845608028654


def _dot(a, b):
    return jnp.dot(a.astype(BF16), b.astype(BF16), preferred_element_type=F32)


def _dot_nt(a, b):
    return lax.dot_general(a.astype(BF16), b.astype(BF16), (((1,), (1,)), ((), ())),
                           preferred_element_type=F32)


def _rmsnorm(x, g):
    return x * lax.rsqrt(jnp.mean(x * x, axis=-1, keepdims=True) + NORM_EPS) * g


def _gelu_tanh(x):
    inner = x * (_GELU_C + (_GELU_C * 0.044715) * (x * x))
    return (0.5 * x) * (1.0 + jnp.tanh(inner))


def _silu(x):
    return x * jax.nn.sigmoid(x)


def _softplus(x):
    return jnp.maximum(x, 0.0) + jnp.log(1.0 + jnp.exp(-jnp.abs(x)))


def _iota(shape, dim):
    return lax.broadcasted_iota(jnp.int32, shape, dim)


def _spatial_gating_prep(u_pre, v_pre, ln_g, ln_b, u_scr, vn_scr):
    u_scr[...] = _gelu_tanh(u_pre)
    v = _gelu_tanh(v_pre)
    vc = v - jnp.mean(v, axis=-1, keepdims=True)
    vn_scr[...] = (vc * lax.rsqrt(jnp.mean(vc * vc, axis=-1, keepdims=True) + NORM_EPS) * ln_g
                   + ln_b).astype(BF16)


def _spatial_gating_mix(u_scr, vn_scr, sguw_ref, sgub_ref):
    u = u_scr[...]
    vn = vn_scr[...]
    t = u.shape[0]
    causal = _iota((BLOCK, BLOCK), 0) >= _iota((BLOCK, BLOCK), 1)
    groups = []
    for g in range(SGU_GROUPS):
        w = jnp.where(causal, sguw_ref[g], 0.0).astype(BF16)
        bias = sgub_ref[:, g:g + 1]
        cols = slice(g * LANES, (g + 1) * LANES)
        mixed = [_dot(w, vn[blk * BLOCK:(blk + 1) * BLOCK, cols]) + bias
                 for blk in range(t // BLOCK)]
        groups.append(u[:, cols] * jnp.concatenate(mixed, axis=0))
    return jnp.concatenate(groups, axis=1)


def _rope(x, cos, sin):
    half = ROPE_DIM // 2
    first_half = (_iota((1, LANES), 1) % SWA_HD) < half
    sin_a = jnp.where(first_half, -sin, 0.0)
    sin_b = jnp.where(first_half, 0.0, sin)
    n = x.shape[1] // LANES
    if n > 1:
        cos = jnp.concatenate([cos] * n, axis=1)
        sin_a = jnp.concatenate([sin_a] * n, axis=1)
        sin_b = jnp.concatenate([sin_b] * n, axis=1)
    width = x.shape[1]
    return (x * cos + pltpu.roll(x, width - half, 1) * sin_a + pltpu.roll(x, half, 1) * sin_b)


def _sliding_window_attention(q, k, v, cos, sin, sinks_ref, ones, kv_scr, probs_scr, seq_start):
    t = q.shape[0]
    n_blk = t // BLOCK
    group = SWA_HEADS // SWA_KV
    q = _rope(q, cos, sin) * (SWA_HD ** -0.5)
    k = _rope(k, cos, sin)

    kcat = jnp.concatenate([kv_scr[0], k], axis=0)
    vcat = jnp.concatenate([kv_scr[1], v], axis=0)

    lane = _iota((1, LANES), 1)
    lo = lane < SWA_HD
    krot = pltpu.roll(kcat, SWA_HD, 1)
    vrot = pltpu.roll(vcat, SWA_HD, 1)
    kdup = [jnp.where(lo, kcat, krot).astype(BF16), jnp.where(lo, krot, kcat).astype(BF16)]
    vlo = [jnp.where(lo, vcat, 0.0).astype(BF16), jnp.where(lo, vrot, 0.0).astype(BF16)]
    vhi = [jnp.where(lo, 0.0, vrot).astype(BF16), jnp.where(lo, 0.0, vcat).astype(BF16)]

    rows4 = group * BLOCK
    cur_mask = (_iota((rows4, BLOCK), 0) % BLOCK) >= _iota((rows4, BLOCK), 1)
    neg_inf = jnp.float32(-jnp.inf)
    units = [(blk, kh) for blk in range(n_blk) for kh in range(SWA_KV)]

    lhs = []
    for blk, kh in units:
        rows = slice(blk * BLOCK, (blk + 1) * BLOCK)
        s0 = q[rows, (2 * kh) * LANES:(2 * kh + 1) * LANES]
        s1 = q[rows, (2 * kh + 1) * LANES:(2 * kh + 2) * LANES]
        lhs.append(jnp.concatenate([jnp.where(lo, s0, 0.0), jnp.where(lo, s1, 0.0),
                                    jnp.where(lo, 0.0, s0), jnp.where(lo, 0.0, s1)],
                                   axis=0).astype(BF16))
    logits = [_dot_nt(lhs[i], kdup[kh][blk * BLOCK:(blk + 2) * BLOCK])
              for i, (blk, kh) in enumerate(units)]
    yield
    n_unit = len(units)
    sinks = [jnp.concatenate(
        [jnp.broadcast_to(sinks_ref[group * kh + 2 * j + half:group * kh + 2 * j + half + 1, :],
                          (BLOCK, LANES)) for half in range(2) for j in range(2)], axis=0)
        for kh in range(SWA_KV)]
    act = []
    for i, (blk, kh) in enumerate(units):
        l_prev = logits[i][:, :BLOCK]
        if blk == 0:
            l_prev = jnp.where(seq_start, neg_inf, l_prev)
        act.append(jnp.where(cur_mask, logits[i][:, BLOCK:], l_prev))
    m = [jnp.maximum(jnp.broadcast_to(jnp.max(act[i], axis=1, keepdims=True), act[i].shape),
                     sinks[units[i][1]]) for i in range(n_unit)]
    p = [jnp.exp(act[i] - m[i]) for i in range(n_unit)]
    rows_u = p[0].shape[0]
    sums = _dot(jnp.concatenate(p, axis=0), ones)
    den = [sums[i * rows_u:(i + 1) * rows_u] + jnp.exp(sinks[units[i][1]] - m[i])
           for i in range(n_unit)]
    for i in range(n_unit):
        pn = p[i] * (1.0 / den[i])
        probs_scr[i] = jnp.concatenate([jnp.where(cur_mask, 0.0, pn), jnp.where(cur_mask, pn, 0.0)],
                                       axis=1).astype(BF16)
    yield
    probs = [probs_scr[i] for i in range(len(units))]
    out_rows = []
    for blk in range(n_blk):
        slabs = []
        for kh in range(SWA_KV):
            i = blk * SWA_KV + kh
            keys = slice(blk * BLOCK, (blk + 2) * BLOCK)
            res = (_dot(probs[i][:2 * BLOCK], vlo[kh][keys])
                   + _dot(probs[i][2 * BLOCK:], vhi[kh][keys]))
            slabs += [res[:BLOCK], res[BLOCK:]]
        out_rows.append(jnp.concatenate(slabs, axis=1))
    kv_scr[0] = k[t - BLOCK:, :]
    kv_scr[1] = v[t - BLOCK:, :]
    yield jnp.concatenate(out_rows, axis=0)


def _chunk_cumsum(x):
    row = _iota(x.shape, 0) % DN_CHUNK
    shift = 1
    while shift < DN_CHUNK:
        x = x + jnp.where(row >= shift, pltpu.roll(x, shift, 0), 0.0)
        shift *= 2
    return x


def _head_sums(xs, ones_bd):
    t = xs[0].shape[0]
    n_half = xs[0].shape[1] // MXU_DIM
    stacked = jnp.concatenate([(x * x)[:, i * MXU_DIM:(i + 1) * MXU_DIM]
                               for x in xs for i in range(n_half)], axis=0)
    sums = _dot(stacked, ones_bd)
    return [jnp.concatenate([sums[(j * n_half + i) * t:(j * n_half + i + 1) * t]
                             for i in range(n_half)], axis=1) for j in range(len(xs))]


def _gated_deltanet(qkv, get_z, logits, convw_ref, alog, dtb, dnorm, ones_bd,
                    conv_scr, state_scr, fill):
    t = qkv.shape[0]
    c = DN_CHUNK
    n_chunk = t // c
    pad = SUBLANES
    heads = range(DN_HEADS)
    hs = [slice(h * DN_HD, (h + 1) * DN_HD) for h in heads]

    xa = jnp.concatenate([conv_scr[0:pad, :], qkv], axis=0)
    y = qkv * convw_ref[DN_CONV - 1:DN_CONV, :]
    for i in range(DN_CONV - 1):
        y = y + pltpu.roll(xa, DN_CONV - 1 - i, 0)[pad:, :] * convw_ref[i:i + 1, :]
    conv_scr[0:pad, :] = qkv[t - pad:, :]
    y = _silu(y)

    q = y[:, :MIX]
    k = y[:, MIX:2 * MIX]
    v = y[:, 2 * MIX:]
    fill("prologue")
    q_ss, k_ss = _head_sums([q, k], ones_bd)
    q = q * lax.rsqrt(q_ss + NORM_EPS) * (DN_HD ** -0.5)
    k = k * lax.rsqrt(k_ss + NORM_EPS)

    def spread(slab, lo):
        return jnp.concatenate([jnp.broadcast_to(slab[:, lo + j:lo + j + 1], (t, DN_HD))
                                for j in heads], axis=1)

    dl = DN_HEADS
    gc_c = _chunk_cumsum(-jnp.exp(alog) * _softplus(logits + dtb))
    last = [gc_c[(ci + 1) * c - 1:(ci + 1) * c, :] for ci in range(n_chunk)]
    g_last_c = jnp.concatenate([jnp.broadcast_to(r, (c, LANES)) for r in last], axis=0)
    beta = spread(jax.nn.sigmoid(logits), 0)
    gc = spread(gc_c, dl)
    egc = spread(jnp.exp(gc_c), dl)
    kb = k * beta
    vb = v * beta
    kbe = kb * egc
    qd = q * egc
    kd = k * spread(jnp.exp(g_last_c - gc_c), dl)
    c_dec = [jnp.exp(r) for r in last]

    g4_t = gc_c.T
    kd_t = [kd[:, hs[h]].T.astype(BF16) for h in heads]
    fill("norms")

    ri = _iota((t, t), 0)
    ci_ = _iota((t, t), 1)
    same = (ri // c) == (ci_ // c)
    incl = jnp.logical_and(same, ri >= ci_)
    strict = jnp.logical_and(same, ri > ci_)
    neg_inf = jnp.float32(-jnp.inf)

    aq = [_dot_nt(jnp.concatenate([kb[:, hs[h]], q[:, hs[h]]], axis=0), k[:, hs[h]])
          for h in heads]
    lower, attn, sol = [], [], []
    for h in heads:
        g_col = gc[:, hs[h]]
        diff = jnp.concatenate([g_col] * (t // DN_HD), axis=1) - g4_t[dl + h:dl + h + 1, :]
        decay = jnp.exp(jnp.where(incl, diff, neg_inf))
        lower.append(jnp.where(strict, aq[h][:t] * decay, 0.0).astype(BF16))
        attn.append((aq[h][t:] * decay).astype(BF16))
        sol.append(jnp.concatenate([vb[:, hs[h]], kbe[:, hs[h]]], axis=1))
    fill("decay")

    sol = [sol[h] - _dot(lower[h], sol[h]) for h in heads]
    power = [_dot(lower[h], lower[h]).astype(BF16) for h in heads]
    n_sq = 2
    while True:
        sol = [sol[h] + _dot(power[h], sol[h]) for h in heads]
        if 2 * n_sq >= c:
            break
        power = [_dot(power[h], power[h]).astype(BF16) for h in heads]
        n_sq *= 2

    row_chunk = _iota((2 * c, 1), 0) // c
    gn = []
    for h in heads:
        per_chunk = []
        for ci in range(n_chunk):
            pair = slice((ci // 2) * 2 * c, (ci // 2 + 1) * 2 * c)
            rhs = jnp.where(row_chunk == (ci % 2), sol[h][pair], 0.0)
            per_chunk.append(_dot(kd_t[h][:, pair], rhs))
        gn.append(per_chunk)

    au = [_dot(attn[h], sol[h]) for h in heads]
    q_eff = [qd[:, hs[h]] - au[h][:, DN_HD:] for h in heads]

    def scan_and_outputs():
        state = [state_scr[h] for h in heads]
        from_state = [[] for _ in heads]
        for ci in range(n_chunk):
            rows = slice(ci * c, (ci + 1) * c)
            fill(("scan", ci))
            both = [_dot(jnp.concatenate([q_eff[h][rows], gn[h][ci][:, DN_HD:]], axis=0), state[h])
                    for h in heads]
            for h in heads:
                from_state[h].append(both[h][:c])
            state = [state[h] * c_dec[ci][:, dl + h:dl + h + 1]
                     + (gn[h][ci][:, :DN_HD] - both[h][c:]) for h in heads]

        outs = [jnp.concatenate(from_state[h], axis=0) + au[h][:, :DN_HD] for h in heads]
        o = jnp.concatenate(outs, axis=1)
        o = (o * lax.rsqrt(_head_sums([o], ones_bd)[0] * (1.0 / DN_HD) + NORM_EPS) * dnorm
             * _silu(get_z()))
        for h in heads:
            state_scr[h] = state[h]
        return o

    return scan_and_outputs


def _mixer_kernel(x_ref, cos_ref, sin_ref, ones_ref, anorm_ref, wfront_ref, wtail_ref,
                  lng_ref, lnb_ref, sguw_ref, sgub_ref, sinks_ref, convw_ref, alog_ref, dtb_ref,
                  dnorm_ref, wbr_ref, wout_ref, wgu32_ref, wd32_ref, o_ref, wgu16_ref, wd16_ref,
                  kv_scr, conv_scr, state_scr, gate_scr, wgate_scr, u_scr, vn_scr, probs_scr):
    seq_start = pl.program_id(1) == 0

    @pl.when(jnp.logical_and(pl.program_id(0) == 0, seq_start))
    def _():
        for n in range(3):
            window = wtail_ref[:, n * D_MODEL:(n + 1) * D_MODEL + GATE_OFF].astype(F32)
            wgate_scr[:, n * D_MODEL:(n + 1) * D_MODEL] = window[:, GATE_OFF:].astype(BF16)

    @pl.when(seq_start)
    def _():
        kv_scr[...] = jnp.zeros_like(kv_scr)
        conv_scr[0:SUBLANES, :] = jnp.zeros((SUBLANES, 3 * MIX), F32)
        state_scr[...] = jnp.zeros_like(state_scr)

    x = x_ref[...]
    h = _rmsnorm(x, anorm_ref[...]).astype(BF16)

    def proj(w_ref, lo, width):
        return jnp.dot(h, w_ref[:, lo:lo + width], preferred_element_type=F32)

    def gate(n, lo=0, width=D_MODEL):
        return jax.nn.sigmoid(proj(wgate_scr, n * D_MODEL + lo, width))

    park = {}

    def fill(stage):
        if stage == "prologue":
            qkv_b = proj(wfront_ref, C_QKVB, MIX + 2 * SWA_KV * SWA_HD)
            park["swa"] = _sliding_window_attention(
                qkv_b[:, :MIX], qkv_b[:, MIX:MIX + LANES], qkv_b[:, MIX + LANES:],
                cos_ref[...], sin_ref[...], sinks_ref, ones_ref[0:LANES, 0:LANES], kv_scr,
                probs_scr, seq_start)
            park["uv"] = proj(wfront_ref, C_UV, 2 * MIX)
            gate_scr[0] = gate(0)
            next(park["swa"])
        elif stage == "norms":
            gate_scr[1] = gate(1)
        elif stage == "decay":
            park["z"] = proj(wfront_ref, C_Z, MIX)
            next(park["swa"])
            uv = park.pop("uv")
            _spatial_gating_prep(uv[:, :MIX], uv[:, MIX:], lng_ref[...], lnb_ref[...],
                                 u_scr, vn_scr)
        elif stage[0] == "scan":
            lo = stage[1] * 2 * LANES
            gate_scr[2, :, lo:lo + 2 * LANES] = gate(2, lo, 2 * LANES)

    dn_tail = _gated_deltanet(proj(wfront_ref, C_QKVC, 3 * MIX), lambda: park.pop("z"),
                              proj(wtail_ref, 0, LANES), convw_ref,
                              alog_ref[...], dtb_ref[...], dnorm_ref[...], ones_ref[...],
                              conv_scr, state_scr, fill)

    out_c = dn_tail()
    out_a = _spatial_gating_mix(u_scr, vn_scr, sguw_ref, sgub_ref)
    out_b = next(park.pop("swa"))
    merged = (gate_scr[0] * _dot(out_a, wbr_ref[0]) + gate_scr[1] * _dot(out_b, wbr_ref[1])
              + gate_scr[2] * _dot(out_c, wbr_ref[2]))
    o_ref[...] = x + _dot(merged, wout_ref[...])
    wgu16_ref[...] = wgu32_ref[...].astype(BF16)
    wd16_ref[...] = wd32_ref[...].astype(BF16)


def _ffn_kernel(x_ref, fnorm_ref, wgu_ref, wd_ref, final_ref, *rest, apply_final_norm):
    if len(rest) > 1:
        win32_ref, wbr32_ref, wout32_ref, o_ref, wfront16_ref, wtail16_ref, wbr16_ref, wout16_ref = rest
        wfront16_ref[...] = win32_ref[:, :C_FRONT_END].astype(BF16)
        wtail16_ref[...] = win32_ref[:, C_FRONT_END:].astype(BF16)
        wbr16_ref[...] = wbr32_ref[...].astype(BF16)
        wout16_ref[...] = wout32_ref[...].astype(BF16)
    else:
        (o_ref,) = rest
    x = x_ref[...]
    h = _rmsnorm(x, fnorm_ref[...]).astype(BF16)
    gu = jnp.dot(h, wgu_ref[...], preferred_element_type=F32)
    act = _silu(gu[:, :D_FF]) * gu[:, D_FF:]
    y = x + _dot(act, wd_ref[...])
    if apply_final_norm:
        y = _rmsnorm(y, final_ref[...])
    o_ref[...] = y


def _resident(shape_tail, layer):
    n = len(shape_tail)
    return pl.BlockSpec((None,) + tuple(shape_tail), lambda *_: (layer,) + (0,) * n,
                        pipeline_mode=pl.Buffered(1))


def _whole(shape):
    return pl.BlockSpec(shape, lambda *_: (0,) * len(shape), pipeline_mode=pl.Buffered(1))


def _mixer_call(x, rope, mixer_w, params, layer):
    batch, seq, _ = x.shape
    t = MIXER_TILE
    n_t = seq // t
    tok = lambda w: pl.BlockSpec((None, t, w), lambda b, i: (b, i, 0))
    gu_rows = D_MODEL // (batch * n_t)
    wd_steps = D_FF // WD_CAST_ROWS
    assert gu_rows % 16 == 0 and wd_steps <= batch * n_t
    gu_slice = lambda b, i: (b * n_t + i, 0)
    wd_slice = lambda b, i: (jnp.minimum(b * n_t + i, wd_steps - 1), 0)
    in_specs = [
        tok(D_MODEL), tok(LANES), tok(LANES),
        pl.BlockSpec((MXU_DIM, MXU_DIM), lambda b, i: (0, 0), pipeline_mode=pl.Buffered(1)),
        _resident((1, D_MODEL), layer),
        _whole((D_MODEL, C_FRONT_END)),
        _whole((D_MODEL, GATE_OFF + 3 * D_MODEL)),
        _resident((1, MIX), layer), _resident((1, MIX), layer),
        _resident((SGU_GROUPS, BLOCK, BLOCK), layer), _resident((BLOCK, SGU_GROUPS), layer),
        _resident((SWA_HEADS, LANES), layer),
        _resident((DN_CONV, 3 * MIX), layer),
        _resident((1, LANES), layer), _resident((1, LANES), layer), _resident((1, MIX), layer),
        _whole((3, MIX, D_MODEL)), _whole((D_MODEL, D_MODEL)),
        pl.BlockSpec((None, gu_rows, 2 * D_FF), lambda b, i: (layer,) + gu_slice(b, i)),
        pl.BlockSpec((None, WD_CAST_ROWS, D_MODEL), lambda b, i: (layer,) + wd_slice(b, i)),
    ]
    return pl.pallas_call(
        _mixer_kernel,
        out_shape=(jax.ShapeDtypeStruct(x.shape, F32),
                   jax.ShapeDtypeStruct((D_MODEL, 2 * D_FF), BF16),
                   jax.ShapeDtypeStruct((D_FF, D_MODEL), BF16)),
        grid=(batch, seq // t),
        in_specs=in_specs,
        out_specs=(tok(D_MODEL), pl.BlockSpec((gu_rows, 2 * D_FF), gu_slice),
                   pl.BlockSpec((WD_CAST_ROWS, D_MODEL), wd_slice)),
        scratch_shapes=[
            pltpu.VMEM((2, BLOCK, LANES), F32),
            pltpu.VMEM((SUBLANES, 3 * MIX), F32),
            pltpu.VMEM((DN_HEADS, DN_HD, DN_HD), F32),
            pltpu.VMEM((3, t, D_MODEL), F32),
            pltpu.VMEM((D_MODEL, 3 * D_MODEL), BF16),
            pltpu.VMEM((t, MIX), F32),
            pltpu.VMEM((t, MIX), BF16),
            pltpu.VMEM((t // BLOCK * SWA_KV, SWA_HEADS // SWA_KV * BLOCK, 2 * BLOCK), BF16),
        ],
        compiler_params=pltpu.CompilerParams(
            dimension_semantics=("arbitrary", "arbitrary"),
            vmem_limit_bytes=VMEM_LIMIT_BYTES),
        name=f"mixer_l{layer}",
    )(x, *rope, params["head_ones"], params["attn_norm"], mixer_w[0],
      mixer_w[1], params["sgu_ln_g"], params["sgu_ln_b"],
      params["sgu_w"], params["sgu_b"], params["attn_sinks"], params["dn_conv_w"],
      params["dn_a_log"], params["dn_dt_bias"], params["dn_norm"], mixer_w[2],
      mixer_w[3], params["w_gate_up"], params["w_down"])


def _ffn_call(x, w_gate_up16, w_down16, params, layer, apply_final_norm, cast_next):
    batch, seq, _ = x.shape
    t = FFN_TILE
    n_t = seq // t
    steps = batch * n_t
    tok = pl.BlockSpec((None, t, D_MODEL), lambda b, i: (b, i, 0))
    in_specs = [
        tok,
        _resident((1, D_MODEL), layer),
        _whole((D_MODEL, 2 * D_FF)),
        _whole((D_FF, D_MODEL)),
        pl.BlockSpec((1, D_MODEL), lambda b, i: (0, 0)),
    ]
    args = [x, params["ffn_norm"], w_gate_up16, w_down16, params["final_norm"]]
    out_shape = [jax.ShapeDtypeStruct(x.shape, F32)]
    out_specs = [tok]
    if cast_next:
        w_in, w_branch, w_out = params["w_in32"], params["w_branch32"], params["w_out32"]
        tail_cols = w_in.shape[2] - C_FRONT_END
        in_rows, br_rows = D_MODEL // steps, w_branch.shape[1] // steps
        assert in_rows % 16 == 0 and br_rows % 16 == 0
        step = lambda b, i: (b * n_t + i, 0)
        nxt = lambda b, i: (layer + 1,) + step(b, i)
        in_specs += [pl.BlockSpec((None, in_rows, w_in.shape[2]), nxt),
                     pl.BlockSpec((None, br_rows, D_MODEL), nxt),
                     pl.BlockSpec((None, in_rows, D_MODEL), nxt)]
        args += [w_in, w_branch, w_out]
        out_shape += [jax.ShapeDtypeStruct((D_MODEL, C_FRONT_END), BF16),
                      jax.ShapeDtypeStruct((D_MODEL, tail_cols), BF16),
                      jax.ShapeDtypeStruct(w_branch.shape[1:], BF16),
                      jax.ShapeDtypeStruct((D_MODEL, D_MODEL), BF16)]
        out_specs += [pl.BlockSpec((in_rows, C_FRONT_END), step),
                      pl.BlockSpec((in_rows, tail_cols), step),
                      pl.BlockSpec((br_rows, D_MODEL), step),
                      pl.BlockSpec((in_rows, D_MODEL), step)]
    return pl.pallas_call(
        functools.partial(_ffn_kernel, apply_final_norm=apply_final_norm),
        out_shape=tuple(out_shape),
        grid=(batch, n_t),
        in_specs=in_specs,
        out_specs=tuple(out_specs),
        compiler_params=pltpu.CompilerParams(
            dimension_semantics=("arbitrary", "arbitrary"),
            vmem_limit_bytes=VMEM_LIMIT_BYTES),
        name=f"ffn_l{layer}",
    )(*args)


def _rope_tables(positions):
    half = ROPE_DIM // 2
    inv_freq = ROPE_THETA ** (-jnp.arange(0, ROPE_DIM, 2, dtype=F32) / ROPE_DIM)
    ang = positions.astype(F32)[..., None] * inv_freq
    cos, sin = lax.optimization_barrier((jnp.cos(ang), jnp.sin(ang)))
    rotary = (jnp.arange(LANES) % SWA_HD) < ROPE_DIM
    reps = (1,) * (ang.ndim - 1) + (LANES // half,)
    return jnp.where(rotary, jnp.tile(cos, reps), 1.0), jnp.where(rotary, jnp.tile(sin, reps), 0.0)


def _pack_params(attn_norm, w_in, sgu_ln_g, sgu_ln_b, sgu_w, sgu_b, attn_sinks, dn_conv_w,
                 dn_a_log, dn_dt_bias, dn_norm, w_branch, w_out, ffn_norm, w_gate_up, w_down,
                 final_norm):
    depth = w_in.shape[0]
    decay_lanes = lambda a: jnp.pad(a, ((0, 0), (DN_HEADS, LANES - 2 * DN_HEADS)))[:, None, :]
    lane_head = jnp.arange(MXU_DIM) // DN_HD
    w_in16 = w_in[0].astype(BF16)
    return {
        "head_ones": (lane_head[:, None] == lane_head[None, :]).astype(BF16),
        "attn_norm": attn_norm[:, None, :],
        "mixer_w0": (w_in16, w_in16[:, C_FRONT_END:], w_branch[0].astype(BF16),
                     w_out[0].astype(BF16)),
        "w_in32": w_in,
        "w_branch32": w_branch.reshape(depth, -1, D_MODEL),
        "w_out32": w_out,
        "sgu_ln_g": sgu_ln_g[:, None, :],
        "sgu_ln_b": sgu_ln_b[:, None, :],
        "sgu_w": sgu_w,
        "sgu_b": jnp.swapaxes(sgu_b, 1, 2),
        "attn_sinks": jnp.broadcast_to(attn_sinks[:, :, None], (depth, SWA_HEADS, LANES)),
        "dn_conv_w": dn_conv_w,
        "dn_a_log": decay_lanes(dn_a_log),
        "dn_dt_bias": decay_lanes(dn_dt_bias),
        "dn_norm": jnp.tile(dn_norm, (1, DN_HEADS))[:, None, :],
        "ffn_norm": ffn_norm[:, None, :],
        "w_gate_up": w_gate_up,
        "w_down": w_down,
        "final_norm": final_norm[None, :],
    }


def kernel(x, positions, attn_norm, w_in, sgu_ln_g, sgu_ln_b, sgu_w, sgu_b, attn_sinks,
           dn_conv_w, dn_a_log, dn_dt_bias, dn_norm, w_branch, w_out, ffn_norm, w_gate_up,
           w_down, final_norm):
    assert MIXER_TILE == MXU_DIM == 4 * DN_CHUNK
    assert x.shape[1] % MIXER_TILE == 0 and x.shape[1] % FFN_TILE == 0
    assert x.shape[2] == D_MODEL
    assert w_in.shape[2] == C_FRONT_END + GATE_OFF + 3 * D_MODEL
    depth = w_in.shape[0]
    params = _pack_params(attn_norm, w_in, sgu_ln_g, sgu_ln_b, sgu_w, sgu_b, attn_sinks,
                          dn_conv_w, dn_a_log, dn_dt_bias, dn_norm, w_branch, w_out, ffn_norm,
                          w_gate_up, w_down, final_norm)
    rope = _rope_tables(positions)
    mixer_w = params["mixer_w0"]
    for layer in range(depth):
        last = layer == depth - 1
        x, w_gate_up16, w_down16 = _mixer_call(x, rope, mixer_w, params, layer)
        outs = _ffn_call(x, w_gate_up16, w_down16, params, layer, apply_final_norm=last,
                         cast_next=not last)
        x = outs[0]
        if not last:
            mixer_w = (outs[1], outs[2], outs[3].reshape(w_branch.shape[1:]), outs[4])
    return x
```

```python
import functools

import jax
import jax.numpy as jnp
from jax import lax
from jax.experimental import pallas as pl
from jax.experimental.pallas import tpu as pltpu

D_MODEL = 1024
MIX = 512
NORM_EPS = 1e-6

SGU_GROUPS = 4
BLOCK = 128

SWA_HEADS = 8
SWA_KV = 2
SWA_HD = 64
ROPE_THETA = 500000.0
ROPE_DIM = 16

DN_HEADS = 4
DN_HD = 128
DN_CONV = 4
DN_CHUNK = 64

D_FF = 2816

LANES = 128
SUBLANES = 8
MXU_DIM = 256

C_UV = 0
C_QKVB = C_UV + 2 * MIX
C_QKVC = C_QKVB + MIX + 2 * SWA_KV * SWA_HD
C_Z = C_QKVC + 3 * MIX
C_FRONT_END = C_Z + MIX
GATE_OFF = 2 * DN_HEADS

MIXER_TILE = 256
FFN_TILE = 512
WD_CAST_ROWS = 64
VMEM_CAPACITY_BYTES = 64 * 1024 * 1024
VMEM_LIMIT_BYTES = VMEM_CAPACITY_BYTES // 8 * 7

BF16 = jnp.bfloat16
F32 = jnp.float32
_GELU_C = 0.7978845608028654


def _dot(a, b):
    return jnp.dot(a.astype(BF16), b.astype(BF16), preferred_element_type=F32)


def _dot_nt(a, b):
    return lax.dot_general(a.astype(BF16), b.astype(BF16), (((1,), (1,)), ((), ())),
                           preferred_element_type=F32)


def _rmsnorm(x, g):
    return x * lax.rsqrt(jnp.mean(x * x, axis=-1, keepdims=True) + NORM_EPS) * g


def _gelu_tanh(x):
    inner = x * (_GELU_C + (_GELU_C * 0.044715) * (x * x))
    return (0.5 * x) * (1.0 + jnp.tanh(inner))


def _silu(x):
    return x * jax.nn.sigmoid(x)


def _softplus(x):
    return jnp.maximum(x, 0.0) + jnp.log(1.0 + jnp.exp(-jnp.abs(x)))


def _iota(shape, dim):
    return lax.broadcasted_iota(jnp.int32, shape, dim)


def _spatial_gating_prep(u_pre, v_pre, ln_g, ln_b, u_scr, vn_scr):
    u_scr[...] = _gelu_tanh(u_pre)
    v = _gelu_tanh(v_pre)
    vc = v - jnp.mean(v, axis=-1, keepdims=True)
    vn_scr[...] = (vc * lax.rsqrt(jnp.mean(vc * vc, axis=-1, keepdims=True) + NORM_EPS) * ln_g
                   + ln_b).astype(BF16)


def _spatial_gating_mix(u_scr, vn_scr, sguw_ref, sgub_ref):
    u = u_scr[...]
    vn = vn_scr[...]
    t = u.shape[0]
    causal = _iota((BLOCK, BLOCK), 0) >= _iota((BLOCK, BLOCK), 1)
    groups = []
    for g in range(SGU_GROUPS):
        w = jnp.where(causal, sguw_ref[g], 0.0).astype(BF16)
        bias = sgub_ref[:, g:g + 1]
        cols = slice(g * LANES, (g + 1) * LANES)
        mixed = [_dot(w, vn[blk * BLOCK:(blk + 1) * BLOCK, cols]) + bias
                 for blk in range(t // BLOCK)]
        groups.append(u[:, cols] * jnp.concatenate(mixed, axis=0))
    return jnp.concatenate(groups, axis=1)


def _rope(x, cos, sin):
    half = ROPE_DIM // 2
    first_half = (_iota((1, LANES), 1) % SWA_HD) < half
    sin_a = jnp.where(first_half, -sin, 0.0)
    sin_b = jnp.where(first_half, 0.0, sin)
    n = x.shape[1] // LANES
    if n > 1:
        cos = jnp.concatenate([cos] * n, axis=1)
        sin_a = jnp.concatenate([sin_a] * n, axis=1)
        sin_b = jnp.concatenate([sin_b] * n, axis=1)
    width = x.shape[1]
    return (x * cos + pltpu.roll(x, width - half, 1) * sin_a + pltpu.roll(x, half, 1) * sin_b)


def _sliding_window_attention(q, k, v, cos, sin, sinks_ref, ones, kv_scr, probs_scr, seq_start):
    t = q.shape[0]
    n_blk = t // BLOCK
    group = SWA_HEADS // SWA_KV
    q = _rope(q, cos, sin) * (SWA_HD ** -0.5)
    k = _rope(k, cos, sin)

    kcat = jnp.concatenate([kv_scr[0], k], axis=0)
    vcat = jnp.concatenate([kv_scr[1], v], axis=0)

    lane = _iota((1, LANES), 1)
    lo = lane < SWA_HD
    krot = pltpu.roll(kcat, SWA_HD, 1)
    vrot = pltpu.roll(vcat, SWA_HD, 1)
    kdup = [jnp.where(lo, kcat, krot).astype(BF16), jnp.where(lo, krot, kcat).astype(BF16)]
    vlo = [jnp.where(lo, vcat, 0.0).astype(BF16), jnp.where(lo, vrot, 0.0).astype(BF16)]
    vhi = [jnp.where(lo, 0.0, vrot).astype(BF16), jnp.where(lo, 0.0, vcat).astype(BF16)]

    rows4 = group * BLOCK
    cur_mask = (_iota((rows4, BLOCK), 0) % BLOCK) >= _iota((rows4, BLOCK), 1)
    neg_inf = jnp.float32(-jnp.inf)
    units = [(blk, kh) for blk in range(n_blk) for kh in range(SWA_KV)]

    lhs = []
    for blk, kh in units:
        rows = slice(blk * BLOCK, (blk + 1) * BLOCK)
        s0 = q[rows, (2 * kh) * LANES:(2 * kh + 1) * LANES]
        s1 = q[rows, (2 * kh + 1) * LANES:(2 * kh + 2) * LANES]
        lhs.append(jnp.concatenate([jnp.where(lo, s0, 0.0), jnp.where(lo, s1, 0.0),
                                    jnp.where(lo, 0.0, s0), jnp.where(lo, 0.0, s1)],
                                   axis=0).astype(BF16))
    logits = [_dot_nt(lhs[i], kdup[kh][blk * BLOCK:(blk + 2) * BLOCK])
              for i, (blk, kh) in enumerate(units)]
    yield
    n_unit = len(units)
    sinks = [jnp.concatenate(
        [jnp.broadcast_to(sinks_ref[group * kh + 2 * j + half:group * kh + 2 * j + half + 1, :],
                          (BLOCK, LANES)) for half in range(2) for j in range(2)], axis=0)
        for kh in range(SWA_KV)]
    act = []
    for i, (blk, kh) in enumerate(units):
        l_prev = logits[i][:, :BLOCK]
        if blk == 0:
            l_prev = jnp.where(seq_start, neg_inf, l_prev)
        act.append(jnp.where(cur_mask, logits[i][:, BLOCK:], l_prev))
    m = [jnp.maximum(jnp.broadcast_to(jnp.max(act[i], axis=1, keepdims=True), act[i].shape),
                     sinks[units[i][1]]) for i in range(n_unit)]
    p = [jnp.exp(act[i] - m[i]) for i in range(n_unit)]
    rows_u = p[0].shape[0]
    sums = _dot(jnp.concatenate(p, axis=0), ones)
    den = [sums[i * rows_u:(i + 1) * rows_u] + jnp.exp(sinks[units[i][1]] - m[i])
           for i in range(n_unit)]
    for i in range(n_unit):
        pn = p[i] * (1.0 / den[i])
        probs_scr[i] = jnp.concatenate([jnp.where(cur_mask, 0.0, pn), jnp.where(cur_mask, pn, 0.0)],
                                       axis=1).astype(BF16)
    yield
    probs = [probs_scr[i] for i in range(len(units))]
    out_rows = []
    for blk in range(n_blk):
        slabs = []
        for kh in range(SWA_KV):
            i = blk * SWA_KV + kh
            keys = slice(blk * BLOCK, (blk + 2) * BLOCK)
            res = (_dot(probs[i][:2 * BLOCK], vlo[kh][keys])
                   + _dot(probs[i][2 * BLOCK:], vhi[kh][keys]))
            slabs += [res[:BLOCK], res[BLOCK:]]
        out_rows.append(jnp.concatenate(slabs, axis=1))
    kv_scr[0] = k[t - BLOCK:, :]
    kv_scr[1] = v[t - BLOCK:, :]
    yield jnp.concatenate(out_rows, axis=0)


def _chunk_cumsum(x):
    row = _iota(x.shape, 0) % DN_CHUNK
    shift = 1
    while shift < DN_CHUNK:
        x = x + jnp.where(row >= shift, pltpu.roll(x, shift, 0), 0.0)
        shift *= 2
    return x


def _head_sums(xs, ones_bd):
    t = xs[0].shape[0]
    n_half = xs[0].shape[1] // MXU_DIM
    stacked = jnp.concatenate([(x * x)[:, i * MXU_DIM:(i + 1) * MXU_DIM]
                               for x in xs for i in range(n_half)], axis=0)
    sums = _dot(stacked, ones_bd)
    return [jnp.concatenate([sums[(j * n_half + i) * t:(j * n_half + i + 1) * t]
                             for i in range(n_half)], axis=1) for j in range(len(xs))]


def _gated_deltanet(qkv, get_z, logits, convw_ref, alog, dtb, dnorm, ones_bd,
                    conv_scr, state_scr, fill):
    t = qkv.shape[0]
    c = DN_CHUNK
    n_chunk = t // c
    pad = SUBLANES
    heads = range(DN_HEADS)
    hs = [slice(h * DN_HD, (h + 1) * DN_HD) for h in heads]

    xa = jnp.concatenate([conv_scr[0:pad, :], qkv], axis=0)
    y = qkv * convw_ref[DN_CONV - 1:DN_CONV, :]
    for i in range(DN_CONV - 1):
        y = y + pltpu.roll(xa, DN_CONV - 1 - i, 0)[pad:, :] * convw_ref[i:i + 1, :]
    conv_scr[0:pad, :] = qkv[t - pad:, :]
    y = _silu(y)

    q = y[:, :MIX]
    k = y[:, MIX:2 * MIX]
    v = y[:, 2 * MIX:]
    fill("prologue")
    q_ss, k_ss = _head_sums([q, k], ones_bd)
    q = q * lax.rsqrt(q_ss + NORM_EPS) * (DN_HD ** -0.5)
    k = k * lax.rsqrt(k_ss + NORM_EPS)

    def spread(slab, lo):
        return jnp.concatenate([jnp.broadcast_to(slab[:, lo + j:lo + j + 1], (t, DN_HD))
                                for j in heads], axis=1)

    dl = DN_HEADS
    gc_c = _chunk_cumsum(-jnp.exp(alog) * _softplus(logits + dtb))
    last = [gc_c[(ci + 1) * c - 1:(ci + 1) * c, :] for ci in range(n_chunk)]
    g_last_c = jnp.concatenate([jnp.broadcast_to(r, (c, LANES)) for r in last], axis=0)
    beta = spread(jax.nn.sigmoid(logits), 0)
    gc = spread(gc_c, dl)
    egc = spread(jnp.exp(gc_c), dl)
    kb = k * beta
    vb = v * beta
    kbe = kb * egc
    qd = q * egc
    kd = k * spread(jnp.exp(g_last_c - gc_c), dl)
    c_dec = [jnp.exp(r) for r in last]

    g4_t = gc_c.T
    kd_t = [kd[:, hs[h]].T.astype(BF16) for h in heads]
    fill("norms")

    ri = _iota((t, t), 0)
    ci_ = _iota((t, t), 1)
    same = (ri // c) == (ci_ // c)
    incl = jnp.logical_and(same, ri >= ci_)
    strict = jnp.logical_and(same, ri > ci_)
    neg_inf = jnp.float32(-jnp.inf)

    aq = [_dot_nt(jnp.concatenate([kb[:, hs[h]], q[:, hs[h]]], axis=0), k[:, hs[h]])
          for h in heads]
    lower, attn, sol = [], [], []
    for h in heads:
        g_col = gc[:, hs[h]]
        diff = jnp.concatenate([g_col] * (t // DN_HD), axis=1) - g4_t[dl + h:dl + h + 1, :]
        decay = jnp.exp(jnp.where(incl, diff, neg_inf))
        lower.append(jnp.where(strict, aq[h][:t] * decay, 0.0).astype(BF16))
        attn.append((aq[h][t:] * decay).astype(BF16))
        sol.append(jnp.concatenate([vb[:, hs[h]], kbe[:, hs[h]]], axis=1))
    fill("decay")

    sol = [sol[h] - _dot(lower[h], sol[h]) for h in heads]
    power = [_dot(lower[h], lower[h]).astype(BF16) for h in heads]
    n_sq = 2
    while True:
        sol = [sol[h] + _dot(power[h], sol[h]) for h in heads]
        if 2 * n_sq >= c:
            break
        power = [_dot(power[h], power[h]).astype(BF16) for h in heads]
        n_sq *= 2

    row_chunk = _iota((2 * c, 1), 0) // c
    gn = []
    for h in heads:
        per_chunk = []
        for ci in range(n_chunk):
            pair = slice((ci // 2) * 2 * c, (ci // 2 + 1) * 2 * c)
            rhs = jnp.where(row_chunk == (ci % 2), sol[h][pair], 0.0)
            per_chunk.append(_dot(kd_t[h][:, pair], rhs))
        gn.append(per_chunk)

    au = [_dot(attn[h], sol[h]) for h in heads]
    q_eff = [qd[:, hs[h]] - au[h][:, DN_HD:] for h in heads]

    def scan_and_outputs():
        state = [state_scr[h] for h in heads]
        from_state = [[] for _ in heads]
        for ci in range(n_chunk):
            rows = slice(ci * c, (ci + 1) * c)
            fill(("scan", ci))
            both = [_dot(jnp.concatenate([q_eff[h][rows], gn[h][ci][:, DN_HD:]], axis=0), state[h])
                    for h in heads]
            for h in heads:
                from_state[h].append(both[h][:c])
            state = [state[h] * c_dec[ci][:, dl + h:dl + h + 1]
                     + (gn[h][ci][:, :DN_HD] - both[h][c:]) for h in heads]

        outs = [jnp.concatenate(from_state[h], axis=0) + au[h][:, :DN_HD] for h in heads]
        o = jnp.concatenate(outs, axis=1)
        o = (o * lax.rsqrt(_head_sums([o], ones_bd)[0] * (1.0 / DN_HD) + NORM_EPS) * dnorm
             * _silu(get_z()))
        for h in heads:
            state_scr[h] = state[h]
        return o

    return scan_and_outputs


def _mixer_kernel(x_ref, cos_ref, sin_ref, ones_ref, anorm_ref, wfront_ref, wtail_ref,
                  lng_ref, lnb_ref, sguw_ref, sgub_ref, sinks_ref, convw_ref, alog_ref, dtb_ref,
                  dnorm_ref, wbr_ref, wout_ref, wgu32_ref, wd32_ref, o_ref, wgu16_ref, wd16_ref,
                  kv_scr, conv_scr, state_scr, gate_scr, wgate_scr, u_scr, vn_scr, probs_scr):
    seq_start = pl.program_id(1) == 0

    @pl.when(jnp.logical_and(pl.program_id(0) == 0, seq_start))
    def _():
        for n in range(3):
            window = wtail_ref[:, n * D_MODEL:(n + 1) * D_MODEL + GATE_OFF].astype(F32)
            wgate_scr[:, n * D_MODEL:(n + 1) * D_MODEL] = window[:, GATE_OFF:].astype(BF16)

    @pl.when(seq_start)
    def _():
        kv_scr[...] = jnp.zeros_like(kv_scr)
        conv_scr[0:SUBLANES, :] = jnp.zeros((SUBLANES, 3 * MIX), F32)
        state_scr[...] = jnp.zeros_like(state_scr)

    x = x_ref[...]
    h = _rmsnorm(x, anorm_ref[...]).astype(BF16)

    def proj(w_ref, lo, width):
        return jnp.dot(h, w_ref[:, lo:lo + width], preferred_element_type=F32)

    def gate(n, lo=0, width=D_MODEL):
        return jax.nn.sigmoid(proj(wgate_scr, n * D_MODEL + lo, width))

    park = {}

    def fill(stage):
        if stage == "prologue":
            qkv_b = proj(wfront_ref, C_QKVB, MIX + 2 * SWA_KV * SWA_HD)
            park["swa"] = _sliding_window_attention(
                qkv_b[:, :MIX], qkv_b[:, MIX:MIX + LANES], qkv_b[:, MIX + LANES:],
                cos_ref[...], sin_ref[...], sinks_ref, ones_ref[0:LANES, 0:LANES], kv_scr,
                probs_scr, seq_start)
            park["uv"] = proj(wfront_ref, C_UV, 2 * MIX)
            gate_scr[0] = gate(0)
            next(park["swa"])
        elif stage == "norms":
            gate_scr[1] = gate(1)
        elif stage == "decay":
            park["z"] = proj(wfront_ref, C_Z, MIX)
            next(park["swa"])
            uv = park.pop("uv")
            _spatial_gating_prep(uv[:, :MIX], uv[:, MIX:], lng_ref[...], lnb_ref[...],
                                 u_scr, vn_scr)
        elif stage[0] == "scan":
            lo = stage[1] * 2 * LANES
            gate_scr[2, :, lo:lo + 2 * LANES] = gate(2, lo, 2 * LANES)

    dn_tail = _gated_deltanet(proj(wfront_ref, C_QKVC, 3 * MIX), lambda: park.pop("z"),
                              proj(wtail_ref, 0, LANES), convw_ref,
                              alog_ref[...], dtb_ref[...], dnorm_ref[...], ones_ref[...],
                              conv_scr, state_scr, fill)

    out_c = dn_tail()
    out_a = _spatial_gating_mix(u_scr, vn_scr, sguw_ref, sgub_ref)
    out_b = next(park.pop("swa"))
    merged = (gate_scr[0] * _dot(out_a, wbr_ref[0]) + gate_scr[1] * _dot(out_b, wbr_ref[1])
              + gate_scr[2] * _dot(out_c, wbr_ref[2]))
    o_ref[...] = x + _dot(merged, wout_ref[...])
    wgu16_ref[...] = wgu32_ref[...].astype(BF16)
    wd16_ref[...] = wd32_ref[...].astype(BF16)


def _ffn_kernel(x_ref, fnorm_ref, wgu_ref, wd_ref, final_ref, o_ref, *, apply_final_norm):
    x = x_ref[...]
    h = _rmsnorm(x, fnorm_ref[...]).astype(BF16)
    gu = jnp.dot(h, wgu_ref[...], preferred_element_type=F32)
    act = _silu(gu[:, :D_FF]) * gu[:, D_FF:]
    y = x + _dot(act, wd_ref[...])
    if apply_final_norm:
        y = _rmsnorm(y, final_ref[...])
    o_ref[...] = y


def _resident(shape_tail, layer):
    n = len(shape_tail)
    return pl.BlockSpec((None,) + tuple(shape_tail), lambda *_: (layer,) + (0,) * n,
                        pipeline_mode=pl.Buffered(1))


def _mixer_call(x, rope, params, layer):
    batch, seq, _ = x.shape
    t = MIXER_TILE
    n_t = seq // t
    tok = lambda w: pl.BlockSpec((None, t, w), lambda b, i: (b, i, 0))
    gu_rows = D_MODEL // (batch * n_t)
    wd_steps = D_FF // WD_CAST_ROWS
    assert gu_rows % 16 == 0 and wd_steps <= batch * n_t
    gu_slice = lambda b, i: (b * n_t + i, 0)
    wd_slice = lambda b, i: (jnp.minimum(b * n_t + i, wd_steps - 1), 0)
    in_specs = [
        tok(D_MODEL), tok(LANES), tok(LANES),
        pl.BlockSpec((MXU_DIM, MXU_DIM), lambda b, i: (0, 0), pipeline_mode=pl.Buffered(1)),
        _resident((1, D_MODEL), layer),
        _resident((D_MODEL, C_FRONT_END), layer),
        pl.BlockSpec((None, D_MODEL, C_FRONT_END), lambda *_: (layer, 0, 1),
                     pipeline_mode=pl.Buffered(1)),
        _resident((1, MIX), layer), _resident((1, MIX), layer),
        _resident((SGU_GROUPS, BLOCK, BLOCK), layer), _resident((BLOCK, SGU_GROUPS), layer),
        _resident((SWA_HEADS, LANES), layer),
        _resident((DN_CONV, 3 * MIX), layer),
        _resident((1, LANES), layer), _resident((1, LANES), layer), _resident((1, MIX), layer),
        _resident((3, MIX, D_MODEL), layer), _resident((D_MODEL, D_MODEL), layer),
        pl.BlockSpec((None, gu_rows, 2 * D_FF), lambda b, i: (layer,) + gu_slice(b, i)),
        pl.BlockSpec((None, WD_CAST_ROWS, D_MODEL), lambda b, i: (layer,) + wd_slice(b, i)),
    ]
    return pl.pallas_call(
        _mixer_kernel,
        out_shape=(jax.ShapeDtypeStruct(x.shape, F32),
                   jax.ShapeDtypeStruct((D_MODEL, 2 * D_FF), BF16),
                   jax.ShapeDtypeStruct((D_FF, D_MODEL), BF16)),
        grid=(batch, seq // t),
        in_specs=in_specs,
        out_specs=(tok(D_MODEL), pl.BlockSpec((gu_rows, 2 * D_FF), gu_slice),
                   pl.BlockSpec((WD_CAST_ROWS, D_MODEL), wd_slice)),
        scratch_shapes=[
            pltpu.VMEM((2, BLOCK, LANES), F32),
            pltpu.VMEM((SUBLANES, 3 * MIX), F32),
            pltpu.VMEM((DN_HEADS, DN_HD, DN_HD), F32),
            pltpu.VMEM((3, t, D_MODEL), F32),
            pltpu.VMEM((D_MODEL, 3 * D_MODEL), BF16),
            pltpu.VMEM((t, MIX), F32),
            pltpu.VMEM((t, MIX), BF16),
            pltpu.VMEM((t // BLOCK * SWA_KV, SWA_HEADS // SWA_KV * BLOCK, 2 * BLOCK), BF16),
        ],
        compiler_params=pltpu.CompilerParams(
            dimension_semantics=("arbitrary", "arbitrary"),
            vmem_limit_bytes=VMEM_LIMIT_BYTES),
        name=f"mixer_l{layer}",
    )(x, *rope, params["head_ones"], params["attn_norm"], params["w_in"],
      params["w_in"], params["sgu_ln_g"], params["sgu_ln_b"],
      params["sgu_w"], params["sgu_b"], params["attn_sinks"], params["dn_conv_w"],
      params["dn_a_log"], params["dn_dt_bias"], params["dn_norm"], params["w_branch"],
      params["w_out"], params["w_gate_up"], params["w_down"])


def _ffn_call(x, w_gate_up16, w_down16, params, layer, apply_final_norm):
    batch, seq, _ = x.shape
    t = FFN_TILE
    tok = pl.BlockSpec((None, t, D_MODEL), lambda b, i: (b, i, 0))
    return pl.pallas_call(
        functools.partial(_ffn_kernel, apply_final_norm=apply_final_norm),
        out_shape=jax.ShapeDtypeStruct(x.shape, F32),
        grid=(batch, seq // t),
        in_specs=[
            tok,
            _resident((1, D_MODEL), layer),
            pl.BlockSpec((D_MODEL, 2 * D_FF), lambda b, i: (0, 0), pipeline_mode=pl.Buffered(1)),
            pl.BlockSpec((D_FF, D_MODEL), lambda b, i: (0, 0), pipeline_mode=pl.Buffered(1)),
            pl.BlockSpec((1, D_MODEL), lambda b, i: (0, 0)),
        ],
        out_specs=tok,
        compiler_params=pltpu.CompilerParams(
            dimension_semantics=("arbitrary", "arbitrary"),
            vmem_limit_bytes=VMEM_LIMIT_BYTES),
        name=f"ffn_l{layer}",
    )(x, params["ffn_norm"], w_gate_up16, w_down16, params["final_norm"])


def _rope_tables(positions):
    half = ROPE_DIM // 2
    inv_freq = ROPE_THETA ** (-jnp.arange(0, ROPE_DIM, 2, dtype=F32) / ROPE_DIM)
    ang = positions.astype(F32)[..., None] * inv_freq
    cos, sin = lax.optimization_barrier((jnp.cos(ang), jnp.sin(ang)))
    rotary = (jnp.arange(LANES) % SWA_HD) < ROPE_DIM
    reps = (1,) * (ang.ndim - 1) + (LANES // half,)
    return jnp.where(rotary, jnp.tile(cos, reps), 1.0), jnp.where(rotary, jnp.tile(sin, reps), 0.0)


def _pack_params(attn_norm, w_in, sgu_ln_g, sgu_ln_b, sgu_w, sgu_b, attn_sinks, dn_conv_w,
                 dn_a_log, dn_dt_bias, dn_norm, w_branch, w_out, ffn_norm, w_gate_up, w_down,
                 final_norm):
    depth = w_in.shape[0]
    decay_lanes = lambda a: jnp.pad(a, ((0, 0), (DN_HEADS, LANES - 2 * DN_HEADS)))[:, None, :]
    lane_head = jnp.arange(MXU_DIM) // DN_HD
    w_in16 = w_in.astype(BF16)
    return {
        "head_ones": (lane_head[:, None] == lane_head[None, :]).astype(BF16),
        "attn_norm": attn_norm[:, None, :],
        "w_in": w_in16,
        "sgu_ln_g": sgu_ln_g[:, None, :],
        "sgu_ln_b": sgu_ln_b[:, None, :],
        "sgu_w": sgu_w,
        "sgu_b": jnp.swapaxes(sgu_b, 1, 2),
        "attn_sinks": jnp.broadcast_to(attn_sinks[:, :, None], (depth, SWA_HEADS, LANES)),
        "dn_conv_w": dn_conv_w,
        "dn_a_log": decay_lanes(dn_a_log),
        "dn_dt_bias": decay_lanes(dn_dt_bias),
        "dn_norm": jnp.tile(dn_norm, (1, DN_HEADS))[:, None, :],
        "w_branch": w_branch.astype(BF16),
        "w_out": w_out.astype(BF16),
        "ffn_norm": ffn_norm[:, None, :],
        "w_gate_up": w_gate_up,
        "w_down": w_down,
        "final_norm": final_norm[None, :],
    }


def kernel(x, positions, attn_norm, w_in, sgu_ln_g, sgu_ln_b, sgu_w, sgu_b, attn_sinks,
           dn_conv_w, dn_a_log, dn_dt_bias, dn_norm, w_branch, w_out, ffn_norm, w_gate_up,
           w_down, final_norm):
    assert MIXER_TILE == MXU_DIM == 4 * DN_CHUNK and GATE_OFF + 3 * D_MODEL <= C_FRONT_END
    assert x.shape[1] % MIXER_TILE == 0 and x.shape[1] % FFN_TILE == 0
    assert x.shape[2] == D_MODEL
    assert w_in.shape[2] == C_FRONT_END + GATE_OFF + 3 * D_MODEL
    depth = w_in.shape[0]
    params = _pack_params(attn_norm, w_in, sgu_ln_g, sgu_ln_b, sgu_w, sgu_b, attn_sinks,
                          dn_conv_w, dn_a_log, dn_dt_bias, dn_norm, w_branch, w_out, ffn_norm,
                          w_gate_up, w_down, final_norm)
    rope = _rope_tables(positions)
    for layer in range(depth):
        x, w_gate_up16, w_down16 = _mixer_call(x, rope, params, layer)
        x = _ffn_call(x, w_gate_up16, w_down16, params, layer,
                      apply_final_norm=(layer == depth - 1))
    return x
```

```python
import functools

import jax
import jax.numpy as jnp
from jax import lax
from jax.experimental import pallas as pl
from jax.experimental.pallas import tpu as pltpu

D_MODEL = 1024
MIX = 512
NORM_EPS = 1e-6

SGU_GROUPS = 4
BLOCK = 128

SWA_HEADS = 8
SWA_KV = 2
SWA_HD = 64
ROPE_THETA = 500000.0
ROPE_DIM = 16

DN_HEADS = 4
DN_HD = 128
DN_CONV = 4
DN_CHUNK = 64

D_FF = 2816

LANES = 128
SUBLANES = 8
MXU_DIM = 256

C_UV = 0
C_QKVB = C_UV + 2 * MIX
C_QKVC = C_QKVB + MIX + 2 * SWA_KV * SWA_HD
C_Z = C_QKVC + 3 * MIX
C_FRONT_END = C_Z + MIX
GATE_OFF = 2 * DN_HEADS

MIXER_TILE = 256
FFN_TILE = 512
WD_CAST_ROWS = 64
VMEM_CAPACITY_BYTES = 64 * 1024 * 1024
VMEM_LIMIT_BYTES = VMEM_CAPACITY_BYTES // 8 * 7

BF16 = jnp.bfloat16
F32 = jnp.float32
_GELU_C = 0.7978845608028654


def _dot(a, b):
    return jnp.dot(a.astype(BF16), b.astype(BF16), preferred_element_type=F32)


def _dot_nt(a, b):
    return lax.dot_general(a.astype(BF16), b.astype(BF16), (((1,), (1,)), ((), ())),
                           preferred_element_type=F32)


def _rmsnorm(x, g):
    return x * lax.rsqrt(jnp.mean(x * x, axis=-1, keepdims=True) + NORM_EPS) * g


def _gelu_tanh(x):
    inner = x * (_GELU_C + (_GELU_C * 0.044715) * (x * x))
    return (0.5 * x) * (1.0 + jnp.tanh(inner))


def _silu(x):
    return x * jax.nn.sigmoid(x)


def _softplus(x):
    return jnp.maximum(x, 0.0) + jnp.log(1.0 + jnp.exp(-jnp.abs(x)))


def _iota(shape, dim):
    return lax.broadcasted_iota(jnp.int32, shape, dim)


def _spatial_gating_prep(u_pre, v_pre, ln_g, ln_b, u_scr, vn_scr):
    u_scr[...] = _gelu_tanh(u_pre)
    v = _gelu_tanh(v_pre)
    vc = v - jnp.mean(v, axis=-1, keepdims=True)
    vn_scr[...] = (vc * lax.rsqrt(jnp.mean(vc * vc, axis=-1, keepdims=True) + NORM_EPS) * ln_g
                   + ln_b).astype(BF16)


def _spatial_gating_mix(u_scr, vn_scr, sguw_ref, sgub_ref):
    u = u_scr[...]
    vn = vn_scr[...]
    t = u.shape[0]
    causal = _iota((BLOCK, BLOCK), 0) >= _iota((BLOCK, BLOCK), 1)
    groups = []
    for g in range(SGU_GROUPS):
        w = jnp.where(causal, sguw_ref[g], 0.0).astype(BF16)
        bias = sgub_ref[:, g:g + 1]
        cols = slice(g * LANES, (g + 1) * LANES)
        mixed = [_dot(w, vn[blk * BLOCK:(blk + 1) * BLOCK, cols]) + bias
                 for blk in range(t // BLOCK)]
        groups.append(u[:, cols] * jnp.concatenate(mixed, axis=0))
    return jnp.concatenate(groups, axis=1)


def _rope(x, cos, sin):
    half = ROPE_DIM // 2
    first_half = (_iota((1, LANES), 1) % SWA_HD) < half
    sin_a = jnp.where(first_half, -sin, 0.0)
    sin_b = jnp.where(first_half, 0.0, sin)
    n = x.shape[1] // LANES
    if n > 1:
        cos = jnp.concatenate([cos] * n, axis=1)
        sin_a = jnp.concatenate([sin_a] * n, axis=1)
        sin_b = jnp.concatenate([sin_b] * n, axis=1)
    width = x.shape[1]
    return (x * cos + pltpu.roll(x, width - half, 1) * sin_a + pltpu.roll(x, half, 1) * sin_b)


def _sliding_window_attention(q, k, v, cos, sin, sinks_ref, ones, kv_scr, probs_scr, seq_start):
    t = q.shape[0]
    n_blk = t // BLOCK
    group = SWA_HEADS // SWA_KV
    q = _rope(q, cos, sin) * (SWA_HD ** -0.5)
    k = _rope(k, cos, sin)

    kcat = jnp.concatenate([kv_scr[0], k], axis=0)
    vcat = jnp.concatenate([kv_scr[1], v], axis=0)

    lane = _iota((1, LANES), 1)
    lo = lane < SWA_HD
    krot = pltpu.roll(kcat, SWA_HD, 1)
    vrot = pltpu.roll(vcat, SWA_HD, 1)
    kdup = [jnp.where(lo, kcat, krot).astype(BF16), jnp.where(lo, krot, kcat).astype(BF16)]
    vlo = [jnp.where(lo, vcat, 0.0).astype(BF16), jnp.where(lo, vrot, 0.0).astype(BF16)]
    vhi = [jnp.where(lo, 0.0, vrot).astype(BF16), jnp.where(lo, 0.0, vcat).astype(BF16)]

    rows4 = group * BLOCK
    cur_mask = (_iota((rows4, BLOCK), 0) % BLOCK) >= _iota((rows4, BLOCK), 1)
    neg_inf = jnp.float32(-jnp.inf)
    units = [(blk, kh) for blk in range(n_blk) for kh in range(SWA_KV)]

    lhs = []
    for blk, kh in units:
        rows = slice(blk * BLOCK, (blk + 1) * BLOCK)
        s0 = q[rows, (2 * kh) * LANES:(2 * kh + 1) * LANES]
        s1 = q[rows, (2 * kh + 1) * LANES:(2 * kh + 2) * LANES]
        lhs.append(jnp.concatenate([jnp.where(lo, s0, 0.0), jnp.where(lo, s1, 0.0),
                                    jnp.where(lo, 0.0, s0), jnp.where(lo, 0.0, s1)],
                                   axis=0).astype(BF16))
    logits = [_dot_nt(lhs[i], kdup[kh][blk * BLOCK:(blk + 2) * BLOCK])
              for i, (blk, kh) in enumerate(units)]
    yield
    n_unit = len(units)
    sinks = [jnp.concatenate(
        [jnp.broadcast_to(sinks_ref[group * kh + 2 * j + half:group * kh + 2 * j + half + 1, :],
                          (BLOCK, LANES)) for half in range(2) for j in range(2)], axis=0)
        for kh in range(SWA_KV)]
    act = []
    for i, (blk, kh) in enumerate(units):
        l_prev = logits[i][:, :BLOCK]
        if blk == 0:
            l_prev = jnp.where(seq_start, neg_inf, l_prev)
        act.append(jnp.where(cur_mask, logits[i][:, BLOCK:], l_prev))
    m = [jnp.maximum(jnp.broadcast_to(jnp.max(act[i], axis=1, keepdims=True), act[i].shape),
                     sinks[units[i][1]]) for i in range(n_unit)]
    p = [jnp.exp(act[i] - m[i]) for i in range(n_unit)]
    rows_u = p[0].shape[0]
    sums = _dot(jnp.concatenate(p, axis=0), ones)
    den = [sums[i * rows_u:(i + 1) * rows_u] + jnp.exp(sinks[units[i][1]] - m[i])
           for i in range(n_unit)]
    for i in range(n_unit):
        pn = p[i] * (1.0 / den[i])
        probs_scr[i] = jnp.concatenate([jnp.where(cur_mask, 0.0, pn), jnp.where(cur_mask, pn, 0.0)],
                                       axis=1).astype(BF16)
    yield
    probs = [probs_scr[i] for i in range(len(units))]
    out_rows = []
    for blk in range(n_blk):
        slabs = []
        for kh in range(SWA_KV):
            i = blk * SWA_KV + kh
            keys = slice(blk * BLOCK, (blk + 2) * BLOCK)
            res = (_dot(probs[i][:2 * BLOCK], vlo[kh][keys])
                   + _dot(probs[i][2 * BLOCK:], vhi[kh][keys]))
            slabs += [res[:BLOCK], res[BLOCK:]]
        out_rows.append(jnp.concatenate(slabs, axis=1))
    kv_scr[0] = k[t - BLOCK:, :]
    kv_scr[1] = v[t - BLOCK:, :]
    yield jnp.concatenate(out_rows, axis=0)


def _chunk_cumsum(x):
    row = _iota(x.shape, 0) % DN_CHUNK
    shift = 1
    while shift < DN_CHUNK:
        x = x + jnp.where(row >= shift, pltpu.roll(x, shift, 0), 0.0)
        shift *= 2
    return x


def _head_sums(xs, ones_bd):
    t = xs[0].shape[0]
    n_half = xs[0].shape[1] // MXU_DIM
    stacked = jnp.concatenate([(x * x)[:, i * MXU_DIM:(i + 1) * MXU_DIM]
                               for x in xs for i in range(n_half)], axis=0)
    sums = _dot(stacked, ones_bd)
    return [jnp.concatenate([sums[(j * n_half + i) * t:(j * n_half + i + 1) * t]
                             for i in range(n_half)], axis=1) for j in range(len(xs))]


def _gated_deltanet(qkv, get_z, logits, convw_ref, alog, dtb, dnorm, ones_bd,
                    conv_scr, state_scr, fill):
    t = qkv.shape[0]
    c = DN_CHUNK
    n_chunk = t // c
    pad = SUBLANES
    heads = range(DN_HEADS)
    hs = [slice(h * DN_HD, (h + 1) * DN_HD) for h in heads]

    xa = jnp.concatenate([conv_scr[0:pad, :], qkv], axis=0)
    y = qkv * convw_ref[DN_CONV - 1:DN_CONV, :]
    for i in range(DN_CONV - 1):
        y = y + pltpu.roll(xa, DN_CONV - 1 - i, 0)[pad:, :] * convw_ref[i:i + 1, :]
    conv_scr[0:pad, :] = qkv[t - pad:, :]
    y = _silu(y)

    q = y[:, :MIX]
    k = y[:, MIX:2 * MIX]
    v = y[:, 2 * MIX:]
    fill("prologue")
    q_ss, k_ss = _head_sums([q, k], ones_bd)
    q = q * lax.rsqrt(q_ss + NORM_EPS) * (DN_HD ** -0.5)
    k = k * lax.rsqrt(k_ss + NORM_EPS)

    def spread(slab, lo):
        return jnp.concatenate([jnp.broadcast_to(slab[:, lo + j:lo + j + 1], (t, DN_HD))
                                for j in heads], axis=1)

    dl = DN_HEADS
    gc_c = _chunk_cumsum(-jnp.exp(alog) * _softplus(logits + dtb))
    last = [gc_c[(ci + 1) * c - 1:(ci + 1) * c, :] for ci in range(n_chunk)]
    g_last_c = jnp.concatenate([jnp.broadcast_to(r, (c, LANES)) for r in last], axis=0)
    beta = spread(jax.nn.sigmoid(logits), 0)
    gc = spread(gc_c, dl)
    egc = spread(jnp.exp(gc_c), dl)
    kb = k * beta
    vb = v * beta
    kbe = kb * egc
    qd = q * egc
    kd = k * spread(jnp.exp(g_last_c - gc_c), dl)
    c_dec = [jnp.exp(r) for r in last]

    g4_t = gc_c.T
    kd_t = [kd[:, hs[h]].T.astype(BF16) for h in heads]
    fill("norms")

    ri = _iota((t, t), 0)
    ci_ = _iota((t, t), 1)
    same = (ri // c) == (ci_ // c)
    incl = jnp.logical_and(same, ri >= ci_)
    strict = jnp.logical_and(same, ri > ci_)
    neg_inf = jnp.float32(-jnp.inf)

    aq = [_dot_nt(jnp.concatenate([kb[:, hs[h]], q[:, hs[h]]], axis=0), k[:, hs[h]])
          for h in heads]
    lower, attn, sol = [], [], []
    for h in heads:
        g_col = gc[:, hs[h]]
        diff = jnp.concatenate([g_col] * (t // DN_HD), axis=1) - g4_t[dl + h:dl + h + 1, :]
        decay = jnp.exp(jnp.where(incl, diff, neg_inf))
        lower.append(jnp.where(strict, aq[h][:t] * decay, 0.0).astype(BF16))
        attn.append((aq[h][t:] * decay).astype(BF16))
        sol.append(jnp.concatenate([vb[:, hs[h]], kbe[:, hs[h]]], axis=1))
    fill("decay")

    sol = [sol[h] - _dot(lower[h], sol[h]) for h in heads]
    power = [_dot(lower[h], lower[h]).astype(BF16) for h in heads]
    n_sq = 2
    while True:
        sol = [sol[h] + _dot(power[h], sol[h]) for h in heads]
        if 2 * n_sq >= c:
            break
        power = [_dot(power[h], power[h]).astype(BF16) for h in heads]
        n_sq *= 2

    row_chunk = _iota((2 * c, 1), 0) // c
    gn = []
    for h in heads:
        per_chunk = []
        for ci in range(n_chunk):
            pair = slice((ci // 2) * 2 * c, (ci // 2 + 1) * 2 * c)
            rhs = jnp.where(row_chunk == (ci % 2), sol[h][pair], 0.0)
            per_chunk.append(_dot(kd_t[h][:, pair], rhs))
        gn.append(per_chunk)

    au = [_dot(attn[h], sol[h]) for h in heads]
    q_eff = [qd[:, hs[h]] - au[h][:, DN_HD:] for h in heads]

    def scan_and_outputs():
        state = [state_scr[h] for h in heads]
        from_state = [[] for _ in heads]
        for ci in range(n_chunk):
            rows = slice(ci * c, (ci + 1) * c)
            fill(("scan", ci))
            both = [_dot(jnp.concatenate([q_eff[h][rows], gn[h][ci][:, DN_HD:]], axis=0), state[h])
                    for h in heads]
            for h in heads:
                from_state[h].append(both[h][:c])
            state = [state[h] * c_dec[ci][:, dl + h:dl + h + 1]
                     + (gn[h][ci][:, :DN_HD] - both[h][c:]) for h in heads]

        outs = [jnp.concatenate(from_state[h], axis=0) + au[h][:, :DN_HD] for h in heads]
        o = jnp.concatenate(outs, axis=1)
        o = (o * lax.rsqrt(_head_sums([o], ones_bd)[0] * (1.0 / DN_HD) + NORM_EPS) * dnorm
             * _silu(get_z()))
        for h in heads:
            state_scr[h] = state[h]
        return o

    return scan_and_outputs


def _mixer_kernel(x_ref, cos_ref, sin_ref, ones_ref, anorm_ref, wfront_ref, wtail_ref,
                  lng_ref, lnb_ref, sguw_ref, sgub_ref, sinks_ref, convw_ref, alog_ref, dtb_ref,
                  dnorm_ref, wbr_ref, wout_ref, wgu32_ref, wd32_ref, o_ref, wgu16_ref, wd16_ref,
                  kv_scr, conv_scr, state_scr, gate_scr, wgate_scr, wbr16_scr, wout16_scr, u_scr,
                  vn_scr, probs_scr):
    seq_start = pl.program_id(1) == 0

    @pl.when(jnp.logical_and(pl.program_id(0) == 0, seq_start))
    def _():
        for n in range(3):
            window = wtail_ref[:, n * D_MODEL:(n + 1) * D_MODEL + GATE_OFF].astype(F32)
            wgate_scr[:, n * D_MODEL:(n + 1) * D_MODEL] = window[:, GATE_OFF:].astype(BF16)
        for n in range(3):
            wbr16_scr[n] = wbr_ref[n].astype(BF16)
        wout16_scr[...] = wout_ref[...].astype(BF16)

    @pl.when(seq_start)
    def _():
        kv_scr[...] = jnp.zeros_like(kv_scr)
        conv_scr[0:SUBLANES, :] = jnp.zeros((SUBLANES, 3 * MIX), F32)
        state_scr[...] = jnp.zeros_like(state_scr)

    x = x_ref[...]
    h = _rmsnorm(x, anorm_ref[...]).astype(BF16)

    def proj(w_ref, lo, width):
        return jnp.dot(h, w_ref[:, lo:lo + width], preferred_element_type=F32)

    def gate(n, lo=0, width=D_MODEL):
        return jax.nn.sigmoid(proj(wgate_scr, n * D_MODEL + lo, width))

    park = {}

    def fill(stage):
        if stage == "prologue":
            qkv_b = proj(wfront_ref, C_QKVB, MIX + 2 * SWA_KV * SWA_HD)
            park["swa"] = _sliding_window_attention(
                qkv_b[:, :MIX], qkv_b[:, MIX:MIX + LANES], qkv_b[:, MIX + LANES:],
                cos_ref[...], sin_ref[...], sinks_ref, ones_ref[0:LANES, 0:LANES], kv_scr,
                probs_scr, seq_start)
            park["uv"] = proj(wfront_ref, C_UV, 2 * MIX)
            gate_scr[0] = gate(0)
            next(park["swa"])
        elif stage == "norms":
            gate_scr[1] = gate(1)
        elif stage == "decay":
            park["z"] = proj(wfront_ref, C_Z, MIX)
            next(park["swa"])
            uv = park.pop("uv")
            _spatial_gating_prep(uv[:, :MIX], uv[:, MIX:], lng_ref[...], lnb_ref[...],
                                 u_scr, vn_scr)
        elif stage[0] == "scan":
            lo = stage[1] * 2 * LANES
            gate_scr[2, :, lo:lo + 2 * LANES] = gate(2, lo, 2 * LANES)

    dn_tail = _gated_deltanet(proj(wfront_ref, C_QKVC, 3 * MIX), lambda: park.pop("z"),
                              proj(wtail_ref, 0, LANES), convw_ref,
                              alog_ref[...], dtb_ref[...], dnorm_ref[...], ones_ref[...],
                              conv_scr, state_scr, fill)

    out_c = dn_tail()
    out_a = _spatial_gating_mix(u_scr, vn_scr, sguw_ref, sgub_ref)
    out_b = next(park.pop("swa"))
    merged = (gate_scr[0] * _dot(out_a, wbr16_scr[0]) + gate_scr[1] * _dot(out_b, wbr16_scr[1])
              + gate_scr[2] * _dot(out_c, wbr16_scr[2]))
    o_ref[...] = x + _dot(merged, wout16_scr[...])
    wgu16_ref[...] = wgu32_ref[...].astype(BF16)
    wd16_ref[...] = wd32_ref[...].astype(BF16)


def _ffn_kernel(x_ref, fnorm_ref, wgu_ref, wd_ref, final_ref, o_ref, *, apply_final_norm):
    x = x_ref[...]
    h = _rmsnorm(x, fnorm_ref[...]).astype(BF16)
    gu = jnp.dot(h, wgu_ref[...], preferred_element_type=F32)
    act = _silu(gu[:, :D_FF]) * gu[:, D_FF:]
    y = x + _dot(act, wd_ref[...])
    if apply_final_norm:
        y = _rmsnorm(y, final_ref[...])
    o_ref[...] = y


def _resident(shape_tail, layer):
    n = len(shape_tail)
    return pl.BlockSpec((None,) + tuple(shape_tail), lambda *_: (layer,) + (0,) * n,
                        pipeline_mode=pl.Buffered(1))


def _mixer_call(x, rope, params, layer):
    batch, seq, _ = x.shape
    t = MIXER_TILE
    n_t = seq // t
    tok = lambda w: pl.BlockSpec((None, t, w), lambda b, i: (b, i, 0))
    gu_rows = D_MODEL // (batch * n_t)
    wd_steps = D_FF // WD_CAST_ROWS
    assert gu_rows % 16 == 0 and wd_steps <= batch * n_t
    gu_slice = lambda b, i: (b * n_t + i, 0)
    wd_slice = lambda b, i: (jnp.minimum(b * n_t + i, wd_steps - 1), 0)
    in_specs = [
        tok(D_MODEL), tok(LANES), tok(LANES),
        pl.BlockSpec((MXU_DIM, MXU_DIM), lambda b, i: (0, 0), pipeline_mode=pl.Buffered(1)),
        _resident((1, D_MODEL), layer),
        _resident((D_MODEL, C_FRONT_END), layer),
        pl.BlockSpec((None, D_MODEL, C_FRONT_END), lambda *_: (layer, 0, 1),
                     pipeline_mode=pl.Buffered(1)),
        _resident((1, MIX), layer), _resident((1, MIX), layer),
        _resident((SGU_GROUPS, BLOCK, BLOCK), layer), _resident((BLOCK, SGU_GROUPS), layer),
        _resident((SWA_HEADS, LANES), layer),
        _resident((DN_CONV, 3 * MIX), layer),
        _resident((1, LANES), layer), _resident((1, LANES), layer), _resident((1, MIX), layer),
        _resident((3, MIX, D_MODEL), layer), _resident((D_MODEL, D_MODEL), layer),
        pl.BlockSpec((None, gu_rows, 2 * D_FF), lambda b, i: (layer,) + gu_slice(b, i)),
        pl.BlockSpec((None, WD_CAST_ROWS, D_MODEL), lambda b, i: (layer,) + wd_slice(b, i)),
    ]
    return pl.pallas_call(
        _mixer_kernel,
        out_shape=(jax.ShapeDtypeStruct(x.shape, F32),
                   jax.ShapeDtypeStruct((D_MODEL, 2 * D_FF), BF16),
                   jax.ShapeDtypeStruct((D_FF, D_MODEL), BF16)),
        grid=(batch, seq // t),
        in_specs=in_specs,
        out_specs=(tok(D_MODEL), pl.BlockSpec((gu_rows, 2 * D_FF), gu_slice),
                   pl.BlockSpec((WD_CAST_ROWS, D_MODEL), wd_slice)),
        scratch_shapes=[
            pltpu.VMEM((2, BLOCK, LANES), F32),
            pltpu.VMEM((SUBLANES, 3 * MIX), F32),
            pltpu.VMEM((DN_HEADS, DN_HD, DN_HD), F32),
            pltpu.VMEM((3, t, D_MODEL), F32),
            pltpu.VMEM((D_MODEL, 3 * D_MODEL), BF16),
            pltpu.VMEM((3, MIX, D_MODEL), BF16),
            pltpu.VMEM((D_MODEL, D_MODEL), BF16),
            pltpu.VMEM((t, MIX), F32),
            pltpu.VMEM((t, MIX), BF16),
            pltpu.VMEM((t // BLOCK * SWA_KV, SWA_HEADS // SWA_KV * BLOCK, 2 * BLOCK), BF16),
        ],
        compiler_params=pltpu.CompilerParams(
            dimension_semantics=("arbitrary", "arbitrary"),
            vmem_limit_bytes=VMEM_LIMIT_BYTES),
        name=f"mixer_l{layer}",
    )(x, *rope, params["head_ones"], params["attn_norm"], params["w_in"],
      params["w_in"], params["sgu_ln_g"], params["sgu_ln_b"],
      params["sgu_w"], params["sgu_b"], params["attn_sinks"], params["dn_conv_w"],
      params["dn_a_log"], params["dn_dt_bias"], params["dn_norm"], params["w_branch"],
      params["w_out"], params["w_gate_up"], params["w_down"])


def _ffn_call(x, w_gate_up16, w_down16, params, layer, apply_final_norm):
    batch, seq, _ = x.shape
    t = FFN_TILE
    tok = pl.BlockSpec((None, t, D_MODEL), lambda b, i: (b, i, 0))
    return pl.pallas_call(
        functools.partial(_ffn_kernel, apply_final_norm=apply_final_norm),
        out_shape=jax.ShapeDtypeStruct(x.shape, F32),
        grid=(batch, seq // t),
        in_specs=[
            tok,
            _resident((1, D_MODEL), layer),
            pl.BlockSpec((D_MODEL, 2 * D_FF), lambda b, i: (0, 0), pipeline_mode=pl.Buffered(1)),
            pl.BlockSpec((D_FF, D_MODEL), lambda b, i: (0, 0), pipeline_mode=pl.Buffered(1)),
            pl.BlockSpec((1, D_MODEL), lambda b, i: (0, 0)),
        ],
        out_specs=tok,
        compiler_params=pltpu.CompilerParams(
            dimension_semantics=("arbitrary", "arbitrary"),
            vmem_limit_bytes=VMEM_LIMIT_BYTES),
        name=f"ffn_l{layer}",
    )(x, params["ffn_norm"], w_gate_up16, w_down16, params["final_norm"])


def _rope_tables(positions):
    half = ROPE_DIM // 2
    inv_freq = ROPE_THETA ** (-jnp.arange(0, ROPE_DIM, 2, dtype=F32) / ROPE_DIM)
    ang = positions.astype(F32)[..., None] * inv_freq
    cos, sin = lax.optimization_barrier((jnp.cos(ang), jnp.sin(ang)))
    rotary = (jnp.arange(LANES) % SWA_HD) < ROPE_DIM
    reps = (1,) * (ang.ndim - 1) + (LANES // half,)
    return jnp.where(rotary, jnp.tile(cos, reps), 1.0), jnp.where(rotary, jnp.tile(sin, reps), 0.0)


def _pack_params(attn_norm, w_in, sgu_ln_g, sgu_ln_b, sgu_w, sgu_b, attn_sinks, dn_conv_w,
                 dn_a_log, dn_dt_bias, dn_norm, w_branch, w_out, ffn_norm, w_gate_up, w_down,
                 final_norm):
    depth = w_in.shape[0]
    decay_lanes = lambda a: jnp.pad(a, ((0, 0), (DN_HEADS, LANES - 2 * DN_HEADS)))[:, None, :]
    lane_head = jnp.arange(MXU_DIM) // DN_HD
    w_in16 = w_in.astype(BF16)
    return {
        "head_ones": (lane_head[:, None] == lane_head[None, :]).astype(BF16),
        "attn_norm": attn_norm[:, None, :],
        "w_in": w_in16,
        "sgu_ln_g": sgu_ln_g[:, None, :],
        "sgu_ln_b": sgu_ln_b[:, None, :],
        "sgu_w": sgu_w,
        "sgu_b": jnp.swapaxes(sgu_b, 1, 2),
        "attn_sinks": jnp.broadcast_to(attn_sinks[:, :, None], (depth, SWA_HEADS, LANES)),
        "dn_conv_w": dn_conv_w,
        "dn_a_log": decay_lanes(dn_a_log),
        "dn_dt_bias": decay_lanes(dn_dt_bias),
        "dn_norm": jnp.tile(dn_norm, (1, DN_HEADS))[:, None, :],
        "w_branch": w_branch,
        "w_out": w_out,
        "ffn_norm": ffn_norm[:, None, :],
        "w_gate_up": w_gate_up,
        "w_down": w_down,
        "final_norm": final_norm[None, :],
    }


def kernel(x, positions, attn_norm, w_in, sgu_ln_g, sgu_ln_b, sgu_w, sgu_b, attn_sinks,
           dn_conv_w, dn_a_log, dn_dt_bias, dn_norm, w_branch, w_out, ffn_norm, w_gate_up,
           w_down, final_norm):
    assert MIXER_TILE == MXU_DIM == 4 * DN_CHUNK and GATE_OFF + 3 * D_MODEL <= C_FRONT_END
    assert x.shape[1] % MIXER_TILE == 0 and x.shape[1] % FFN_TILE == 0
    assert x.shape[2] == D_MODEL
    assert w_in.shape[2] == C_FRONT_END + GATE_OFF + 3 * D_MODEL
    depth = w_in.shape[0]
    params = _pack_params(attn_norm, w_in, sgu_ln_g, sgu_ln_b, sgu_w, sgu_b, attn_sinks,
                          dn_conv_w, dn_a_log, dn_dt_bias, dn_norm, w_branch, w_out, ffn_norm,
                          w_gate_up, w_down, final_norm)
    rope = _rope_tables(positions)
    for layer in range(depth):
        x, w_gate_up16, w_down16 = _mixer_call(x, rope, params, layer)
        x = _ffn_call(x, w_gate_up16, w_down16, params, layer,
                      apply_final_norm=(layer == depth - 1))
    return x
```

```python
import functools

import jax
import jax.numpy as jnp
from jax import lax
from jax.experimental import pallas as pl
from jax.experimental.pallas import tpu as pltpu

D_MODEL = 1024
MIX = 512
NORM_EPS = 1e-6

SGU_GROUPS = 4
BLOCK = 128

SWA_HEADS = 8
SWA_KV = 2
SWA_HD = 64
ROPE_THETA = 500000.0
ROPE_DIM = 16
ROPE_PACK = 64

DN_HEADS = 4
DN_HD = 128
DN_CONV = 4
DN_CHUNK = 64

D_FF = 2816

LANES = 128
SUBLANES = 8
MXU_DIM = 256

C_UV = 0
C_QKVB = C_UV + 2 * MIX
C_QKVC = C_QKVB + MIX + 2 * SWA_KV * SWA_HD
C_Z = C_QKVC + 3 * MIX
C_FRONT_END = C_Z + MIX
GATE_OFF = 2 * DN_HEADS

MIXER_TILE = 256
FFN_TILE = 512
WD_CAST_ROWS = 64
VMEM_CAPACITY_BYTES = 64 * 1024 * 1024
VMEM_LIMIT_BYTES = VMEM_CAPACITY_BYTES // 8 * 7

BF16 = jnp.bfloat16
F32 = jnp.float32
_GELU_C = 0.7978845608028654


def _dot(a, b):
    return jnp.dot(a.astype(BF16), b.astype(BF16), preferred_element_type=F32)


def _dot_nt(a, b):
    return lax.dot_general(a.astype(BF16), b.astype(BF16), (((1,), (1,)), ((), ())),
                           preferred_element_type=F32)


def _rmsnorm(x, g):
    return x * lax.rsqrt(jnp.mean(x * x, axis=-1, keepdims=True) + NORM_EPS) * g


def _gelu_tanh(x):
    inner = x * (_GELU_C + (_GELU_C * 0.044715) * (x * x))
    return (0.5 * x) * (1.0 + jnp.tanh(inner))


def _silu(x):
    return x * jax.nn.sigmoid(x)


def _softplus(x):
    return jnp.maximum(x, 0.0) + jnp.log(1.0 + jnp.exp(-jnp.abs(x)))


def _iota(shape, dim):
    return lax.broadcasted_iota(jnp.int32, shape, dim)


def _spatial_gating_prep(u_pre, v_pre, ln_g, ln_b, u_scr, vn_scr):
    u_scr[...] = _gelu_tanh(u_pre)
    v = _gelu_tanh(v_pre)
    vc = v - jnp.mean(v, axis=-1, keepdims=True)
    vn_scr[...] = (vc * lax.rsqrt(jnp.mean(vc * vc, axis=-1, keepdims=True) + NORM_EPS) * ln_g
                   + ln_b).astype(BF16)


def _spatial_gating_mix(u_scr, vn_scr, sguw_ref, sgub_ref):
    u = u_scr[...]
    vn = vn_scr[...]
    t = u.shape[0]
    causal = _iota((BLOCK, BLOCK), 0) >= _iota((BLOCK, BLOCK), 1)
    groups = []
    for g in range(SGU_GROUPS):
        w = jnp.where(causal, sguw_ref[g], 0.0).astype(BF16)
        bias = sgub_ref[:, g:g + 1]
        cols = slice(g * LANES, (g + 1) * LANES)
        mixed = [_dot(w, vn[blk * BLOCK:(blk + 1) * BLOCK, cols]) + bias
                 for blk in range(t // BLOCK)]
        groups.append(u[:, cols] * jnp.concatenate(mixed, axis=0))
    return jnp.concatenate(groups, axis=1)


def _rope(x, cos, sin):
    half = ROPE_DIM // 2
    first_half = (_iota((1, LANES), 1) % SWA_HD) < half
    sin_a = jnp.where(first_half, -sin, 0.0)
    sin_b = jnp.where(first_half, 0.0, sin)
    n = x.shape[1] // LANES
    if n > 1:
        cos = jnp.concatenate([cos] * n, axis=1)
        sin_a = jnp.concatenate([sin_a] * n, axis=1)
        sin_b = jnp.concatenate([sin_b] * n, axis=1)
    width = x.shape[1]
    return (x * cos + pltpu.roll(x, width - half, 1) * sin_a + pltpu.roll(x, half, 1) * sin_b)


def _sliding_window_attention(q, k, v, cos, sin, sinks_ref, ones, kv_scr, probs_scr, seq_start):
    t = q.shape[0]
    n_blk = t // BLOCK
    group = SWA_HEADS // SWA_KV
    q = _rope(q, cos, sin) * (SWA_HD ** -0.5)
    k = _rope(k, cos, sin)

    kcat = jnp.concatenate([kv_scr[0], k], axis=0)
    vcat = jnp.concatenate([kv_scr[1], v], axis=0)

    lane = _iota((1, LANES), 1)
    lo = lane < SWA_HD
    krot = pltpu.roll(kcat, SWA_HD, 1)
    vrot = pltpu.roll(vcat, SWA_HD, 1)
    kdup = [jnp.where(lo, kcat, krot).astype(BF16), jnp.where(lo, krot, kcat).astype(BF16)]
    vlo = [jnp.where(lo, vcat, 0.0).astype(BF16), jnp.where(lo, vrot, 0.0).astype(BF16)]
    vhi = [jnp.where(lo, 0.0, vrot).astype(BF16), jnp.where(lo, 0.0, vcat).astype(BF16)]

    rows4 = group * BLOCK
    cur_mask = (_iota((rows4, BLOCK), 0) % BLOCK) >= _iota((rows4, BLOCK), 1)
    neg_inf = jnp.float32(-jnp.inf)
    units = [(blk, kh) for blk in range(n_blk) for kh in range(SWA_KV)]

    lhs = []
    for blk, kh in units:
        rows = slice(blk * BLOCK, (blk + 1) * BLOCK)
        s0 = q[rows, (2 * kh) * LANES:(2 * kh + 1) * LANES]
        s1 = q[rows, (2 * kh + 1) * LANES:(2 * kh + 2) * LANES]
        lhs.append(jnp.concatenate([jnp.where(lo, s0, 0.0), jnp.where(lo, s1, 0.0),
                                    jnp.where(lo, 0.0, s0), jnp.where(lo, 0.0, s1)],
                                   axis=0).astype(BF16))
    logits = [_dot_nt(lhs[i], kdup[kh][blk * BLOCK:(blk + 2) * BLOCK])
              for i, (blk, kh) in enumerate(units)]
    yield
    n_unit = len(units)
    sinks = [jnp.concatenate(
        [jnp.broadcast_to(sinks_ref[group * kh + 2 * j + half:group * kh + 2 * j + half + 1, :],
                          (BLOCK, LANES)) for half in range(2) for j in range(2)], axis=0)
        for kh in range(SWA_KV)]
    act = []
    for i, (blk, kh) in enumerate(units):
        l_prev = logits[i][:, :BLOCK]
        if blk == 0:
            l_prev = jnp.where(seq_start, neg_inf, l_prev)
        act.append(jnp.where(cur_mask, logits[i][:, BLOCK:], l_prev))
    m = [jnp.maximum(jnp.broadcast_to(jnp.max(act[i], axis=1, keepdims=True), act[i].shape),
                     sinks[units[i][1]]) for i in range(n_unit)]
    p = [jnp.exp(act[i] - m[i]) for i in range(n_unit)]
    rows_u = p[0].shape[0]
    sums = _dot(jnp.concatenate(p, axis=0), ones)
    den = [sums[i * rows_u:(i + 1) * rows_u] + jnp.exp(sinks[units[i][1]] - m[i])
           for i in range(n_unit)]
    for i in range(n_unit):
        pn = p[i] * (1.0 / den[i])
        probs_scr[i] = jnp.concatenate([jnp.where(cur_mask, 0.0, pn), jnp.where(cur_mask, pn, 0.0)],
                                       axis=1).astype(BF16)
    yield
    probs = [probs_scr[i] for i in range(len(units))]
    out_rows = []
    for blk in range(n_blk):
        slabs = []
        for kh in range(SWA_KV):
            i = blk * SWA_KV + kh
            keys = slice(blk * BLOCK, (blk + 2) * BLOCK)
            res = (_dot(probs[i][:2 * BLOCK], vlo[kh][keys])
                   + _dot(probs[i][2 * BLOCK:], vhi[kh][keys]))
            slabs += [res[:BLOCK], res[BLOCK:]]
        out_rows.append(jnp.concatenate(slabs, axis=1))
    kv_scr[0] = k[t - BLOCK:, :]
    kv_scr[1] = v[t - BLOCK:, :]
    yield jnp.concatenate(out_rows, axis=0)


def _chunk_cumsum(x):
    row = _iota(x.shape, 0) % DN_CHUNK
    shift = 1
    while shift < DN_CHUNK:
        x = x + jnp.where(row >= shift, pltpu.roll(x, shift, 0), 0.0)
        shift *= 2
    return x


def _head_sums(xs, ones_bd):
    t = xs[0].shape[0]
    n_half = xs[0].shape[1] // MXU_DIM
    stacked = jnp.concatenate([(x * x)[:, i * MXU_DIM:(i + 1) * MXU_DIM]
                               for x in xs for i in range(n_half)], axis=0)
    sums = _dot(stacked, ones_bd)
    return [jnp.concatenate([sums[(j * n_half + i) * t:(j * n_half + i + 1) * t]
                             for i in range(n_half)], axis=1) for j in range(len(xs))]


def _gated_deltanet(qkv, get_z, logits, convw_ref, alog, dtb, dnorm, ones_bd,
                    conv_scr, state_scr, fill):
    t = qkv.shape[0]
    c = DN_CHUNK
    n_chunk = t // c
    pad = SUBLANES
    heads = range(DN_HEADS)
    hs = [slice(h * DN_HD, (h + 1) * DN_HD) for h in heads]

    xa = jnp.concatenate([conv_scr[0:pad, :], qkv], axis=0)
    y = qkv * convw_ref[DN_CONV - 1:DN_CONV, :]
    for i in range(DN_CONV - 1):
        y = y + pltpu.roll(xa, DN_CONV - 1 - i, 0)[pad:, :] * convw_ref[i:i + 1, :]
    conv_scr[0:pad, :] = qkv[t - pad:, :]
    y = _silu(y)

    q = y[:, :MIX]
    k = y[:, MIX:2 * MIX]
    v = y[:, 2 * MIX:]
    fill("prologue")
    q_ss, k_ss = _head_sums([q, k], ones_bd)
    q = q * lax.rsqrt(q_ss + NORM_EPS) * (DN_HD ** -0.5)
    k = k * lax.rsqrt(k_ss + NORM_EPS)

    def spread(slab, lo):
        return jnp.concatenate([jnp.broadcast_to(slab[:, lo + j:lo + j + 1], (t, DN_HD))
                                for j in heads], axis=1)

    dl = DN_HEADS
    gc_c = _chunk_cumsum(-jnp.exp(alog) * _softplus(logits + dtb))
    last = [gc_c[(ci + 1) * c - 1:(ci + 1) * c, :] for ci in range(n_chunk)]
    g_last_c = jnp.concatenate([jnp.broadcast_to(r, (c, LANES)) for r in last], axis=0)
    beta = spread(jax.nn.sigmoid(logits), 0)
    gc = spread(gc_c, dl)
    egc = spread(jnp.exp(gc_c), dl)
    kb = k * beta
    vb = v * beta
    kbe = kb * egc
    qd = q * egc
    kd = k * spread(jnp.exp(g_last_c - gc_c), dl)
    c_dec = [jnp.exp(r) for r in last]

    g4_t = gc_c.T
    kd_t = [kd[:, hs[h]].T.astype(BF16) for h in heads]
    fill("norms")

    ri = _iota((t, t), 0)
    ci_ = _iota((t, t), 1)
    same = (ri // c) == (ci_ // c)
    incl = jnp.logical_and(same, ri >= ci_)
    strict = jnp.logical_and(same, ri > ci_)
    neg_inf = jnp.float32(-jnp.inf)

    aq = [_dot_nt(jnp.concatenate([kb[:, hs[h]], q[:, hs[h]]], axis=0), k[:, hs[h]])
          for h in heads]
    lower, attn, sol = [], [], []
    for h in heads:
        g_col = gc[:, hs[h]]
        diff = jnp.concatenate([g_col] * (t // DN_HD), axis=1) - g4_t[dl + h:dl + h + 1, :]
        decay = jnp.exp(jnp.where(incl, diff, neg_inf))
        lower.append(jnp.where(strict, aq[h][:t] * decay, 0.0).astype(BF16))
        attn.append((aq[h][t:] * decay).astype(BF16))
        sol.append(jnp.concatenate([vb[:, hs[h]], kbe[:, hs[h]]], axis=1))
    fill("decay")

    sol = [sol[h] - _dot(lower[h], sol[h]) for h in heads]
    power = [_dot(lower[h], lower[h]).astype(BF16) for h in heads]
    n_sq = 2
    while True:
        sol = [sol[h] + _dot(power[h], sol[h]) for h in heads]
        if 2 * n_sq >= c:
            break
        power = [_dot(power[h], power[h]).astype(BF16) for h in heads]
        n_sq *= 2

    row_chunk = _iota((2 * c, 1), 0) // c
    gn = []
    for h in heads:
        per_chunk = []
        for ci in range(n_chunk):
            pair = slice((ci // 2) * 2 * c, (ci // 2 + 1) * 2 * c)
            rhs = jnp.where(row_chunk == (ci % 2), sol[h][pair], 0.0)
            per_chunk.append(_dot(kd_t[h][:, pair], rhs))
        gn.append(per_chunk)

    au = [_dot(attn[h], sol[h]) for h in heads]
    q_eff = [qd[:, hs[h]] - au[h][:, DN_HD:] for h in heads]

    def scan_and_outputs():
        state = [state_scr[h] for h in heads]
        from_state = [[] for _ in heads]
        for ci in range(n_chunk):
            rows = slice(ci * c, (ci + 1) * c)
            fill(("scan", ci))
            both = [_dot(jnp.concatenate([q_eff[h][rows], gn[h][ci][:, DN_HD:]], axis=0), state[h])
                    for h in heads]
            for h in heads:
                from_state[h].append(both[h][:c])
            state = [state[h] * c_dec[ci][:, dl + h:dl + h + 1]
                     + (gn[h][ci][:, :DN_HD] - both[h][c:]) for h in heads]

        outs = [jnp.concatenate(from_state[h], axis=0) + au[h][:, :DN_HD] for h in heads]
        o = jnp.concatenate(outs, axis=1)
        o = (o * lax.rsqrt(_head_sums([o], ones_bd)[0] * (1.0 / DN_HD) + NORM_EPS) * dnorm
             * _silu(get_z()))
        for h in heads:
            state_scr[h] = state[h]
        return o

    return scan_and_outputs


def _mixer_kernel(x_ref, rope_ref, expand_ref, ones_ref, anorm_ref, wfront_ref, wtail_ref,
                  lng_ref, lnb_ref, sguw_ref, sgub_ref, sinks_ref, convw_ref, alog_ref, dtb_ref,
                  dnorm_ref, wbr_ref, wout_ref, wgu32_ref, wd32_ref, o_ref, wgu16_ref, wd16_ref,
                  kv_scr, conv_scr, state_scr, gate_scr, wgate_scr, wbr16_scr, wout16_scr, u_scr,
                  vn_scr, probs_scr):
    seq_start = pl.program_id(1) == 0

    @pl.when(jnp.logical_and(pl.program_id(0) == 0, seq_start))
    def _():
        for n in range(3):
            window = wtail_ref[:, n * D_MODEL:(n + 1) * D_MODEL + GATE_OFF].astype(F32)
            wgate_scr[:, n * D_MODEL:(n + 1) * D_MODEL] = window[:, GATE_OFF:].astype(BF16)
        for n in range(3):
            wbr16_scr[n] = wbr_ref[n].astype(BF16)
        wout16_scr[...] = wout_ref[...].astype(BF16)

    @pl.when(seq_start)
    def _():
        kv_scr[...] = jnp.zeros_like(kv_scr)
        conv_scr[0:SUBLANES, :] = jnp.zeros((SUBLANES, 3 * MIX), F32)
        state_scr[...] = jnp.zeros_like(state_scr)

    x = x_ref[...]
    h = _rmsnorm(x, anorm_ref[...]).astype(BF16)

    def proj(w_ref, lo, width):
        return jnp.dot(h, w_ref[:, lo:lo + width], preferred_element_type=F32)

    def gate(n, lo=0, width=D_MODEL):
        return jax.nn.sigmoid(proj(wgate_scr, n * D_MODEL + lo, width))

    park = {}

    def fill(stage):
        if stage == "prologue":
            qkv_b = proj(wfront_ref, C_QKVB, MIX + 2 * SWA_KV * SWA_HD)
            cos_sin = jnp.dot(rope_ref[...], expand_ref[...], preferred_element_type=F32)
            park["swa"] = _sliding_window_attention(
                qkv_b[:, :MIX], qkv_b[:, MIX:MIX + LANES], qkv_b[:, MIX + LANES:],
                cos_sin[:, :LANES], cos_sin[:, LANES:], sinks_ref, ones_ref[0:LANES, 0:LANES], kv_scr,
                probs_scr, seq_start)
            park["uv"] = proj(wfront_ref, C_UV, 2 * MIX)
            gate_scr[0] = gate(0)
            next(park["swa"])
        elif stage == "norms":
            gate_scr[1] = gate(1)
        elif stage == "decay":
            park["z"] = proj(wfront_ref, C_Z, MIX)
            next(park["swa"])
            uv = park.pop("uv")
            _spatial_gating_prep(uv[:, :MIX], uv[:, MIX:], lng_ref[...], lnb_ref[...],
                                 u_scr, vn_scr)
        elif stage[0] == "scan":
            lo = stage[1] * 2 * LANES
            gate_scr[2, :, lo:lo + 2 * LANES] = gate(2, lo, 2 * LANES)

    dn_tail = _gated_deltanet(proj(wfront_ref, C_QKVC, 3 * MIX), lambda: park.pop("z"),
                              proj(wtail_ref, 0, LANES), convw_ref,
                              alog_ref[...], dtb_ref[...], dnorm_ref[...], ones_ref[...],
                              conv_scr, state_scr, fill)

    out_c = dn_tail()
    out_a = _spatial_gating_mix(u_scr, vn_scr, sguw_ref, sgub_ref)
    out_b = next(park.pop("swa"))
    merged = (gate_scr[0] * _dot(out_a, wbr16_scr[0]) + gate_scr[1] * _dot(out_b, wbr16_scr[1])
              + gate_scr[2] * _dot(out_c, wbr16_scr[2]))
    o_ref[...] = x + _dot(merged, wout16_scr[...])
    wgu16_ref[...] = wgu32_ref[...].astype(BF16)
    wd16_ref[...] = wd32_ref[...].astype(BF16)


def _ffn_kernel(x_ref, fnorm_ref, wgu_ref, wd_ref, final_ref, o_ref, *, apply_final_norm):
    x = x_ref[...]
    h = _rmsnorm(x, fnorm_ref[...]).astype(BF16)
    gu = jnp.dot(h, wgu_ref[...], preferred_element_type=F32)
    act = _silu(gu[:, :D_FF]) * gu[:, D_FF:]
    y = x + _dot(act, wd_ref[...])
    if apply_final_norm:
        y = _rmsnorm(y, final_ref[...])
    o_ref[...] = y


def _resident(shape_tail, layer):
    n = len(shape_tail)
    return pl.BlockSpec((None,) + tuple(shape_tail), lambda *_: (layer,) + (0,) * n,
                        pipeline_mode=pl.Buffered(1))


def _mixer_call(x, rope, params, layer):
    batch, seq, _ = x.shape
    t = MIXER_TILE
    n_t = seq // t
    tok = lambda w: pl.BlockSpec((None, t, w), lambda b, i: (b, i, 0))
    gu_rows = D_MODEL // (batch * n_t)
    wd_steps = D_FF // WD_CAST_ROWS
    assert gu_rows % 16 == 0 and wd_steps <= batch * n_t
    gu_slice = lambda b, i: (b * n_t + i, 0)
    wd_slice = lambda b, i: (jnp.minimum(b * n_t + i, wd_steps - 1), 0)
    in_specs = [
        tok(D_MODEL), tok(ROPE_PACK),
        pl.BlockSpec((ROPE_PACK, 2 * LANES), lambda b, i: (0, 0), pipeline_mode=pl.Buffered(1)),
        pl.BlockSpec((MXU_DIM, MXU_DIM), lambda b, i: (0, 0), pipeline_mode=pl.Buffered(1)),
        _resident((1, D_MODEL), layer),
        _resident((D_MODEL, C_FRONT_END), layer),
        pl.BlockSpec((None, D_MODEL, C_FRONT_END), lambda *_: (layer, 0, 1),
                     pipeline_mode=pl.Buffered(1)),
        _resident((1, MIX), layer), _resident((1, MIX), layer),
        _resident((SGU_GROUPS, BLOCK, BLOCK), layer), _resident((BLOCK, SGU_GROUPS), layer),
        _resident((SWA_HEADS, LANES), layer),
        _resident((DN_CONV, 3 * MIX), layer),
        _resident((1, LANES), layer), _resident((1, LANES), layer), _resident((1, MIX), layer),
        _resident((3, MIX, D_MODEL), layer), _resident((D_MODEL, D_MODEL), layer),
        pl.BlockSpec((None, gu_rows, 2 * D_FF), lambda b, i: (layer,) + gu_slice(b, i)),
        pl.BlockSpec((None, WD_CAST_ROWS, D_MODEL), lambda b, i: (layer,) + wd_slice(b, i)),
    ]
    return pl.pallas_call(
        _mixer_kernel,
        out_shape=(jax.ShapeDtypeStruct(x.shape, F32),
                   jax.ShapeDtypeStruct((D_MODEL, 2 * D_FF), BF16),
                   jax.ShapeDtypeStruct((D_FF, D_MODEL), BF16)),
        grid=(batch, seq // t),
        in_specs=in_specs,
        out_specs=(tok(D_MODEL), pl.BlockSpec((gu_rows, 2 * D_FF), gu_slice),
                   pl.BlockSpec((WD_CAST_ROWS, D_MODEL), wd_slice)),
        scratch_shapes=[
            pltpu.VMEM((2, BLOCK, LANES), F32),
            pltpu.VMEM((SUBLANES, 3 * MIX), F32),
            pltpu.VMEM((DN_HEADS, DN_HD, DN_HD), F32),
            pltpu.VMEM((3, t, D_MODEL), F32),
            pltpu.VMEM((D_MODEL, 3 * D_MODEL), BF16),
            pltpu.VMEM((3, MIX, D_MODEL), BF16),
            pltpu.VMEM((D_MODEL, D_MODEL), BF16),
            pltpu.VMEM((t, MIX), F32),
            pltpu.VMEM((t, MIX), BF16),
            pltpu.VMEM((t // BLOCK * SWA_KV, SWA_HEADS // SWA_KV * BLOCK, 2 * BLOCK), BF16),
        ],
        compiler_params=pltpu.CompilerParams(
            dimension_semantics=("arbitrary", "arbitrary"),
            vmem_limit_bytes=VMEM_LIMIT_BYTES),
        name=f"mixer_l{layer}",
    )(x, *rope, params["head_ones"], params["attn_norm"], params["w_in"],
      params["w_in"], params["sgu_ln_g"], params["sgu_ln_b"],
      params["sgu_w"], params["sgu_b"], params["attn_sinks"], params["dn_conv_w"],
      params["dn_a_log"], params["dn_dt_bias"], params["dn_norm"], params["w_branch"],
      params["w_out"], params["w_gate_up"], params["w_down"])


def _ffn_call(x, w_gate_up16, w_down16, params, layer, apply_final_norm):
    batch, seq, _ = x.shape
    t = FFN_TILE
    tok = pl.BlockSpec((None, t, D_MODEL), lambda b, i: (b, i, 0))
    return pl.pallas_call(
        functools.partial(_ffn_kernel, apply_final_norm=apply_final_norm),
        out_shape=jax.ShapeDtypeStruct(x.shape, F32),
        grid=(batch, seq // t),
        in_specs=[
            tok,
            _resident((1, D_MODEL), layer),
            pl.BlockSpec((D_MODEL, 2 * D_FF), lambda b, i: (0, 0), pipeline_mode=pl.Buffered(1)),
            pl.BlockSpec((D_FF, D_MODEL), lambda b, i: (0, 0), pipeline_mode=pl.Buffered(1)),
            pl.BlockSpec((1, D_MODEL), lambda b, i: (0, 0)),
        ],
        out_specs=tok,
        compiler_params=pltpu.CompilerParams(
            dimension_semantics=("arbitrary", "arbitrary"),
            vmem_limit_bytes=VMEM_LIMIT_BYTES),
        name=f"ffn_l{layer}",
    )(x, params["ffn_norm"], w_gate_up16, w_down16, params["final_norm"])


def _rope_tables(positions):
    half = ROPE_DIM // 2
    inv_freq = ROPE_THETA ** (-jnp.arange(0, ROPE_DIM, 2, dtype=F32) / ROPE_DIM)
    ang = positions.astype(F32)[..., None] * inv_freq
    cos, sin = lax.optimization_barrier((jnp.cos(ang), jnp.sin(ang)))
    cs = jnp.concatenate([cos, sin], axis=-1)
    hi = lax.reduce_precision(cs, exponent_bits=8, mantissa_bits=7)
    mid = lax.reduce_precision(cs - hi, exponent_bits=8, mantissa_bits=7)
    lo = cs - hi - mid
    ones = jnp.ones(cs.shape[:-1] + (1,), F32)
    pad = jnp.zeros(cs.shape[:-1] + (ROPE_PACK - 3 * ROPE_DIM - 1,), F32)
    packed = jnp.concatenate([hi, mid, lo, ones, pad], axis=-1).astype(BF16)
    lane = jnp.arange(LANES)
    rotary = (lane % SWA_HD) < ROPE_DIM
    src = jnp.arange(ROPE_DIM)[:, None]
    cos_cols = rotary[None, :] & ((lane % half)[None, :] == src)
    sin_cols = rotary[None, :] & ((lane % half)[None, :] == src - half)
    piece = jnp.concatenate([cos_cols, sin_cols], axis=1)
    ones_row = jnp.concatenate([~rotary, jnp.zeros((LANES,), bool)])[None, :]
    expand = jnp.concatenate([piece, piece, piece, ones_row,
                              jnp.zeros((ROPE_PACK - 3 * ROPE_DIM - 1, 2 * LANES), bool)], axis=0)
    return packed, expand.astype(BF16)


def _pack_params(attn_norm, w_in, sgu_ln_g, sgu_ln_b, sgu_w, sgu_b, attn_sinks, dn_conv_w,
                 dn_a_log, dn_dt_bias, dn_norm, w_branch, w_out, ffn_norm, w_gate_up, w_down,
                 final_norm):
    depth = w_in.shape[0]
    decay_lanes = lambda a: jnp.pad(a, ((0, 0), (DN_HEADS, LANES - 2 * DN_HEADS)))[:, None, :]
    lane_head = jnp.arange(MXU_DIM) // DN_HD
    w_in16 = w_in.astype(BF16)
    return {
        "head_ones": (lane_head[:, None] == lane_head[None, :]).astype(BF16),
        "attn_norm": attn_norm[:, None, :],
        "w_in": w_in16,
        "sgu_ln_g": sgu_ln_g[:, None, :],
        "sgu_ln_b": sgu_ln_b[:, None, :],
        "sgu_w": sgu_w,
        "sgu_b": jnp.swapaxes(sgu_b, 1, 2),
        "attn_sinks": jnp.broadcast_to(attn_sinks[:, :, None], (depth, SWA_HEADS, LANES)),
        "dn_conv_w": dn_conv_w,
        "dn_a_log": decay_lanes(dn_a_log),
        "dn_dt_bias": decay_lanes(dn_dt_bias),
        "dn_norm": jnp.tile(dn_norm, (1, DN_HEADS))[:, None, :],
        "w_branch": w_branch,
        "w_out": w_out,
        "ffn_norm": ffn_norm[:, None, :],
        "w_gate_up": w_gate_up,
        "w_down": w_down,
        "final_norm": final_norm[None, :],
    }


def kernel(x, positions, attn_norm, w_in, sgu_ln_g, sgu_ln_b, sgu_w, sgu_b, attn_sinks,
           dn_conv_w, dn_a_log, dn_dt_bias, dn_norm, w_branch, w_out, ffn_norm, w_gate_up,
           w_down, final_norm):
    assert MIXER_TILE == MXU_DIM == 4 * DN_CHUNK and GATE_OFF + 3 * D_MODEL <= C_FRONT_END
    assert x.shape[1] % MIXER_TILE == 0 and x.shape[1] % FFN_TILE == 0
    assert x.shape[2] == D_MODEL
    assert w_in.shape[2] == C_FRONT_END + GATE_OFF + 3 * D_MODEL
    depth = w_in.shape[0]
    params = _pack_params(attn_norm, w_in, sgu_ln_g, sgu_ln_b, sgu_w, sgu_b, attn_sinks,
                          dn_conv_w, dn_a_log, dn_dt_bias, dn_norm, w_branch, w_out, ffn_norm,
                          w_gate_up, w_down, final_norm)
    rope = _rope_tables(positions)
    for layer in range(depth):
        x, w_gate_up16, w_down16 = _mixer_call(x, rope, params, layer)
        x = _ffn_call(x, w_gate_up16, w_down16, params, layer,
                      apply_final_norm=(layer == depth - 1))
    return x
```

```python
import functools

import jax
import jax.numpy as jnp
from jax import lax
from jax.experimental import pallas as pl
from jax.experimental.pallas import tpu as pltpu

D_MODEL = 1024
MIX = 512
NORM_EPS = 1e-6

SGU_GROUPS = 4
BLOCK = 128

SWA_HEADS = 8
SWA_KV = 2
SWA_HD = 64
ROPE_THETA = 500000.0
ROPE_DIM = 16
ROPE_PACK = 64

DN_HEADS = 4
DN_HD = 128
DN_CONV = 4
DN_CHUNK = 64

D_FF = 2816

LANES = 128
SUBLANES = 8
MXU_DIM = 256

C_UV = 0
C_QKVB = C_UV + 2 * MIX
C_QKVC = C_QKVB + MIX + 2 * SWA_KV * SWA_HD
C_Z = C_QKVC + 3 * MIX
C_FRONT_END = C_Z + MIX
GATE_OFF = 2 * DN_HEADS

MIXER_TILE = 256
FFN_TILE = 512
WD_CAST_ROWS = 64
VMEM_CAPACITY_BYTES = 64 * 1024 * 1024
VMEM_LIMIT_BYTES = VMEM_CAPACITY_BYTES // 8 * 7

BF16 = jnp.bfloat16
F32 = jnp.float32
_GELU_C = 0.7978845608028654


def _dot(a, b):
    return jnp.dot(a.astype(BF16), b.astype(BF16), preferred_element_type=F32)


def _dot_nt(a, b):
    return lax.dot_general(a.astype(BF16), b.astype(BF16), (((1,), (1,)), ((), ())),
                           preferred_element_type=F32)


def _rmsnorm(x, g):
    return x * lax.rsqrt(jnp.mean(x * x, axis=-1, keepdims=True) + NORM_EPS) * g


def _gelu_tanh(x):
    inner = x * (_GELU_C + (_GELU_C * 0.044715) * (x * x))
    return (0.5 * x) * (1.0 + jnp.tanh(inner))


def _silu(x):
    return x * jax.nn.sigmoid(x)


def _softplus(x):
    return jnp.maximum(x, 0.0) + jnp.log(1.0 + jnp.exp(-jnp.abs(x)))


def _iota(shape, dim):
    return lax.broadcasted_iota(jnp.int32, shape, dim)


def _spatial_gating_prep(u_pre, v_pre, ln_g, ln_b, u_scr, vn_scr):
    u_scr[...] = _gelu_tanh(u_pre)
    v = _gelu_tanh(v_pre)
    vc = v - jnp.mean(v, axis=-1, keepdims=True)
    vn_scr[...] = (vc * lax.rsqrt(jnp.mean(vc * vc, axis=-1, keepdims=True) + NORM_EPS) * ln_g
                   + ln_b).astype(BF16)


def _spatial_gating_mix(u_scr, vn_scr, sguw_ref, sgub_ref):
    u = u_scr[...]
    vn = vn_scr[...]
    t = u.shape[0]
    causal = _iota((BLOCK, BLOCK), 0) >= _iota((BLOCK, BLOCK), 1)
    groups = []
    for g in range(SGU_GROUPS):
        w = jnp.where(causal, sguw_ref[g], 0.0).astype(BF16)
        bias = sgub_ref[:, g:g + 1]
        cols = slice(g * LANES, (g + 1) * LANES)
        mixed = [_dot(w, vn[blk * BLOCK:(blk + 1) * BLOCK, cols]) + bias
                 for blk in range(t // BLOCK)]
        groups.append(u[:, cols] * jnp.concatenate(mixed, axis=0))
    return jnp.concatenate(groups, axis=1)


def _rope(x, cos, sin):
    half = ROPE_DIM // 2
    first_half = (_iota((1, LANES), 1) % SWA_HD) < half
    sin_a = jnp.where(first_half, -sin, 0.0)
    sin_b = jnp.where(first_half, 0.0, sin)
    n = x.shape[1] // LANES
    if n > 1:
        cos = jnp.concatenate([cos] * n, axis=1)
        sin_a = jnp.concatenate([sin_a] * n, axis=1)
        sin_b = jnp.concatenate([sin_b] * n, axis=1)
    width = x.shape[1]
    return (x * cos + pltpu.roll(x, width - half, 1) * sin_a + pltpu.roll(x, half, 1) * sin_b)


def _sliding_window_attention(q, k, v, cos, sin, sinks_ref, ones, kv_scr, probs_scr, seq_start):
    t = q.shape[0]
    n_blk = t // BLOCK
    group = SWA_HEADS // SWA_KV
    q = _rope(q, cos, sin) * (SWA_HD ** -0.5)
    k = _rope(k, cos, sin)

    kcat = jnp.concatenate([kv_scr[0], k], axis=0)
    vcat = jnp.concatenate([kv_scr[1], v], axis=0)

    lane = _iota((1, LANES), 1)
    lo = lane < SWA_HD
    krot = pltpu.roll(kcat, SWA_HD, 1)
    vrot = pltpu.roll(vcat, SWA_HD, 1)
    kdup = [jnp.where(lo, kcat, krot).astype(BF16), jnp.where(lo, krot, kcat).astype(BF16)]
    vlo = [jnp.where(lo, vcat, 0.0).astype(BF16), jnp.where(lo, vrot, 0.0).astype(BF16)]
    vhi = [jnp.where(lo, 0.0, vrot).astype(BF16), jnp.where(lo, 0.0, vcat).astype(BF16)]

    rows4 = group * BLOCK
    cur_mask = (_iota((rows4, BLOCK), 0) % BLOCK) >= _iota((rows4, BLOCK), 1)
    neg_inf = jnp.float32(-jnp.inf)
    units = [(blk, kh) for blk in range(n_blk) for kh in range(SWA_KV)]

    lhs = []
    for blk, kh in units:
        rows = slice(blk * BLOCK, (blk + 1) * BLOCK)
        s0 = q[rows, (2 * kh) * LANES:(2 * kh + 1) * LANES]
        s1 = q[rows, (2 * kh + 1) * LANES:(2 * kh + 2) * LANES]
        lhs.append(jnp.concatenate([jnp.where(lo, s0, 0.0), jnp.where(lo, s1, 0.0),
                                    jnp.where(lo, 0.0, s0), jnp.where(lo, 0.0, s1)],
                                   axis=0).astype(BF16))
    logits = [_dot_nt(lhs[i], kdup[kh][blk * BLOCK:(blk + 2) * BLOCK])
              for i, (blk, kh) in enumerate(units)]
    yield
    n_unit = len(units)
    sinks = [jnp.concatenate(
        [jnp.broadcast_to(sinks_ref[group * kh + 2 * j + half:group * kh + 2 * j + half + 1, :],
                          (BLOCK, LANES)) for half in range(2) for j in range(2)], axis=0)
        for kh in range(SWA_KV)]
    act = []
    for i, (blk, kh) in enumerate(units):
        l_prev = logits[i][:, :BLOCK]
        if blk == 0:
            l_prev = jnp.where(seq_start, neg_inf, l_prev)
        act.append(jnp.where(cur_mask, logits[i][:, BLOCK:], l_prev))
    m = [jnp.maximum(jnp.broadcast_to(jnp.max(act[i], axis=1, keepdims=True), act[i].shape),
                     sinks[units[i][1]]) for i in range(n_unit)]
    p = [jnp.exp(act[i] - m[i]) for i in range(n_unit)]
    rows_u = p[0].shape[0]
    sums = _dot(jnp.concatenate(p, axis=0), ones)
    den = [sums[i * rows_u:(i + 1) * rows_u] + jnp.exp(sinks[units[i][1]] - m[i])
           for i in range(n_unit)]
    for i in range(n_unit):
        pn = p[i] * (1.0 / den[i])
        probs_scr[i] = jnp.concatenate([jnp.where(cur_mask, 0.0, pn), jnp.where(cur_mask, pn, 0.0)],
                                       axis=1).astype(BF16)
    yield
    probs = [probs_scr[i] for i in range(len(units))]
    out_rows = []
    for blk in range(n_blk):
        slabs = []
        for kh in range(SWA_KV):
            i = blk * SWA_KV + kh
            keys = slice(blk * BLOCK, (blk + 2) * BLOCK)
            res = (_dot(probs[i][:2 * BLOCK], vlo[kh][keys])
                   + _dot(probs[i][2 * BLOCK:], vhi[kh][keys]))
            slabs += [res[:BLOCK], res[BLOCK:]]
        out_rows.append(jnp.concatenate(slabs, axis=1))
    kv_scr[0] = k[t - BLOCK:, :]
    kv_scr[1] = v[t - BLOCK:, :]
    yield jnp.concatenate(out_rows, axis=0)


def _chunk_cumsum(x):
    row = _iota(x.shape, 0) % DN_CHUNK
    shift = 1
    while shift < DN_CHUNK:
        x = x + jnp.where(row >= shift, pltpu.roll(x, shift, 0), 0.0)
        shift *= 2
    return x


def _head_sums(xs, ones_bd):
    t = xs[0].shape[0]
    n_half = xs[0].shape[1] // MXU_DIM
    stacked = jnp.concatenate([(x * x)[:, i * MXU_DIM:(i + 1) * MXU_DIM]
                               for x in xs for i in range(n_half)], axis=0)
    sums = _dot(stacked, ones_bd)
    return [jnp.concatenate([sums[(j * n_half + i) * t:(j * n_half + i + 1) * t]
                             for i in range(n_half)], axis=1) for j in range(len(xs))]


def _gated_deltanet(qkv, get_z, logits, convw_ref, alog, dtb, dnorm, ones_bd,
                    conv_scr, state_scr, fill):
    t = qkv.shape[0]
    c = DN_CHUNK
    n_chunk = t // c
    pad = SUBLANES
    heads = range(DN_HEADS)
    hs = [slice(h * DN_HD, (h + 1) * DN_HD) for h in heads]

    xa = jnp.concatenate([conv_scr[0:pad, :], qkv], axis=0)
    y = qkv * convw_ref[DN_CONV - 1:DN_CONV, :]
    for i in range(DN_CONV - 1):
        y = y + pltpu.roll(xa, DN_CONV - 1 - i, 0)[pad:, :] * convw_ref[i:i + 1, :]
    conv_scr[0:pad, :] = qkv[t - pad:, :]
    y = _silu(y)

    q = y[:, :MIX]
    k = y[:, MIX:2 * MIX]
    v = y[:, 2 * MIX:]
    fill("prologue")
    q_ss, k_ss = _head_sums([q, k], ones_bd)
    q = q * lax.rsqrt(q_ss + NORM_EPS) * (DN_HD ** -0.5)
    k = k * lax.rsqrt(k_ss + NORM_EPS)

    def spread(slab, lo):
        return jnp.concatenate([jnp.broadcast_to(slab[:, lo + j:lo + j + 1], (t, DN_HD))
                                for j in heads], axis=1)

    dl = DN_HEADS
    gc_c = _chunk_cumsum(-jnp.exp(alog) * _softplus(logits + dtb))
    last = [gc_c[(ci + 1) * c - 1:(ci + 1) * c, :] for ci in range(n_chunk)]
    g_last_c = jnp.concatenate([jnp.broadcast_to(r, (c, LANES)) for r in last], axis=0)
    beta = spread(jax.nn.sigmoid(logits), 0)
    gc = spread(gc_c, dl)
    egc = spread(jnp.exp(gc_c), dl)
    kb = k * beta
    vb = v * beta
    kbe = kb * egc
    qd = q * egc
    kd = k * spread(jnp.exp(g_last_c - gc_c), dl)
    c_dec = [jnp.exp(r) for r in last]

    g4_t = gc_c.T
    kd_t = [kd[:, hs[h]].T.astype(BF16) for h in heads]
    fill("norms")

    ri = _iota((t, t), 0)
    ci_ = _iota((t, t), 1)
    same = (ri // c) == (ci_ // c)
    incl = jnp.logical_and(same, ri >= ci_)
    strict = jnp.logical_and(same, ri > ci_)
    neg_inf = jnp.float32(-jnp.inf)

    aq = [_dot_nt(jnp.concatenate([kb[:, hs[h]], q[:, hs[h]]], axis=0), k[:, hs[h]])
          for h in heads]
    lower, attn, sol = [], [], []
    for h in heads:
        g_col = gc[:, hs[h]]
        diff = jnp.concatenate([g_col] * (t // DN_HD), axis=1) - g4_t[dl + h:dl + h + 1, :]
        decay = jnp.exp(jnp.where(incl, diff, neg_inf))
        lower.append(jnp.where(strict, aq[h][:t] * decay, 0.0).astype(BF16))
        attn.append((aq[h][t:] * decay).astype(BF16))
        sol.append(jnp.concatenate([vb[:, hs[h]], kbe[:, hs[h]]], axis=1))
    fill("decay")

    sol = [sol[h] - _dot(lower[h], sol[h]) for h in heads]
    power = [_dot(lower[h], lower[h]).astype(BF16) for h in heads]
    n_sq = 2
    while True:
        sol = [sol[h] + _dot(power[h], sol[h]) for h in heads]
        if 2 * n_sq >= c:
            break
        power = [_dot(power[h], power[h]).astype(BF16) for h in heads]
        n_sq *= 2

    row_chunk = _iota((2 * c, 1), 0) // c
    gn = []
    for h in heads:
        per_chunk = []
        for ci in range(n_chunk):
            pair = slice((ci // 2) * 2 * c, (ci // 2 + 1) * 2 * c)
            rhs = jnp.where(row_chunk == (ci % 2), sol[h][pair], 0.0)
            per_chunk.append(_dot(kd_t[h][:, pair], rhs))
        gn.append(per_chunk)

    au = [_dot(attn[h], sol[h]) for h in heads]
    q_eff = [qd[:, hs[h]] - au[h][:, DN_HD:] for h in heads]

    def scan_and_outputs():
        state = [state_scr[h] for h in heads]
        from_state = [[] for _ in heads]
        for ci in range(n_chunk):
            rows = slice(ci * c, (ci + 1) * c)
            fill(("scan", ci))
            both = [_dot(jnp.concatenate([q_eff[h][rows], gn[h][ci][:, DN_HD:]], axis=0), state[h])
                    for h in heads]
            for h in heads:
                from_state[h].append(both[h][:c])
            state = [state[h] * c_dec[ci][:, dl + h:dl + h + 1]
                     + (gn[h][ci][:, :DN_HD] - both[h][c:]) for h in heads]

        outs = [jnp.concatenate(from_state[h], axis=0) + au[h][:, :DN_HD] for h in heads]
        o = jnp.concatenate(outs, axis=1)
        o = (o * lax.rsqrt(_head_sums([o], ones_bd)[0] * (1.0 / DN_HD) + NORM_EPS) * dnorm
             * _silu(get_z()))
        for h in heads:
            state_scr[h] = state[h]
        return o

    return scan_and_outputs


def _mixer_kernel(x_ref, rope_ref, expand_ref, ones_ref, anorm_ref, wfront_ref, wtail_ref,
                  lng_ref, lnb_ref, sguw_ref, sgub_ref, sinks_ref, convw_ref, alog_ref, dtb_ref,
                  dnorm_ref, wbr_ref, wout_ref, wgu32_ref, wd32_ref, o_ref, wgu16_ref, wd16_ref,
                  kv_scr, conv_scr, state_scr, gate_scr, wgate_scr, wbr16_scr, wout16_scr, u_scr,
                  vn_scr, probs_scr):
    seq_start = pl.program_id(1) == 0

    @pl.when(jnp.logical_and(pl.program_id(0) == 0, seq_start))
    def _():
        for n in range(3):
            window = wtail_ref[:, n * D_MODEL:(n + 1) * D_MODEL + GATE_OFF].astype(F32)
            wgate_scr[:, n * D_MODEL:(n + 1) * D_MODEL] = window[:, GATE_OFF:].astype(BF16)
        for n in range(3):
            wbr16_scr[n] = wbr_ref[n].astype(BF16)
        wout16_scr[...] = wout_ref[...].astype(BF16)

    @pl.when(seq_start)
    def _():
        kv_scr[...] = jnp.zeros_like(kv_scr)
        conv_scr[0:SUBLANES, :] = jnp.zeros((SUBLANES, 3 * MIX), F32)
        state_scr[...] = jnp.zeros_like(state_scr)

    x = x_ref[...]
    h = _rmsnorm(x, anorm_ref[...]).astype(BF16)

    def proj(w_ref, lo, width):
        return jnp.dot(h, w_ref[:, lo:lo + width], preferred_element_type=F32)

    def gate(n, lo=0, width=D_MODEL):
        return jax.nn.sigmoid(proj(wgate_scr, n * D_MODEL + lo, width))

    park = {}

    def fill(stage):
        if stage == "prologue":
            qkv_b = proj(wfront_ref, C_QKVB, MIX + 2 * SWA_KV * SWA_HD)
            cos_sin = jnp.dot(rope_ref[...], expand_ref[...], preferred_element_type=F32)
            park["swa"] = _sliding_window_attention(
                qkv_b[:, :MIX], qkv_b[:, MIX:MIX + LANES], qkv_b[:, MIX + LANES:],
                cos_sin[:, :LANES], cos_sin[:, LANES:], sinks_ref, ones_ref[0:LANES, 0:LANES], kv_scr,
                probs_scr, seq_start)
            park["uv"] = proj(wfront_ref, C_UV, 2 * MIX)
            gate_scr[0] = gate(0)
            next(park["swa"])
        elif stage == "norms":
            gate_scr[1] = gate(1)
        elif stage == "decay":
            park["z"] = proj(wfront_ref, C_Z, MIX)
            next(park["swa"])
            uv = park.pop("uv")
            _spatial_gating_prep(uv[:, :MIX], uv[:, MIX:], lng_ref[...], lnb_ref[...],
                                 u_scr, vn_scr)
        elif stage[0] == "scan":
            lo = stage[1] * 2 * LANES
            gate_scr[2, :, lo:lo + 2 * LANES] = gate(2, lo, 2 * LANES)

    dn_tail = _gated_deltanet(proj(wfront_ref, C_QKVC, 3 * MIX), lambda: park.pop("z"),
                              proj(wtail_ref, 0, LANES), convw_ref,
                              alog_ref[...], dtb_ref[...], dnorm_ref[...], ones_ref[...],
                              conv_scr, state_scr, fill)

    out_c = dn_tail()
    out_a = _spatial_gating_mix(u_scr, vn_scr, sguw_ref, sgub_ref)
    out_b = next(park.pop("swa"))
    merged = (gate_scr[0] * _dot(out_a, wbr16_scr[0]) + gate_scr[1] * _dot(out_b, wbr16_scr[1])
              + gate_scr[2] * _dot(out_c, wbr16_scr[2]))
    o_ref[...] = x + _dot(merged, wout16_scr[...])
    wgu16_ref[...] = wgu32_ref[...].astype(BF16)
    wd16_ref[...] = wd32_ref[...].astype(BF16)


def _ffn_kernel(x_ref, fnorm_ref, wgu_ref, wd_ref, final_ref, o_ref, *, apply_final_norm):
    x = x_ref[...]
    h = _rmsnorm(x, fnorm_ref[...]).astype(BF16)
    gu = jnp.dot(h, wgu_ref[...], preferred_element_type=F32)
    act = _silu(gu[:, :D_FF]) * gu[:, D_FF:]
    y = x + _dot(act, wd_ref[...])
    if apply_final_norm:
        y = _rmsnorm(y, final_ref[...])
    o_ref[...] = y


def _resident(shape_tail, layer):
    n = len(shape_tail)
    return pl.BlockSpec((None,) + tuple(shape_tail), lambda *_: (layer,) + (0,) * n,
                        pipeline_mode=pl.Buffered(1))


def _mixer_call(x, rope, params, layer):
    batch, seq, _ = x.shape
    t = MIXER_TILE
    n_t = seq // t
    tok = lambda w: pl.BlockSpec((None, t, w), lambda b, i: (b, i, 0))
    gu_rows = D_MODEL // (batch * n_t)
    wd_steps = D_FF // WD_CAST_ROWS
    assert gu_rows % 16 == 0 and wd_steps <= batch * n_t
    gu_slice = lambda b, i: (b * n_t + i, 0)
    wd_slice = lambda b, i: (jnp.minimum(b * n_t + i, wd_steps - 1), 0)
    in_specs = [
        tok(D_MODEL), tok(ROPE_PACK),
        pl.BlockSpec((ROPE_PACK, 2 * LANES), lambda b, i: (0, 0), pipeline_mode=pl.Buffered(1)),
        pl.BlockSpec((MXU_DIM, MXU_DIM), lambda b, i: (0, 0), pipeline_mode=pl.Buffered(1)),
        _resident((1, D_MODEL), layer),
        _resident((D_MODEL, C_FRONT_END), layer),
        pl.BlockSpec((None, D_MODEL, C_FRONT_END), lambda *_: (layer, 0, 1),
                     pipeline_mode=pl.Buffered(1)),
        _resident((1, MIX), layer), _resident((1, MIX), layer),
        _resident((SGU_GROUPS, BLOCK, BLOCK), layer), _resident((BLOCK, SGU_GROUPS), layer),
        _resident((SWA_HEADS, LANES), layer),
        _resident((DN_CONV, 3 * MIX), layer),
        _resident((1, LANES), layer), _resident((1, LANES), layer), _resident((1, MIX), layer),
        _resident((3, MIX, D_MODEL), layer), _resident((D_MODEL, D_MODEL), layer),
        pl.BlockSpec((None, gu_rows, 2 * D_FF), lambda b, i: (layer,) + gu_slice(b, i)),
        pl.BlockSpec((None, WD_CAST_ROWS, D_MODEL), lambda b, i: (layer,) + wd_slice(b, i)),
    ]
    return pl.pallas_call(
        _mixer_kernel,
        out_shape=(jax.ShapeDtypeStruct(x.shape, F32),
                   jax.ShapeDtypeStruct((D_MODEL, 2 * D_FF), BF16),
                   jax.ShapeDtypeStruct((D_FF, D_MODEL), BF16)),
        grid=(batch, seq // t),
        in_specs=in_specs,
        out_specs=(tok(D_MODEL), pl.BlockSpec((gu_rows, 2 * D_FF), gu_slice),
                   pl.BlockSpec((WD_CAST_ROWS, D_MODEL), wd_slice)),
        scratch_shapes=[
            pltpu.VMEM((2, BLOCK, LANES), F32),
            pltpu.VMEM((SUBLANES, 3 * MIX), F32),
            pltpu.VMEM((DN_HEADS, DN_HD, DN_HD), F32),
            pltpu.VMEM((3, t, D_MODEL), F32),
            pltpu.VMEM((D_MODEL, 3 * D_MODEL), BF16),
            pltpu.VMEM((3, MIX, D_MODEL), BF16),
            pltpu.VMEM((D_MODEL, D_MODEL), BF16),
            pltpu.VMEM((t, MIX), F32),
            pltpu.VMEM((t, MIX), BF16),
            pltpu.VMEM((t // BLOCK * SWA_KV, SWA_HEADS // SWA_KV * BLOCK, 2 * BLOCK), BF16),
        ],
        compiler_params=pltpu.CompilerParams(
            dimension_semantics=("arbitrary", "arbitrary"),
            vmem_limit_bytes=VMEM_LIMIT_BYTES),
        name=f"mixer_l{layer}",
    )(x, *rope, params["head_ones"], params["attn_norm"], params["w_in"],
      params["w_in"], params["sgu_ln_g"], params["sgu_ln_b"],
      params["sgu_w"], params["sgu_b"], params["attn_sinks"], params["dn_conv_w"],
      params["dn_a_log"], params["dn_dt_bias"], params["dn_norm"], params["w_branch"],
      params["w_out"], params["w_gate_up"], params["w_down"])


def _ffn_call(x, w_gate_up16, w_down16, params, layer, apply_final_norm):
    batch, seq, _ = x.shape
    t = FFN_TILE
    tok = pl.BlockSpec((None, t, D_MODEL), lambda b, i: (b, i, 0))
    return pl.pallas_call(
        functools.partial(_ffn_kernel, apply_final_norm=apply_final_norm),
        out_shape=jax.ShapeDtypeStruct(x.shape, F32),
        grid=(batch, seq // t),
        in_specs=[
            tok,
            _resident((1, D_MODEL), layer),
            pl.BlockSpec((D_MODEL, 2 * D_FF), lambda b, i: (0, 0), pipeline_mode=pl.Buffered(1)),
            pl.BlockSpec((D_FF, D_MODEL), lambda b, i: (0, 0), pipeline_mode=pl.Buffered(1)),
            pl.BlockSpec((1, D_MODEL), lambda b, i: (0, 0)),
        ],
        out_specs=tok,
        compiler_params=pltpu.CompilerParams(
            dimension_semantics=("arbitrary", "arbitrary"),
            vmem_limit_bytes=VMEM_LIMIT_BYTES),
        name=f"ffn_l{layer}",
    )(x, params["ffn_norm"], w_gate_up16, w_down16, params["final_norm"])


def _rope_tables(positions):
    half = ROPE_DIM // 2
    inv_freq = ROPE_THETA ** (-jnp.arange(0, ROPE_DIM, 2, dtype=F32) / ROPE_DIM)
    ang = positions.astype(F32)[..., None] * inv_freq
    cos, sin = lax.optimization_barrier((jnp.cos(ang), jnp.sin(ang)))
    cs = jnp.concatenate([cos, sin], axis=-1)
    hi = lax.reduce_precision(cs, exponent_bits=8, mantissa_bits=7)
    mid = lax.reduce_precision(cs - hi, exponent_bits=8, mantissa_bits=7)
    lo = cs - hi - mid
    place = lambda a, n: jnp.pad(a, ((0, 0),) * (a.ndim - 1)
                                 + ((n * ROPE_DIM, ROPE_PACK - (n + 1) * ROPE_DIM),))
    one = (jnp.arange(ROPE_PACK) == 3 * ROPE_DIM).astype(F32)
    packed = (place(hi, 0) + place(mid, 1) + place(lo, 2) + one).astype(BF16)
    lane = jnp.arange(LANES)
    rotary = (lane % SWA_HD) < ROPE_DIM
    src = jnp.arange(ROPE_DIM)[:, None]
    cos_cols = rotary[None, :] & ((lane % half)[None, :] == src)
    sin_cols = rotary[None, :] & ((lane % half)[None, :] == src - half)
    piece = jnp.concatenate([cos_cols, sin_cols], axis=1)
    ones_row = jnp.concatenate([~rotary, jnp.zeros((LANES,), bool)])[None, :]
    expand = jnp.concatenate([piece, piece, piece, ones_row,
                              jnp.zeros((ROPE_PACK - 3 * ROPE_DIM - 1, 2 * LANES), bool)], axis=0)
    return packed, expand.astype(BF16)


def _pack_params(attn_norm, w_in, sgu_ln_g, sgu_ln_b, sgu_w, sgu_b, attn_sinks, dn_conv_w,
                 dn_a_log, dn_dt_bias, dn_norm, w_branch, w_out, ffn_norm, w_gate_up, w_down,
                 final_norm):
    depth = w_in.shape[0]
    decay_lanes = lambda a: jnp.pad(a, ((0, 0), (DN_HEADS, LANES - 2 * DN_HEADS)))[:, None, :]
    lane_head = jnp.arange(MXU_DIM) // DN_HD
    w_in16 = w_in.astype(BF16)
    return {
        "head_ones": (lane_head[:, None] == lane_head[None, :]).astype(BF16),
        "attn_norm": attn_norm[:, None, :],
        "w_in": w_in16,
        "sgu_ln_g": sgu_ln_g[:, None, :],
        "sgu_ln_b": sgu_ln_b[:, None, :],
        "sgu_w": sgu_w,
        "sgu_b": jnp.swapaxes(sgu_b, 1, 2),
        "attn_sinks": jnp.broadcast_to(attn_sinks[:, :, None], (depth, SWA_HEADS, LANES)),
        "dn_conv_w": dn_conv_w,
        "dn_a_log": decay_lanes(dn_a_log),
        "dn_dt_bias": decay_lanes(dn_dt_bias),
        "dn_norm": jnp.tile(dn_norm, (1, DN_HEADS))[:, None, :],
        "w_branch": w_branch,
        "w_out": w_out,
        "ffn_norm": ffn_norm[:, None, :],
        "w_gate_up": w_gate_up,
        "w_down": w_down,
        "final_norm": final_norm[None, :],
    }


def kernel(x, positions, attn_norm, w_in, sgu_ln_g, sgu_ln_b, sgu_w, sgu_b, attn_sinks,
           dn_conv_w, dn_a_log, dn_dt_bias, dn_norm, w_branch, w_out, ffn_norm, w_gate_up,
           w_down, final_norm):
    assert MIXER_TILE == MXU_DIM == 4 * DN_CHUNK and GATE_OFF + 3 * D_MODEL <= C_FRONT_END
    assert x.shape[1] % MIXER_TILE == 0 and x.shape[1] % FFN_TILE == 0
    assert x.shape[2] == D_MODEL
    assert w_in.shape[2] == C_FRONT_END + GATE_OFF + 3 * D_MODEL
    depth = w_in.shape[0]
    params = _pack_params(attn_norm, w_in, sgu_ln_g, sgu_ln_b, sgu_w, sgu_b, attn_sinks,
                          dn_conv_w, dn_a_log, dn_dt_bias, dn_norm, w_branch, w_out, ffn_norm,
                          w_gate_up, w_down, final_norm)
    rope = _rope_tables(positions)
    for layer in range(depth):
        x, w_gate_up16, w_down16 = _mixer_call(x, rope, params, layer)
        x = _ffn_call(x, w_gate_up16, w_down16, params, layer,
                      apply_final_norm=(layer == depth - 1))
    return x
```
